```python
import jax, jax.numpy as jnp
from jax import lax
import numpy as np

D_MODEL = 1024
BATCH = 8
SEQ = 2048
DEPTH = 2

EPS = 1e-6
N_BRANCHES = 2

GM_GROUPS = 8
GM_GROUP_CH = D_MODEL // GM_GROUPS
GM_WIDTH = GM_GROUPS * GM_GROUP_CH
GM_CHUNK = 128

MLA_HEADS = 16
QK_NOPE = 64
QK_ROPE = 32
QK_DIM = QK_NOPE + QK_ROPE
V_DIM = 64
Q_LORA = D_MODEL // 2
KV_LORA = D_MODEL // 4
ROPE_THETA = 10000.0
ATT_BLOCK = 128

OFF_U = 0
OFF_V = OFF_U + GM_WIDTH
OFF_CQ = OFF_V + GM_WIDTH
OFF_CKV = OFF_CQ + Q_LORA
OFF_KR = OFF_CKV + KV_LORA
OFF_GATE = OFF_KR + QK_ROPE
D_IN = OFF_GATE + N_BRANCHES * D_MODEL

DENSE_FF = 2816
N_EXPERTS = 8
TOP_K = 2
EXPERT_FF = 2816
NUM_DENSE = (DEPTH + 1) // 2
NUM_MOE = DEPTH // 2

kernel_name = "hybrid_gmlp_mla_gated_moe_block"


def rms_norm(x, g):
    xf = x.astype(jnp.float32)
    y = xf * lax.rsqrt(jnp.mean(xf * xf, axis=-1, keepdims=True) + EPS)
    return (y * g.astype(jnp.float32)).astype(x.dtype)


def layer_norm(x, g, b):
    xf = x.astype(jnp.float32)
    mu = jnp.mean(xf, axis=-1, keepdims=True)
    xc = xf - mu
    y = xc * lax.rsqrt(jnp.mean(xc * xc, axis=-1, keepdims=True) + EPS)
    return (y * g.astype(jnp.float32) + b.astype(jnp.float32)).astype(x.dtype)


def rope_tables(seq_len):
    pos = jnp.arange(seq_len, dtype=jnp.float32)
    inv_freq = ROPE_THETA ** (-jnp.arange(0, QK_ROPE, 2, dtype=jnp.float32) / QK_ROPE)
    ang = pos[:, None] * inv_freq[None, :]
    return jnp.cos(ang), jnp.sin(ang)


def apply_rope(x, cos, sin):
    c = cos[None, :, None, :].astype(x.dtype)
    s = sin[None, :, None, :].astype(x.dtype)
    x1, x2 = jnp.split(x, 2, axis=-1)
    return jnp.concatenate([x1 * c - x2 * s, x2 * c + x1 * s], axis=-1)


def swiglu(h, w_gu, w_down):
    g, u = jnp.split(h @ w_gu, 2, axis=-1)
    return (jax.nn.silu(g) * u) @ w_down


def gmlp_branch(a_u, a_v, v_g, v_b, w_s, b_s):
    B, S, _ = a_u.shape
    u = jax.nn.gelu(a_u)
    v = layer_norm(jax.nn.gelu(a_v), v_g, v_b)
    vc = v.reshape(B, S // GM_CHUNK, GM_CHUNK, GM_GROUPS, GM_GROUP_CH)
    causal = jnp.tril(jnp.ones((GM_CHUNK, GM_CHUNK), dtype=bool))
    ws = jnp.where(causal[None], w_s, jnp.zeros((), w_s.dtype))
    mixed = jnp.einsum('gts,bcsgd->bctgd', ws, vc) + b_s.T[None, None, :, :, None]
    return u * mixed.reshape(B, S, GM_WIDTH)


def causal_block_attention(q, k, v):
    B, S, H, Dq = q.shape
    nb = S // ATT_BLOCK
    scale = 1.0 / float(np.sqrt(Dq))
    qb = q.reshape(B, nb, ATT_BLOCK, H, Dq).transpose(1, 0, 2, 3, 4)
    kpos = jnp.arange(S)

    def one_block(args):
        q_blk, i = args
        qpos = i * ATT_BLOCK + jnp.arange(ATT_BLOCK)
        s = jnp.einsum('bqhd,bkhd->bhqk', q_blk, k, preferred_element_type=jnp.float32) * scale
        s = jnp.where(kpos[None, :] <= qpos[:, None], s, jnp.float32(-1e30))
        p = jax.nn.softmax(s, axis=-1).astype(v.dtype)
        return jnp.einsum('bhqk,bkhd->bqhd', p, v)

    out = lax.map(one_block, (qb, jnp.arange(nb)))
    return out.transpose(1, 0, 2, 3, 4).reshape(B, S, H * v.shape[-1])


def mla_branch(c_q, c_kv, k_r, q_lat_g, w_uq, kv_lat_g, w_ukv, q_norm_g, k_norm_g, cos, sin):
    B, S, _ = c_q.shape
    q = (rms_norm(c_q, q_lat_g) @ w_uq).reshape(B, S, MLA_HEADS, QK_DIM)
    kv = (rms_norm(c_kv, kv_lat_g) @ w_ukv).reshape(B, S, MLA_HEADS, QK_NOPE + V_DIM)
    k_nope, v = kv[..., :QK_NOPE], kv[..., QK_NOPE:]
    k_rope = jnp.broadcast_to(k_r[:, :, None, :], (B, S, MLA_HEADS, QK_ROPE))
    k = jnp.concatenate([k_nope, k_rope], axis=-1)
    q = rms_norm(q, q_norm_g)
    k = rms_norm(k, k_norm_g)
    q = jnp.concatenate([q[..., :QK_NOPE], apply_rope(q[..., QK_NOPE:], cos, sin)], axis=-1)
    k = jnp.concatenate([k[..., :QK_NOPE], apply_rope(k[..., QK_NOPE:], cos, sin)], axis=-1)
    return causal_block_attention(q, k, v)


def moe_swiglu(h, w_router, b_router, w_gu, w_down):
    B, S, D = h.shape
    xf = h.reshape(B * S, D)
    logits = (xf @ w_router).astype(jnp.float32) + b_router.astype(jnp.float32)
    top_vals, top_idx = lax.top_k(logits, TOP_K)
    wts = jax.nn.softmax(top_vals, axis=-1)
    combine = jnp.sum(jax.nn.one_hot(top_idx, N_EXPERTS, dtype=jnp.float32) * wts[..., None], axis=1)
    combine = combine.astype(h.dtype)
    y = jnp.zeros_like(xf)
    for e in range(N_EXPERTS):
        y = y + combine[:, e:e + 1] * swiglu(xf, w_gu[e], w_down[e])
    return y.reshape(B, S, D)


def setup_inputs(seed: int = 0) -> dict:
    key = jax.random.key(seed)
    ks = iter(jax.random.split(key, 32))

    def nrm(shape, fan_in):
        return jax.random.normal(next(ks), shape, jnp.float32) * (fan_in ** -0.5)

    def gain(shape):
        return 1.0 + 0.05 * jax.random.normal(next(ks), shape, jnp.float32)

    L = DEPTH
    return {
        "x": jax.random.normal(next(ks), (BATCH, SEQ, D_MODEL), jnp.float32),
        "mix_norm_g": gain((L, D_MODEL)),
        "w_in": nrm((L, D_MODEL, D_IN), D_MODEL),
        "gm_v_norm_g": gain((L, GM_WIDTH)),
        "gm_v_norm_b": 0.02 * jax.random.normal(next(ks), (L, GM_WIDTH), jnp.float32),
        "gm_w_spatial": nrm((L, GM_GROUPS, GM_CHUNK, GM_CHUNK), GM_CHUNK),
        "gm_b_spatial": gain((L, GM_GROUPS, GM_CHUNK)),
        "gm_w_proj": nrm((L, GM_WIDTH, D_MODEL), GM_WIDTH),
        "mla_q_lat_g": gain((L, Q_LORA)),
        "mla_w_uq": nrm((L, Q_LORA, MLA_HEADS * QK_DIM), Q_LORA),
        "mla_kv_lat_g": gain((L, KV_LORA)),
        "mla_w_ukv": nrm((L, KV_LORA, MLA_HEADS * (QK_NOPE + V_DIM)), KV_LORA),
        "mla_q_norm_g": gain((L, QK_DIM)),
        "mla_k_norm_g": gain((L, QK_DIM)),
        "mla_w_proj": nrm((L, MLA_HEADS * V_DIM, D_MODEL), MLA_HEADS * V_DIM),
        "w_out": nrm((L, D_MODEL, D_MODEL), D_MODEL),
        "ffn_norm_g": gain((L, D_MODEL)),
        "dense_w_gu": nrm((NUM_DENSE, D_MODEL, 2 * DENSE_FF), D_MODEL),
        "dense_w_down": nrm((NUM_DENSE, DENSE_FF, D_MODEL), DENSE_FF),
        "moe_w_router": nrm((NUM_MOE, D_MODEL, N_EXPERTS), D_MODEL),
        "moe_b_router": 0.01 * jax.random.normal(next(ks), (NUM_MOE, N_EXPERTS), jnp.float32),
        "moe_w_gu": nrm((NUM_MOE, N_EXPERTS, D_MODEL, 2 * EXPERT_FF), D_MODEL),
        "moe_w_down": nrm((NUM_MOE, N_EXPERTS, EXPERT_FF, D_MODEL), EXPERT_FF),
    }


def reference(x, mix_norm_g, w_in, gm_v_norm_g, gm_v_norm_b, gm_w_spatial, gm_b_spatial, gm_w_proj,
              mla_q_lat_g, mla_w_uq, mla_kv_lat_g, mla_w_ukv, mla_q_norm_g, mla_k_norm_g, mla_w_proj,
              w_out, ffn_norm_g, dense_w_gu, dense_w_down, moe_w_router, moe_b_router, moe_w_gu, moe_w_down):
    B, S, D = x.shape
    cos, sin = rope_tables(S)
    for l in range(DEPTH):
        h = rms_norm(x, mix_norm_g[l])
        z = h @ w_in[l]
        y_a = gmlp_branch(z[..., OFF_U:OFF_V], z[..., OFF_V:OFF_CQ],
                          gm_v_norm_g[l], gm_v_norm_b[l], gm_w_spatial[l], gm_b_spatial[l]) @ gm_w_proj[l]
        y_b = mla_branch(z[..., OFF_CQ:OFF_CKV], z[..., OFF_CKV:OFF_KR], z[..., OFF_KR:OFF_GATE],
                         mla_q_lat_g[l], mla_w_uq[l], mla_kv_lat_g[l], mla_w_ukv[l],
                         mla_q_norm_g[l], mla_k_norm_g[l], cos, sin) @ mla_w_proj[l]
        gates = jax.nn.sigmoid(z[..., OFF_GATE:].reshape(B, S, N_BRANCHES, D))
        merged = gates[..., 0, :] * y_a + gates[..., 1, :] * y_b
        x = x + merged @ w_out[l]
        h2 = rms_norm(x, ffn_norm_g[l])
        if l % 2 == 0:
            x = x + swiglu(h2, dense_w_gu[l // 2], dense_w_down[l // 2])
        else:
            m = l // 2
            x = x + moe_swiglu(h2, moe_w_router[m], moe_b_router[m], moe_w_gu[m], moe_w_down[m])
    return x
```

```python
import functools

import jax
import jax.numpy as jnp
import numpy as np
from jax import lax
from jax.experimental import pallas as pl
from jax.experimental.pallas import tpu as pltpu

F32 = jnp.float32
BF16 = jnp.bfloat16

EPS = 1e-6
LANES = 128

D_MODEL = 1024
GM_GROUPS = 8
GM_GROUP_CH = 128
GM_WIDTH = 1024
GM_CHUNK = 128
MLA_HEADS = 16
QK_NOPE = 64
QK_ROPE = 32
QK_DIM = 96
V_DIM = 64
Q_LORA = 512
KV_LORA = 256
ROPE_THETA = 10000.0
HEAD_SLOT = LANES
QK_WIDTH = MLA_HEADS * HEAD_SLOT
V_WIDTH = MLA_HEADS * V_DIM
LAT_WIDTH = Q_LORA + KV_LORA + LANES
N_EXPERTS = 8
FF = 2816

TM_IN = 256
TM_QKV = 256
TM_GMLP = 512
TM_MERGE = 512
TM_FFN = 512
TF_FFN = 1408
TM_ROUTE = 1024
ATT_T = 512

VMEM_LIMIT = 56 * 1024 * 1024


def _cparams(sem):
    return pltpu.CompilerParams(dimension_semantics=sem, vmem_limit_bytes=VMEM_LIMIT)


def _rms(xf, g):
    return xf * lax.rsqrt(jnp.mean(xf * xf, axis=-1, keepdims=True) + EPS) * g


def _in_proj_kernel(x_ref, g_ref, w_ref, uv_ref, lat_ref, gate_ref):
    h = _rms(x_ref[...], g_ref[...]).astype(BF16)
    o_lat = 2 * GM_WIDTH
    o_gate = o_lat + LAT_WIDTH
    uv_ref[...] = jnp.dot(h, w_ref[:, :o_lat], preferred_element_type=F32).astype(BF16)
    lat_ref[...] = jnp.dot(h, w_ref[:, o_lat:o_gate], preferred_element_type=F32).astype(BF16)
    gate_ref[...] = jnp.dot(h, w_ref[:, o_gate:], preferred_element_type=F32).astype(BF16)


def _in_proj(x2, g, w1):
    n = x2.shape[0]
    wcols = w1.shape[1]
    return pl.pallas_call(
        _in_proj_kernel,
        grid=(n // TM_IN,),
        in_specs=[
            pl.BlockSpec((TM_IN, D_MODEL), lambda i: (i, 0)),
            pl.BlockSpec((1, D_MODEL), lambda i: (0, 0)),
            pl.BlockSpec((D_MODEL, wcols), lambda i: (0, 0)),
        ],
        out_specs=[
            pl.BlockSpec((TM_IN, 2 * GM_WIDTH), lambda i: (i, 0)),
            pl.BlockSpec((TM_IN, LAT_WIDTH), lambda i: (i, 0)),
            pl.BlockSpec((TM_IN, 2 * D_MODEL), lambda i: (i, 0)),
        ],
        out_shape=[
            jax.ShapeDtypeStruct((n, 2 * GM_WIDTH), BF16),
            jax.ShapeDtypeStruct((n, LAT_WIDTH), BF16),
            jax.ShapeDtypeStruct((n, 2 * D_MODEL), BF16),
        ],
        compiler_params=_cparams(("parallel",)),
        name="in_proj",
    )(x2, g, w1)


def _rope_slot(x, c, sm, sp):
    return x * c + pltpu.roll(x, HEAD_SLOT - QK_ROPE // 2, 1) * sm + pltpu.roll(x, QK_ROPE // 2, 1) * sp


def _qkv_prep_kernel(lat_ref, qg_ref, wq_ref, kvg_ref, wk_ref, wvt_ref, qn_ref, kn_ref,
                     c_ref, sm_ref, sp_ref, q_ref, k_ref, vt_ref):
    lat = lat_ref[...].astype(F32)
    cq = _rms(lat[:, :Q_LORA], qg_ref[...]).astype(BF16)
    ckv = _rms(lat[:, Q_LORA:Q_LORA + KV_LORA], kvg_ref[...]).astype(BF16)
    kr = lat[:, Q_LORA + KV_LORA:]
    c, sm, sp = c_ref[...], sm_ref[...], sp_ref[...]
    qn, kn = qn_ref[...], kn_ref[...]
    scale = 1.0 / float(np.sqrt(QK_DIM))
    q_all = jnp.dot(cq, wq_ref[...], preferred_element_type=F32)
    k_all = jnp.dot(ckv, wk_ref[...], preferred_element_type=F32)
    for h in range(MLA_HEADS):
        sl = slice(h * HEAD_SLOT, (h + 1) * HEAD_SLOT)
        qh = q_all[:, sl]
        qh = qh * lax.rsqrt(jnp.sum(qh * qh, axis=-1, keepdims=True) * (1.0 / QK_DIM) + EPS) * qn
        q_ref[:, sl] = (_rope_slot(qh, c, sm, sp) * scale).astype(BF16)
        kh = k_all[:, sl] + kr
        kh = kh * lax.rsqrt(jnp.sum(kh * kh, axis=-1, keepdims=True) * (1.0 / QK_DIM) + EPS) * kn
        k_ref[:, sl] = _rope_slot(kh, c, sm, sp).astype(BF16)
    vt_ref[...] = lax.dot_general(wvt_ref[...], ckv, (((1,), (1,)), ((), ())),
                                  preferred_element_type=F32).astype(BF16)


def _qkv_prep(lat, qg, wq, kvg, wk, wvt, qn, kn, rc, rsm, rsp, batch, seq):
    n = lat.shape[0]
    tps = seq // TM_QKV
    full = lambda shape: pl.BlockSpec(shape, lambda i: (0,) * len(shape))
    rope = pl.BlockSpec((TM_QKV, HEAD_SLOT), lambda i: (i % tps, 0))
    return pl.pallas_call(
        _qkv_prep_kernel,
        grid=(n // TM_QKV,),
        in_specs=[
            pl.BlockSpec((TM_QKV, LAT_WIDTH), lambda i: (i, 0)),
            full((1, Q_LORA)), full((Q_LORA, QK_WIDTH)),
            full((1, KV_LORA)), full((KV_LORA, QK_WIDTH)), full((V_WIDTH, KV_LORA)),
            full((1, HEAD_SLOT)), full((1, HEAD_SLOT)),
            rope, rope, rope,
        ],
        out_specs=[
            pl.BlockSpec((TM_QKV, QK_WIDTH), lambda i: (i, 0)),
            pl.BlockSpec((TM_QKV, QK_WIDTH), lambda i: (i, 0)),
            pl.BlockSpec((None, V_WIDTH, TM_QKV), lambda i: (i // tps, 0, i % tps)),
        ],
        out_shape=[
            jax.ShapeDtypeStruct((n, QK_WIDTH), BF16),
            jax.ShapeDtypeStruct((n, QK_WIDTH), BF16),
            jax.ShapeDtypeStruct((batch, V_WIDTH, seq), BF16),
        ],
        compiler_params=_cparams(("parallel",)),
        name="qkv_prep",
    )(lat, qg, wq, kvg, wk, wvt, qn, kn, rc, rsm, rsp)


def _attention_kernel(q_ref, k_ref, vt_ref, ot_ref, *, seq):
    nblk = seq // ATT_T
    key_pos = lax.broadcasted_iota(jnp.int32, (ATT_T, ATT_T), 0)
    qry_pos = lax.broadcasted_iota(jnp.int32, (ATT_T, ATT_T), 1)
    causal = key_pos <= qry_pos
    for qi in range(nblk):
        q = q_ref[qi * ATT_T:(qi + 1) * ATT_T, :]
        m = jnp.full((1, ATT_T), -1e30, F32)
        l = jnp.zeros((1, ATT_T), F32)
        acc = jnp.zeros((V_DIM, ATT_T), F32)
        for kj in range(qi + 1):
            k = k_ref[kj * ATT_T:(kj + 1) * ATT_T, :]
            st = lax.dot_general(k, q, (((1,), (1,)), ((), ())), preferred_element_type=F32)
            if kj == qi:
                st = jnp.where(causal, st, -1e30)
            m_new = jnp.maximum(m, jnp.max(st, axis=0, keepdims=True))
            alpha = jnp.exp(m - m_new)
            p = jnp.exp(st - m_new)
            l = alpha * l + jnp.sum(p, axis=0, keepdims=True)
            acc = alpha * acc + jnp.dot(vt_ref[:, kj * ATT_T:(kj + 1) * ATT_T], p.astype(BF16),
                                        preferred_element_type=F32)
            m = m_new
        ot_ref[:, qi * ATT_T:(qi + 1) * ATT_T] = (acc / l).astype(BF16)


def _attention(q3, k3, vt):
    batch, seq, _ = q3.shape
    return pl.pallas_call(
        functools.partial(_attention_kernel, seq=seq),
        grid=(batch, MLA_HEADS),
        in_specs=[
            pl.BlockSpec((None, seq, HEAD_SLOT), lambda b, h: (b, 0, h)),
            pl.BlockSpec((None, seq, HEAD_SLOT), lambda b, h: (b, 0, h)),
            pl.BlockSpec((None, V_DIM, seq), lambda b, h: (b, h, 0)),
        ],
        out_specs=pl.BlockSpec((None, V_DIM, seq), lambda b, h: (b, h, 0)),
        out_shape=jax.ShapeDtypeStruct((batch, V_WIDTH, seq), BF16),
        compiler_params=_cparams(("parallel", "parallel")),
        name="attention",
    )(q3, k3, vt)


def _gmlp_kernel(uv_ref, vg_ref, vb_ref, ws_ref, bst_ref, a_ref):
    v = jax.nn.gelu(uv_ref[:, GM_WIDTH:].astype(F32))
    mu = jnp.mean(v, axis=-1, keepdims=True)
    vc = v - mu
    vn = (vc * lax.rsqrt(jnp.mean(vc * vc, axis=-1, keepdims=True) + EPS) * vg_ref[...] + vb_ref[...]).astype(BF16)
    row = lax.broadcasted_iota(jnp.int32, (GM_CHUNK, GM_CHUNK), 0)
    col = lax.broadcasted_iota(jnp.int32, (GM_CHUNK, GM_CHUNK), 1)
    tril = col <= row
    bst = bst_ref[...]
    for g in range(GM_GROUPS):
        ws = jnp.where(tril, ws_ref[g], 0.0).astype(BF16)
        bias = bst[:, g:g + 1]
        cs = slice(g * GM_GROUP_CH, (g + 1) * GM_GROUP_CH)
        for c in range(TM_GMLP // GM_CHUNK):
            rs = slice(c * GM_CHUNK, (c + 1) * GM_CHUNK)
            mixed = jnp.dot(ws, vn[rs, cs], preferred_element_type=F32) + bias
            u = jax.nn.gelu(uv_ref[rs, cs].astype(F32))
            a_ref[rs, cs] = (u * mixed).astype(BF16)


def _gmlp(uv, vg, vb, ws, bst):
    n = uv.shape[0]
    full = lambda shape: pl.BlockSpec(shape, lambda i: (0,) * len(shape))
    return pl.pallas_call(
        _gmlp_kernel,
        grid=(n // TM_GMLP,),
        in_specs=[
            pl.BlockSpec((TM_GMLP, 2 * GM_WIDTH), lambda i: (i, 0)),
            full((1, GM_WIDTH)), full((1, GM_WIDTH)),
            full((GM_GROUPS, GM_CHUNK, GM_CHUNK)), full((GM_CHUNK, GM_GROUPS)),
        ],
        out_specs=pl.BlockSpec((TM_GMLP, GM_WIDTH), lambda i: (i, 0)),
        out_shape=jax.ShapeDtypeStruct((n, GM_WIDTH), BF16),
        compiler_params=_cparams(("parallel",)),
        name="gmlp",
    )(uv, vg, vb, ws, bst)


def _merge_kernel(a_ref, ot_ref, gate_ref, x_ref, wa_ref, wb_ref, wo_ref, fg_ref, *rest, with_router):
    if with_router:
        wr_ref, xn_ref, h2_ref, lg_ref = rest
    else:
        xn_ref, h2_ref = rest
    ya = jnp.dot(a_ref[...], wa_ref[...], preferred_element_type=F32)
    yb = lax.dot_general(ot_ref[...], wb_ref[...], (((0,), (0,)), ((), ())), preferred_element_type=F32)
    ga = jax.nn.sigmoid(gate_ref[:, :D_MODEL].astype(F32))
    gb = jax.nn.sigmoid(gate_ref[:, D_MODEL:].astype(F32))
    merged = (ga * ya + gb * yb).astype(BF16)
    xn = x_ref[...] + jnp.dot(merged, wo_ref[...], preferred_element_type=F32)
    xn_ref[...] = xn
    h2 = _rms(xn, fg_ref[...])
    h2_ref[...] = h2.astype(BF16)
    if with_router:
        lg_ref[...] = jnp.dot(h2, wr_ref[...], preferred_element_type=F32, precision=lax.Precision.HIGHEST)


def _merge(a, ot, gates, x2, wa, wb, wo, fg, wr, seq):
    n = x2.shape[0]
    tps = seq // TM_MERGE
    with_router = wr is not None
    full = lambda shape: pl.BlockSpec(shape, lambda i: (0,) * len(shape))
    row = lambda w: pl.BlockSpec((TM_MERGE, w), lambda i: (i, 0))
    in_specs = [
        row(GM_WIDTH),
        pl.BlockSpec((None, V_WIDTH, TM_MERGE), lambda i: (i // tps, 0, i % tps)),
        row(2 * D_MODEL), row(D_MODEL),
        full((GM_WIDTH, D_MODEL)), full((V_WIDTH, D_MODEL)), full((D_MODEL, D_MODEL)), full((1, D_MODEL)),
    ]
    out_specs = [row(D_MODEL), row(D_MODEL)]
    out_shape = [jax.ShapeDtypeStruct((n, D_MODEL), F32), jax.ShapeDtypeStruct((n, D_MODEL), BF16)]
    args = [a, ot, gates, x2, wa, wb, wo, fg]
    if with_router:
        in_specs.append(full((D_MODEL, LANES)))
        out_specs.append(row(LANES))
        out_shape.append(jax.ShapeDtypeStruct((n, LANES), F32))
        args.append(wr)
    return pl.pallas_call(
        functools.partial(_merge_kernel, with_router=with_router),
        grid=(n // TM_MERGE,),
        in_specs=in_specs, out_specs=out_specs, out_shape=out_shape,
        compiler_params=_cparams(("parallel",)),
        name="merge_router" if with_router else "merge",
    )(*args)


def _route_kernel(lg_ref, br_ref, cmb_ref):
    lane = lax.broadcasted_iota(jnp.int32, (TM_ROUTE, LANES), 1)
    valid = lane < N_EXPERTS
    logits = jnp.where(valid, lg_ref[...] + br_ref[...], -jnp.inf)
    m1 = jnp.max(logits, axis=-1, keepdims=True)
    i1 = jnp.min(jnp.where(logits == m1, lane, LANES), axis=-1, keepdims=True)
    rest = jnp.where(lane == i1, -jnp.inf, logits)
    m2 = jnp.max(rest, axis=-1, keepdims=True)
    i2 = jnp.min(jnp.where(rest == m2, lane, LANES), axis=-1, keepdims=True)
    e2 = jnp.exp(m2 - m1)
    w1 = 1.0 / (1.0 + e2)
    w2 = e2 / (1.0 + e2)
    cmb_ref[...] = jnp.where(lane == i1, w1, 0.0) + jnp.where(lane == i2, w2, 0.0)


def _route(logits, br):
    n = logits.shape[0]
    return pl.pallas_call(
        _route_kernel,
        grid=(n // TM_ROUTE,),
        in_specs=[pl.BlockSpec((TM_ROUTE, LANES), lambda i: (i, 0)),
                  pl.BlockSpec((1, LANES), lambda i: (0, 0))],
        out_specs=pl.BlockSpec((TM_ROUTE, LANES), lambda i: (i, 0)),
        out_shape=jax.ShapeDtypeStruct((n, LANES), F32),
        compiler_params=_cparams(("parallel",)),
        name="route",
    )(logits, br)


def _ffn_kernel(h_ref, x_ref, wg_ref, wu_ref, wd_ref, *rest, n_inner, with_combine):
    if with_combine:
        cmb_ref, o_ref = rest
    else:
        (o_ref,) = rest
    step = pl.program_id(1)
    for ax in range(2, n_inner + 1):
        step = step * pl.num_programs(ax) + pl.program_id(ax)

    @pl.when(step == 0)
    def _():
        o_ref[...] = x_ref[...]

    h = h_ref[...]
    g = jnp.dot(h, wg_ref[...], preferred_element_type=F32)
    u = jnp.dot(h, wu_ref[...], preferred_element_type=F32)
    act = (g * jax.nn.sigmoid(g) * u).astype(BF16)
    y = jnp.dot(act, wd_ref[...], preferred_element_type=F32)
    if with_combine:
        lane = lax.broadcasted_iota(jnp.int32, (TM_FFN, LANES), 1)
        cw = jnp.sum(jnp.where(lane == pl.program_id(1), cmb_ref[...], 0.0), axis=-1, keepdims=True)
        y = y * cw
    o_ref[...] += y


def _ffn_dense(h2, x2, wgu, wd):
    n = h2.shape[0]
    nf = FF // TF_FFN
    return pl.pallas_call(
        functools.partial(_ffn_kernel, n_inner=1, with_combine=False),
        grid=(n // TM_FFN, nf),
        in_specs=[
            pl.BlockSpec((TM_FFN, D_MODEL), lambda i, j: (i, 0)),
            pl.BlockSpec((TM_FFN, D_MODEL), lambda i, j: (i, 0)),
            pl.BlockSpec((D_MODEL, TF_FFN), lambda i, j: (0, j)),
            pl.BlockSpec((D_MODEL, TF_FFN), lambda i, j: (0, j + nf)),
            pl.BlockSpec((TF_FFN, D_MODEL), lambda i, j: (j, 0)),
        ],
        out_specs=pl.BlockSpec((TM_FFN, D_MODEL), lambda i, j: (i, 0)),
        out_shape=jax.ShapeDtypeStruct((n, D_MODEL), F32),
        compiler_params=_cparams(("parallel", "arbitrary")),
        name="ffn_dense",
    )(h2, x2, wgu, wgu, wd)


def _ffn_moe_dense(h2, x2, wgu, wd, cmb):
    n = h2.shape[0]
    nf = FF // TF_FFN
    return pl.pallas_call(
        functools.partial(_ffn_kernel, n_inner=2, with_combine=True),
        grid=(n // TM_FFN, N_EXPERTS, nf),
        in_specs=[
            pl.BlockSpec((TM_FFN, D_MODEL), lambda i, e, j: (i, 0)),
            pl.BlockSpec((TM_FFN, D_MODEL), lambda i, e, j: (i, 0)),
            pl.BlockSpec((None, D_MODEL, TF_FFN), lambda i, e, j: (e, 0, j)),
            pl.BlockSpec((None, D_MODEL, TF_FFN), lambda i, e, j: (e, 0, j + nf)),
            pl.BlockSpec((None, TF_FFN, D_MODEL), lambda i, e, j: (e, j, 0)),
            pl.BlockSpec((TM_FFN, LANES), lambda i, e, j: (i, 0)),
        ],
        out_specs=pl.BlockSpec((TM_FFN, D_MODEL), lambda i, e, j: (i, 0)),
        out_shape=jax.ShapeDtypeStruct((n, D_MODEL), F32),
        compiler_params=_cparams(("parallel", "arbitrary", "arbitrary")),
        name="ffn_moe_dense",
    )(h2, x2, wgu, wgu, wd, cmb)


def _rope_tables(seq):
    pos = jnp.arange(seq, dtype=F32)
    inv_freq = ROPE_THETA ** (-jnp.arange(0, QK_ROPE, 2, dtype=F32) / QK_ROPE)
    ang = pos[:, None] * inv_freq[None, :]
    cos, sin = jnp.cos(ang), jnp.sin(ang)
    half = QK_ROPE // 2
    ones = jnp.ones((seq, QK_NOPE), F32)
    z = lambda w: jnp.zeros((seq, w), F32)
    tail = HEAD_SLOT - QK_DIM
    c = jnp.concatenate([ones, cos, cos, z(tail)], axis=1)
    sm = jnp.concatenate([z(QK_NOPE), -sin, z(half), z(tail)], axis=1)
    sp = jnp.concatenate([z(QK_NOPE), z(half), sin, z(tail)], axis=1)
    return c, sm, sp


def _head_slots(w, width):
    k = w.shape[0]
    w3 = w.reshape(k, MLA_HEADS, width)
    return jnp.pad(w3, ((0, 0), (0, 0), (0, HEAD_SLOT - width))).reshape(k, QK_WIDTH)


def _pad_lanes(v, width=LANES):
    return jnp.pad(v, (0, width - v.shape[0])).reshape(1, width)


def kernel(x, mix_norm_g, w_in, gm_v_norm_g, gm_v_norm_b, gm_w_spatial, gm_b_spatial, gm_w_proj, mla_q_lat_g, mla_w_uq, mla_kv_lat_g, mla_w_ukv, mla_q_norm_g, mla_k_norm_g, mla_w_proj, w_out, ffn_norm_g, dense_w_gu, dense_w_down, moe_w_router, moe_b_router, moe_w_gu, moe_w_down):
    batch, seq, d = x.shape
    n = batch * seq
    depth = w_in.shape[0]
    rc, rsm, rsp = _rope_tables(seq)
    x2 = x.reshape(n, d)

    o_v = GM_WIDTH
    o_cq = 2 * GM_WIDTH
    o_ckv = o_cq + Q_LORA
    o_kr = o_ckv + KV_LORA
    o_gate = o_kr + QK_ROPE

    for l in range(depth):
        wl = w_in[l]
        w_kr = jnp.pad(wl[:, o_kr:o_gate], ((0, 0), (QK_NOPE, LANES - QK_NOPE - QK_ROPE)))
        w1 = jnp.concatenate([wl[:, :o_kr], w_kr, wl[:, o_gate:]], axis=1).astype(BF16)
        uv, lat, gates = _in_proj(x2, mix_norm_g[l].reshape(1, d), w1)

        wq = _head_slots(mla_w_uq[l], QK_DIM).astype(BF16)
        wkv3 = mla_w_ukv[l].reshape(KV_LORA, MLA_HEADS, QK_NOPE + V_DIM)
        wk = _head_slots(wkv3[:, :, :QK_NOPE].reshape(KV_LORA, MLA_HEADS * QK_NOPE), QK_NOPE).astype(BF16)
        wvt = wkv3[:, :, QK_NOPE:].reshape(KV_LORA, V_WIDTH).T.astype(BF16)
        q, k, vt = _qkv_prep(lat, mla_q_lat_g[l].reshape(1, Q_LORA), wq,
                             mla_kv_lat_g[l].reshape(1, KV_LORA), wk, wvt,
                             _pad_lanes(mla_q_norm_g[l]), _pad_lanes(mla_k_norm_g[l]),
                             rc, rsm, rsp, batch, seq)
        ot = _attention(q.reshape(batch, seq, QK_WIDTH), k.reshape(batch, seq, QK_WIDTH), vt)

        a = _gmlp(uv, gm_v_norm_g[l].reshape(1, GM_WIDTH), gm_v_norm_b[l].reshape(1, GM_WIDTH),
                  gm_w_spatial[l], gm_b_spatial[l].T)

        is_moe = l % 2 == 1
        m = l // 2
        wr = jnp.pad(moe_w_router[m], ((0, 0), (0, LANES - N_EXPERTS))) if is_moe else None
        outs = _merge(a, ot, gates, x2, gm_w_proj[l].astype(BF16), mla_w_proj[l].astype(BF16),
                      w_out[l].astype(BF16), ffn_norm_g[l].reshape(1, d), wr, seq)
        if is_moe:
            xn, h2, logits = outs
            cmb = _route(logits, _pad_lanes(moe_b_router[m]))
            x2 = _ffn_moe_dense(h2, xn, moe_w_gu[m].astype(BF16), moe_w_down[m].astype(BF16), cmb)
        else:
            xn, h2 = outs
            x2 = _ffn_dense(h2, xn, dense_w_gu[m].astype(BF16), dense_w_down[m].astype(BF16))
    return x2.reshape(batch, seq, d)
```

```python
import functools

import jax
import jax.numpy as jnp
import numpy as np
from jax import lax
from jax.experimental import pallas as pl
from jax.experimental.pallas import tpu as pltpu

F32 = jnp.float32
BF16 = jnp.bfloat16

EPS = 1e-6
LANES = 128

D_MODEL = 1024
GM_GROUPS = 8
GM_GROUP_CH = 128
GM_WIDTH = 1024
GM_CHUNK = 128
MLA_HEADS = 16
QK_NOPE = 64
QK_ROPE = 32
QK_DIM = 96
V_DIM = 64
Q_LORA = 512
KV_LORA = 256
ROPE_THETA = 10000.0
HEAD_SLOT = LANES
QK_WIDTH = MLA_HEADS * HEAD_SLOT
V_WIDTH = MLA_HEADS * V_DIM
LAT_WIDTH = Q_LORA + KV_LORA + LANES
N_EXPERTS = 8
FF = 2816

TM_IN = 256
TM_QKV = 256
TM_GMLP = 512
TM_MERGE = 512
TM_FFN = 512
TF_FFN = 1408
TM_ROUTE = 512
TM_SCAT = 256
TM_COMB = 256
ATT_T = 512

VMEM_LIMIT = 56 * 1024 * 1024


def _cparams(sem):
    return pltpu.CompilerParams(dimension_semantics=sem, vmem_limit_bytes=VMEM_LIMIT)


def _rms(xf, g):
    return xf * lax.rsqrt(jnp.mean(xf * xf, axis=-1, keepdims=True) + EPS) * g


def _in_proj_kernel(x_ref, g_ref, w_ref, uv_ref, lat_ref, gate_ref):
    h = _rms(x_ref[...], g_ref[...]).astype(BF16)
    o_lat = 2 * GM_WIDTH
    o_gate = o_lat + LAT_WIDTH
    uv_ref[...] = jnp.dot(h, w_ref[:, :o_lat], preferred_element_type=F32).astype(BF16)
    lat_ref[...] = jnp.dot(h, w_ref[:, o_lat:o_gate], preferred_element_type=F32).astype(BF16)
    gate_ref[...] = jnp.dot(h, w_ref[:, o_gate:], preferred_element_type=F32).astype(BF16)


def _in_proj(x2, g, w1):
    n = x2.shape[0]
    wcols = w1.shape[1]
    return pl.pallas_call(
        _in_proj_kernel,
        grid=(n // TM_IN,),
        in_specs=[
            pl.BlockSpec((TM_IN, D_MODEL), lambda i: (i, 0)),
            pl.BlockSpec((1, D_MODEL), lambda i: (0, 0)),
            pl.BlockSpec((D_MODEL, wcols), lambda i: (0, 0)),
        ],
        out_specs=[
            pl.BlockSpec((TM_IN, 2 * GM_WIDTH), lambda i: (i, 0)),
            pl.BlockSpec((TM_IN, LAT_WIDTH), lambda i: (i, 0)),
            pl.BlockSpec((TM_IN, 2 * D_MODEL), lambda i: (i, 0)),
        ],
        out_shape=[
            jax.ShapeDtypeStruct((n, 2 * GM_WIDTH), BF16),
            jax.ShapeDtypeStruct((n, LAT_WIDTH), BF16),
            jax.ShapeDtypeStruct((n, 2 * D_MODEL), BF16),
        ],
        compiler_params=_cparams(("parallel",)),
        name="in_proj",
    )(x2, g, w1)


def _rope_slot(x, c, sm, sp):
    return x * c + pltpu.roll(x, HEAD_SLOT - QK_ROPE // 2, 1) * sm + pltpu.roll(x, QK_ROPE // 2, 1) * sp


def _qkv_prep_kernel(lat_ref, qg_ref, wq_ref, kvg_ref, wk_ref, wvt_ref, qn_ref, kn_ref,
                     c_ref, sm_ref, sp_ref, q_ref, k_ref, vt_ref):
    lat = lat_ref[...].astype(F32)
    cq = _rms(lat[:, :Q_LORA], qg_ref[...]).astype(BF16)
    ckv = _rms(lat[:, Q_LORA:Q_LORA + KV_LORA], kvg_ref[...]).astype(BF16)
    kr = lat[:, Q_LORA + KV_LORA:]
    c, sm, sp = c_ref[...], sm_ref[...], sp_ref[...]
    qn, kn = qn_ref[...], kn_ref[...]
    scale = 1.0 / float(np.sqrt(QK_DIM))
    q_all = jnp.dot(cq, wq_ref[...], preferred_element_type=F32)
    k_all = jnp.dot(ckv, wk_ref[...], preferred_element_type=F32)
    for h in range(MLA_HEADS):
        sl = slice(h * HEAD_SLOT, (h + 1) * HEAD_SLOT)
        qh = q_all[:, sl]
        qh = qh * lax.rsqrt(jnp.sum(qh * qh, axis=-1, keepdims=True) * (1.0 / QK_DIM) + EPS) * qn
        q_ref[:, sl] = (_rope_slot(qh, c, sm, sp) * scale).astype(BF16)
        kh = k_all[:, sl] + kr
        kh = kh * lax.rsqrt(jnp.sum(kh * kh, axis=-1, keepdims=True) * (1.0 / QK_DIM) + EPS) * kn
        k_ref[:, sl] = _rope_slot(kh, c, sm, sp).astype(BF16)
    vt_ref[...] = lax.dot_general(wvt_ref[...], ckv, (((1,), (1,)), ((), ())),
                                  preferred_element_type=F32).astype(BF16)


def _qkv_prep(lat, qg, wq, kvg, wk, wvt, qn, kn, rc, rsm, rsp, batch, seq):
    n = lat.shape[0]
    tps = seq // TM_QKV
    full = lambda shape: pl.BlockSpec(shape, lambda i: (0,) * len(shape))
    rope = pl.BlockSpec((TM_QKV, HEAD_SLOT), lambda i: (i % tps, 0))
    return pl.pallas_call(
        _qkv_prep_kernel,
        grid=(n // TM_QKV,),
        in_specs=[
            pl.BlockSpec((TM_QKV, LAT_WIDTH), lambda i: (i, 0)),
            full((1, Q_LORA)), full((Q_LORA, QK_WIDTH)),
            full((1, KV_LORA)), full((KV_LORA, QK_WIDTH)), full((V_WIDTH, KV_LORA)),
            full((1, HEAD_SLOT)), full((1, HEAD_SLOT)),
            rope, rope, rope,
        ],
        out_specs=[
            pl.BlockSpec((TM_QKV, QK_WIDTH), lambda i: (i, 0)),
            pl.BlockSpec((TM_QKV, QK_WIDTH), lambda i: (i, 0)),
            pl.BlockSpec((None, V_WIDTH, TM_QKV), lambda i: (i // tps, 0, i % tps)),
        ],
        out_shape=[
            jax.ShapeDtypeStruct((n, QK_WIDTH), BF16),
            jax.ShapeDtypeStruct((n, QK_WIDTH), BF16),
            jax.ShapeDtypeStruct((batch, V_WIDTH, seq), BF16),
        ],
        compiler_params=_cparams(("parallel",)),
        name="qkv_prep",
    )(lat, qg, wq, kvg, wk, wvt, qn, kn, rc, rsm, rsp)


def _attention_kernel(q_ref, k_ref, vt_ref, ot_ref, *, seq):
    nblk = seq // ATT_T
    key_pos = lax.broadcasted_iota(jnp.int32, (ATT_T, ATT_T), 0)
    qry_pos = lax.broadcasted_iota(jnp.int32, (ATT_T, ATT_T), 1)
    causal = key_pos <= qry_pos
    for qi in range(nblk):
        q = q_ref[qi * ATT_T:(qi + 1) * ATT_T, :]
        m = jnp.full((1, ATT_T), -1e30, F32)
        l = jnp.zeros((1, ATT_T), F32)
        acc = jnp.zeros((V_DIM, ATT_T), F32)
        for kj in range(qi + 1):
            k = k_ref[kj * ATT_T:(kj + 1) * ATT_T, :]
            st = lax.dot_general(k, q, (((1,), (1,)), ((), ())), preferred_element_type=F32)
            if kj == qi:
                st = jnp.where(causal, st, -1e30)
            m_new = jnp.maximum(m, jnp.max(st, axis=0, keepdims=True))
            alpha = jnp.exp(m - m_new)
            p = jnp.exp(st - m_new)
            l = alpha * l + jnp.sum(p, axis=0, keepdims=True)
            acc = alpha * acc + jnp.dot(vt_ref[:, kj * ATT_T:(kj + 1) * ATT_T], p.astype(BF16),
                                        preferred_element_type=F32)
            m = m_new
        ot_ref[:, qi * ATT_T:(qi + 1) * ATT_T] = (acc / l).astype(BF16)


def _attention(q3, k3, vt):
    batch, seq, _ = q3.shape
    return pl.pallas_call(
        functools.partial(_attention_kernel, seq=seq),
        grid=(batch, MLA_HEADS),
        in_specs=[
            pl.BlockSpec((None, seq, HEAD_SLOT), lambda b, h: (b, 0, h)),
            pl.BlockSpec((None, seq, HEAD_SLOT), lambda b, h: (b, 0, h)),
            pl.BlockSpec((None, V_DIM, seq), lambda b, h: (b, h, 0)),
        ],
        out_specs=pl.BlockSpec((None, V_DIM, seq), lambda b, h: (b, h, 0)),
        out_shape=jax.ShapeDtypeStruct((batch, V_WIDTH, seq), BF16),
        compiler_params=_cparams(("parallel", "parallel")),
        name="attention",
    )(q3, k3, vt)


def _gmlp_kernel(uv_ref, vg_ref, vb_ref, ws_ref, bst_ref, a_ref):
    v = jax.nn.gelu(uv_ref[:, GM_WIDTH:].astype(F32))
    mu = jnp.mean(v, axis=-1, keepdims=True)
    vc = v - mu
    vn = (vc * lax.rsqrt(jnp.mean(vc * vc, axis=-1, keepdims=True) + EPS) * vg_ref[...] + vb_ref[...]).astype(BF16)
    row = lax.broadcasted_iota(jnp.int32, (GM_CHUNK, GM_CHUNK), 0)
    col = lax.broadcasted_iota(jnp.int32, (GM_CHUNK, GM_CHUNK), 1)
    tril = col <= row
    bst = bst_ref[...]
    for g in range(GM_GROUPS):
        ws = jnp.where(tril, ws_ref[g], 0.0).astype(BF16)
        bias = bst[:, g:g + 1]
        cs = slice(g * GM_GROUP_CH, (g + 1) * GM_GROUP_CH)
        for c in range(TM_GMLP // GM_CHUNK):
            rs = slice(c * GM_CHUNK, (c + 1) * GM_CHUNK)
            mixed = jnp.dot(ws, vn[rs, cs], preferred_element_type=F32) + bias
            u = jax.nn.gelu(uv_ref[rs, cs].astype(F32))
            a_ref[rs, cs] = (u * mixed).astype(BF16)


def _gmlp(uv, vg, vb, ws, bst):
    n = uv.shape[0]
    full = lambda shape: pl.BlockSpec(shape, lambda i: (0,) * len(shape))
    return pl.pallas_call(
        _gmlp_kernel,
        grid=(n // TM_GMLP,),
        in_specs=[
            pl.BlockSpec((TM_GMLP, 2 * GM_WIDTH), lambda i: (i, 0)),
            full((1, GM_WIDTH)), full((1, GM_WIDTH)),
            full((GM_GROUPS, GM_CHUNK, GM_CHUNK)), full((GM_CHUNK, GM_GROUPS)),
        ],
        out_specs=pl.BlockSpec((TM_GMLP, GM_WIDTH), lambda i: (i, 0)),
        out_shape=jax.ShapeDtypeStruct((n, GM_WIDTH), BF16),
        compiler_params=_cparams(("parallel",)),
        name="gmlp",
    )(uv, vg, vb, ws, bst)


def _split_bf16(v):
    hi = v.astype(BF16)
    return hi, (v - hi.astype(F32)).astype(BF16)


def _merge_kernel(a_ref, ot_ref, gate_ref, x_ref, wa_ref, wb_ref, wo_ref, fg_ref, *rest, with_router):
    if with_router:
        wr_ref, xn_ref, lg_ref = rest
    else:
        xn_ref, h2_ref = rest
    ya = jnp.dot(a_ref[...], wa_ref[...], preferred_element_type=F32)
    yb = lax.dot_general(ot_ref[...], wb_ref[...], (((0,), (0,)), ((), ())), preferred_element_type=F32)
    ga = jax.nn.sigmoid(gate_ref[:, :D_MODEL].astype(F32))
    gb = jax.nn.sigmoid(gate_ref[:, D_MODEL:].astype(F32))
    merged = (ga * ya + gb * yb).astype(BF16)
    xn = x_ref[...] + jnp.dot(merged, wo_ref[...], preferred_element_type=F32)
    xn_ref[...] = xn
    h2 = _rms(xn, fg_ref[...])
    if with_router:
        h_hi, h_lo = _split_bf16(h2)
        w_hi, w_lo = _split_bf16(wr_ref[...])
        lg_ref[...] = (jnp.dot(h_hi, w_hi, preferred_element_type=F32)
                       + (jnp.dot(h_lo, w_hi, preferred_element_type=F32)
                          + jnp.dot(h_hi, w_lo, preferred_element_type=F32)))
    else:
        h2_ref[...] = h2.astype(BF16)


def _merge(a, ot, gates, x2, wa, wb, wo, fg, wr, seq):
    n = x2.shape[0]
    tps = seq // TM_MERGE
    with_router = wr is not None
    full = lambda shape: pl.BlockSpec(shape, lambda i: (0,) * len(shape))
    row = lambda w: pl.BlockSpec((TM_MERGE, w), lambda i: (i, 0))
    in_specs = [
        row(GM_WIDTH),
        pl.BlockSpec((None, V_WIDTH, TM_MERGE), lambda i: (i // tps, 0, i % tps)),
        row(2 * D_MODEL), row(D_MODEL),
        full((GM_WIDTH, D_MODEL)), full((V_WIDTH, D_MODEL)), full((D_MODEL, D_MODEL)), full((1, D_MODEL)),
    ]
    args = [a, ot, gates, x2, wa, wb, wo, fg]
    if with_router:
        in_specs.append(full((D_MODEL, LANES)))
        out_specs = [row(D_MODEL), row(LANES)]
        out_shape = [jax.ShapeDtypeStruct((n, D_MODEL), F32), jax.ShapeDtypeStruct((n, LANES), F32)]
        args.append(wr)
    else:
        out_specs = [row(D_MODEL), row(D_MODEL)]
        out_shape = [jax.ShapeDtypeStruct((n, D_MODEL), F32), jax.ShapeDtypeStruct((n, D_MODEL), BF16)]
    return pl.pallas_call(
        functools.partial(_merge_kernel, with_router=with_router),
        grid=(n // TM_MERGE,),
        in_specs=in_specs, out_specs=out_specs, out_shape=out_shape,
        compiler_params=_cparams(("parallel",)),
        name="merge_router" if with_router else "merge",
    )(*args)


def _round_up_tile(c):
    return (c + (TM_FFN - 1)) & (-TM_FFN)


def _route_kernel(lg_ref, br_ref, imeta_ref, wcol_ref, cnt_ref, run_ref):
    @pl.when(pl.program_id(0) == 0)
    def _():
        run_ref[...] = jnp.zeros_like(run_ref)

    lt = (lg_ref[...] + br_ref[...]).T[:N_EXPERTS, :]
    sub = lax.broadcasted_iota(jnp.int32, lt.shape, 0)
    m1 = jnp.max(lt, axis=0, keepdims=True)
    i1 = jnp.min(jnp.where(lt == m1, sub, N_EXPERTS), axis=0, keepdims=True)
    rest = jnp.where(sub == i1, -jnp.inf, lt)
    m2 = jnp.max(rest, axis=0, keepdims=True)
    i2 = jnp.min(jnp.where(rest == m2, sub, N_EXPERTS), axis=0, keepdims=True)
    e2 = jnp.exp(m2 - m1)
    w1 = 1.0 / (1.0 + e2)
    w2 = e2 / (1.0 + e2)

    onehot = jnp.where((sub == i1) | (sub == i2), 1.0, 0.0)
    src = lax.broadcasted_iota(jnp.int32, (TM_ROUTE, TM_ROUTE), 0)
    dst = lax.broadcasted_iota(jnp.int32, (TM_ROUTE, TM_ROUTE), 1)
    earlier = jnp.where(src < dst, 1.0, 0.0).astype(BF16)
    seen = jnp.dot(onehot.astype(BF16), earlier, preferred_element_type=F32) + run_ref[:, :1]
    r1 = jnp.sum(jnp.where(sub == i1, seen, 0.0), axis=0, keepdims=True).astype(jnp.int32)
    r2 = jnp.sum(jnp.where(sub == i2, seen, 0.0), axis=0, keepdims=True).astype(jnp.int32)
    imeta_ref[...] = jnp.where(sub == 0, i1, jnp.where(sub == 1, i2, jnp.where(sub == 2, r1, jnp.where(sub == 3, r2, 0))))

    total = run_ref[...] + jnp.sum(onehot, axis=1, keepdims=True)
    run_ref[...] = total
    cnt_ref[...] = total.astype(jnp.int32)

    sub_w = lax.broadcasted_iota(jnp.int32, (LANES, TM_ROUTE), 0)
    wcol_ref[...] = jnp.where(sub_w == 0, w1, jnp.where(sub_w == 1, w2, 0.0)).T


def _route(logits, br):
    n = logits.shape[0]
    return pl.pallas_call(
        _route_kernel,
        grid=(n // TM_ROUTE,),
        in_specs=[pl.BlockSpec((TM_ROUTE, LANES), lambda i: (i, 0)),
                  pl.BlockSpec((1, LANES), lambda i: (0, 0))],
        out_specs=[pl.BlockSpec((N_EXPERTS, TM_ROUTE), lambda i: (0, i)),
                   pl.BlockSpec((TM_ROUTE, LANES), lambda i: (i, 0)),
                   pl.BlockSpec((N_EXPERTS, LANES), lambda i: (0, 0))],
        out_shape=[jax.ShapeDtypeStruct((N_EXPERTS, n), jnp.int32),
                   jax.ShapeDtypeStruct((n, LANES), F32),
                   jax.ShapeDtypeStruct((N_EXPERTS, LANES), jnp.int32)],
        scratch_shapes=[pltpu.VMEM((N_EXPERTS, LANES), F32)],
        compiler_params=_cparams(("arbitrary",)),
        name="route",
    )(logits, br)


def _positions_kernel(cnt_ref, imeta_ref, pos_ref):
    im = imeta_ref[...]
    i1, i2, r1, r2 = im[0:1], im[1:2], im[2:3], im[3:4]
    start = jnp.int32(0)
    p1 = jnp.zeros_like(i1)
    p2 = jnp.zeros_like(i2)
    for e in range(N_EXPERTS):
        p1 = jnp.where(i1 == e, start, p1)
        p2 = jnp.where(i2 == e, start, p2)
        start = start + _round_up_tile(cnt_ref[e])
    pos_ref[...] = jnp.concatenate([p1 + r1, p2 + r2], axis=0)


def _positions(cnt, imeta):
    n = imeta.shape[1]
    return pl.pallas_call(
        _positions_kernel,
        grid_spec=pltpu.PrefetchScalarGridSpec(
            num_scalar_prefetch=1,
            grid=(n // TM_ROUTE,),
            in_specs=[pl.BlockSpec((N_EXPERTS, TM_ROUTE), lambda i, c: (0, i))],
            out_specs=pl.BlockSpec((2, TM_ROUTE), lambda i, c: (0, i)),
        ),
        out_shape=jax.ShapeDtypeStruct((2, n), jnp.int32),
        compiler_params=_cparams(("parallel",)),
        name="positions",
    )(cnt, imeta)


def _scatter_kernel(pos_ref, xn_ref, fg_ref, xs_in_ref, xs_ref, hbuf, sem, *, n_tokens):
    del xs_in_ref
    base = pl.program_id(0) * TM_SCAT
    hbuf[...] = _rms(xn_ref[...], fg_ref[...])

    def issue(r, carry):
        for k in range(2):
            dst = pos_ref[k * n_tokens + base + r]
            pltpu.make_async_copy(hbuf.at[pl.ds(r, 1)], xs_ref.at[pl.ds(dst, 1)], sem).start()
        return carry

    lax.fori_loop(0, TM_SCAT, issue, 0, unroll=8)
    for k in range(2):
        pltpu.make_async_copy(hbuf, xs_ref.at[pl.ds(0, TM_SCAT)], sem).wait()


def _scatter(pos, xn, fg, xs_zero):
    n = xn.shape[0]
    rows = xs_zero.shape[0]
    return pl.pallas_call(
        functools.partial(_scatter_kernel, n_tokens=n),
        grid_spec=pltpu.PrefetchScalarGridSpec(
            num_scalar_prefetch=1,
            grid=(n // TM_SCAT,),
            in_specs=[pl.BlockSpec((TM_SCAT, D_MODEL), lambda i, p: (i, 0)),
                      pl.BlockSpec((1, D_MODEL), lambda i, p: (0, 0)),
                      pl.BlockSpec(memory_space=pl.ANY)],
            out_specs=pl.BlockSpec(memory_space=pl.ANY),
            scratch_shapes=[pltpu.VMEM((TM_SCAT, D_MODEL), F32), pltpu.SemaphoreType.DMA(())],
        ),
        out_shape=jax.ShapeDtypeStruct((rows, D_MODEL), F32),
        input_output_aliases={3: 0},
        compiler_params=_cparams(("arbitrary",)),
        name="moe_scatter",
    )(pos, xn, fg, xs_zero)


def _combine_kernel(pos_ref, xn_ref, wcol_ref, ys_ref, o_ref, buf, sem, *, n_tokens):
    base = pl.program_id(0) * TM_COMB

    def issue(r, carry):
        for k in range(2):
            src = pos_ref[k * n_tokens + base + r]
            pltpu.make_async_copy(ys_ref.at[pl.ds(src, 1)], buf.at[k, pl.ds(r, 1)], sem).start()
        return carry

    lax.fori_loop(0, TM_COMB, issue, 0, unroll=8)
    for k in range(2):
        pltpu.make_async_copy(ys_ref.at[pl.ds(0, TM_COMB)], buf.at[k], sem).wait()
    w = wcol_ref[...]
    o_ref[...] = xn_ref[...] + (w[:, 0:1] * buf[0] + w[:, 1:2] * buf[1])


def _combine(pos, xn, wcol, ys):
    n = xn.shape[0]
    return pl.pallas_call(
        functools.partial(_combine_kernel, n_tokens=n),
        grid_spec=pltpu.PrefetchScalarGridSpec(
            num_scalar_prefetch=1,
            grid=(n // TM_COMB,),
            in_specs=[pl.BlockSpec((TM_COMB, D_MODEL), lambda i, p: (i, 0)),
                      pl.BlockSpec((TM_COMB, LANES), lambda i, p: (i, 0)),
                      pl.BlockSpec(memory_space=pl.ANY)],
            out_specs=pl.BlockSpec((TM_COMB, D_MODEL), lambda i, p: (i, 0)),
            scratch_shapes=[pltpu.VMEM((2, TM_COMB, D_MODEL), F32), pltpu.SemaphoreType.DMA(())],
        ),
        out_shape=jax.ShapeDtypeStruct((n, D_MODEL), F32),
        compiler_params=_cparams(("arbitrary",)),
        name="moe_combine",
    )(pos, xn, wcol, ys)


def _swiglu_partial(h, wg_ref, wu_ref, wd_ref):
    g = jnp.dot(h, wg_ref[...], preferred_element_type=F32)
    u = jnp.dot(h, wu_ref[...], preferred_element_type=F32)
    act = (g * jax.nn.sigmoid(g) * u).astype(BF16)
    return jnp.dot(act, wd_ref[...], preferred_element_type=F32)


def _ffn_dense_kernel(h_ref, x_ref, wg_ref, wu_ref, wd_ref, o_ref):
    @pl.when(pl.program_id(1) == 0)
    def _():
        o_ref[...] = x_ref[...]

    o_ref[...] += _swiglu_partial(h_ref[...], wg_ref, wu_ref, wd_ref)


def _ffn_dense(h2, x2, wgu, wd):
    n = h2.shape[0]
    nf = FF // TF_FFN
    return pl.pallas_call(
        _ffn_dense_kernel,
        grid=(n // TM_FFN, nf),
        in_specs=[
            pl.BlockSpec((TM_FFN, D_MODEL), lambda i, j: (i, 0)),
            pl.BlockSpec((TM_FFN, D_MODEL), lambda i, j: (i, 0)),
            pl.BlockSpec((D_MODEL, TF_FFN), lambda i, j: (0, j)),
            pl.BlockSpec((D_MODEL, TF_FFN), lambda i, j: (0, j + nf)),
            pl.BlockSpec((TF_FFN, D_MODEL), lambda i, j: (j, 0)),
        ],
        out_specs=pl.BlockSpec((TM_FFN, D_MODEL), lambda i, j: (i, 0)),
        out_shape=jax.ShapeDtypeStruct((n, D_MODEL), F32),
        compiler_params=_cparams(("parallel", "arbitrary")),
        name="ffn_dense",
    )(h2, x2, wgu, wgu, wd)


def _tile_plan(i, cnt_ref):
    end = jnp.int32(0)
    expert = jnp.int32(0)
    for e in range(N_EXPERTS):
        end = end + _round_up_tile(cnt_ref[e])
        expert = expert + (end <= i * TM_FFN).astype(jnp.int32)
    return jnp.minimum(expert, N_EXPERTS - 1), end // TM_FFN


def _ffn_grouped_kernel(cnt_ref, xs_ref, wg_ref, wu_ref, wd_ref, o_ref):
    i, j = pl.program_id(0), pl.program_id(1)
    _, used = _tile_plan(i, cnt_ref)

    @pl.when(i < used)
    def _():
        y = _swiglu_partial(xs_ref[...].astype(BF16), wg_ref, wu_ref, wd_ref)

        @pl.when(j == 0)
        def _():
            o_ref[...] = y

        @pl.when(j > 0)
        def _():
            o_ref[...] += y

    @pl.when((i >= used) & (j == 0))
    def _():
        o_ref[...] = jnp.zeros_like(o_ref)


def _ffn_grouped(cnt, xs, wgu, wd):
    rows = xs.shape[0]
    nf = FF // TF_FFN

    def tile(i, c):
        return jnp.minimum(i, _tile_plan(i, c)[1] - 1)

    def half(i, j, c):
        return jnp.where(i < _tile_plan(i, c)[1], j, nf - 1)

    return pl.pallas_call(
        _ffn_grouped_kernel,
        grid_spec=pltpu.PrefetchScalarGridSpec(
            num_scalar_prefetch=1,
            grid=(rows // TM_FFN, nf),
            in_specs=[
                pl.BlockSpec((TM_FFN, D_MODEL), lambda i, j, c: (tile(i, c), 0)),
                pl.BlockSpec((None, D_MODEL, TF_FFN), lambda i, j, c: (_tile_plan(i, c)[0], 0, half(i, j, c))),
                pl.BlockSpec((None, D_MODEL, TF_FFN), lambda i, j, c: (_tile_plan(i, c)[0], 0, half(i, j, c) + nf)),
                pl.BlockSpec((None, TF_FFN, D_MODEL), lambda i, j, c: (_tile_plan(i, c)[0], half(i, j, c), 0)),
            ],
            out_specs=pl.BlockSpec((TM_FFN, D_MODEL), lambda i, j, c: (i, 0)),
        ),
        out_shape=jax.ShapeDtypeStruct((rows, D_MODEL), F32),
        compiler_params=_cparams(("arbitrary", "arbitrary")),
        name="ffn_grouped",
    )(cnt, xs, wgu, wgu, wd)


def _moe(xn, logits, br, fg, wgu, wd):
    n = xn.shape[0]
    imeta, wcol, cnt_lanes = _route(logits, br)
    cnt = cnt_lanes[:, 0]
    pos = _positions(cnt, imeta).reshape(2 * n)
    xs = _scatter(pos, xn, fg, jnp.zeros((2 * n + N_EXPERTS * TM_FFN, D_MODEL), F32))
    ys = _ffn_grouped(cnt, xs, wgu, wd)
    return _combine(pos, xn, wcol, ys)


def _rope_tables(seq):
    pos = jnp.arange(seq, dtype=F32)
    inv_freq = ROPE_THETA ** (-jnp.arange(0, QK_ROPE, 2, dtype=F32) / QK_ROPE)
    ang = pos[:, None] * inv_freq[None, :]
    cos, sin = jnp.cos(ang), jnp.sin(ang)
    half = QK_ROPE // 2
    ones = jnp.ones((seq, QK_NOPE), F32)
    z = lambda w: jnp.zeros((seq, w), F32)
    tail = HEAD_SLOT - QK_DIM
    c = jnp.concatenate([ones, cos, cos, z(tail)], axis=1)
    sm = jnp.concatenate([z(QK_NOPE), -sin, z(half), z(tail)], axis=1)
    sp = jnp.concatenate([z(QK_NOPE), z(half), sin, z(tail)], axis=1)
    return c, sm, sp


def _head_slots(w, width):
    k = w.shape[0]
    w3 = w.reshape(k, MLA_HEADS, width)
    return jnp.pad(w3, ((0, 0), (0, 0), (0, HEAD_SLOT - width))).reshape(k, QK_WIDTH)


def _pad_lanes(v, width=LANES):
    return jnp.pad(v, (0, width - v.shape[0])).reshape(1, width)


def kernel(x, mix_norm_g, w_in, gm_v_norm_g, gm_v_norm_b, gm_w_spatial, gm_b_spatial, gm_w_proj, mla_q_lat_g, mla_w_uq, mla_kv_lat_g, mla_w_ukv, mla_q_norm_g, mla_k_norm_g, mla_w_proj, w_out, ffn_norm_g, dense_w_gu, dense_w_down, moe_w_router, moe_b_router, moe_w_gu, moe_w_down):
    batch, seq, d = x.shape
    n = batch * seq
    depth = w_in.shape[0]
    rc, rsm, rsp = _rope_tables(seq)
    x2 = x.reshape(n, d)

    o_v = GM_WIDTH
    o_cq = 2 * GM_WIDTH
    o_ckv = o_cq + Q_LORA
    o_kr = o_ckv + KV_LORA
    o_gate = o_kr + QK_ROPE

    for l in range(depth):
        wl = w_in[l]
        w_kr = jnp.pad(wl[:, o_kr:o_gate], ((0, 0), (QK_NOPE, LANES - QK_NOPE - QK_ROPE)))
        w1 = jnp.concatenate([wl[:, :o_kr], w_kr, wl[:, o_gate:]], axis=1).astype(BF16)
        uv, lat, gates = _in_proj(x2, mix_norm_g[l].reshape(1, d), w1)

        wq = _head_slots(mla_w_uq[l], QK_DIM).astype(BF16)
        wkv3 = mla_w_ukv[l].reshape(KV_LORA, MLA_HEADS, QK_NOPE + V_DIM)
        wk = _head_slots(wkv3[:, :, :QK_NOPE].reshape(KV_LORA, MLA_HEADS * QK_NOPE), QK_NOPE).astype(BF16)
        wvt = wkv3[:, :, QK_NOPE:].reshape(KV_LORA, V_WIDTH).T.astype(BF16)
        q, k, vt = _qkv_prep(lat, mla_q_lat_g[l].reshape(1, Q_LORA), wq,
                             mla_kv_lat_g[l].reshape(1, KV_LORA), wk, wvt,
                             _pad_lanes(mla_q_norm_g[l]), _pad_lanes(mla_k_norm_g[l]),
                             rc, rsm, rsp, batch, seq)
        ot = _attention(q.reshape(batch, seq, QK_WIDTH), k.reshape(batch, seq, QK_WIDTH), vt)

        a = _gmlp(uv, gm_v_norm_g[l].reshape(1, GM_WIDTH), gm_v_norm_b[l].reshape(1, GM_WIDTH),
                  gm_w_spatial[l], gm_b_spatial[l].T)

        is_moe = l % 2 == 1
        m = l // 2
        wr = jnp.pad(moe_w_router[m], ((0, 0), (0, LANES - N_EXPERTS))) if is_moe else None
        outs = _merge(a, ot, gates, x2, gm_w_proj[l].astype(BF16), mla_w_proj[l].astype(BF16),
                      w_out[l].astype(BF16), ffn_norm_g[l].reshape(1, d), wr, seq)
        if is_moe:
            xn, logits = outs
            x2 = _moe(xn, logits, _pad_lanes(moe_b_router[m]), ffn_norm_g[l].reshape(1, d),
                      moe_w_gu[m].astype(BF16), moe_w_down[m].astype(BF16))
        else:
            xn, h2 = outs
            x2 = _ffn_dense(h2, xn, dense_w_gu[m].astype(BF16), dense_w_down[m].astype(BF16))
    return x2.reshape(batch, seq, d)
```

```python
import functools

import jax
import jax.numpy as jnp
import numpy as np
from jax import lax
from jax.experimental import pallas as pl
from jax.experimental.pallas import tpu as pltpu

F32 = jnp.float32
BF16 = jnp.bfloat16

EPS = 1e-6
LANES = 128

D_MODEL = 1024
GM_GROUPS = 8
GM_GROUP_CH = 128
GM_WIDTH = 1024
GM_CHUNK = 128
MLA_HEADS = 16
QK_NOPE = 64
QK_ROPE = 32
QK_DIM = 96
V_DIM = 64
Q_LORA = 512
KV_LORA = 256
ROPE_THETA = 10000.0
HEAD_SLOT = LANES
QK_WIDTH = MLA_HEADS * HEAD_SLOT
V_WIDTH = MLA_HEADS * V_DIM
LAT_WIDTH = Q_LORA + KV_LORA + 2 * LANES
N_EXPERTS = 8
FF = 2816

TM_IN = 256
TM_QKV = 512
TM_GMLP = 512
TM_MERGE = 512
TM_FFN = 512
TF_FFN = 1408
TM_ROUTE = 512
TM_SCAT = 256
TM_COMB = 256
ATT_T = TM_QKV
ATT_G = 2
ATT_SUM_ROWS = 16

VMEM_LIMIT = 56 * 1024 * 1024


def _cparams(sem):
    return pltpu.CompilerParams(dimension_semantics=sem, vmem_limit_bytes=VMEM_LIMIT)


def _rms(xf, g):
    return xf * lax.rsqrt(jnp.mean(xf * xf, axis=-1, keepdims=True) + EPS) * g


def _in_proj_kernel(x_ref, g_ref, w_ref, uv_ref, lat_ref, gate_ref):
    h = _rms(x_ref[...], g_ref[...]).astype(BF16)
    o_lat = 2 * GM_WIDTH
    o_gate = o_lat + LAT_WIDTH
    uv_ref[...] = jnp.dot(h, w_ref[:, :o_lat], preferred_element_type=F32).astype(BF16)
    lat_ref[...] = jnp.dot(h, w_ref[:, o_lat:o_gate], preferred_element_type=F32).astype(BF16)
    gate_ref[...] = jnp.dot(h, w_ref[:, o_gate:], preferred_element_type=F32).astype(BF16)


def _in_proj(x2, g, w1):
    n = x2.shape[0]
    wcols = w1.shape[1]
    return pl.pallas_call(
        _in_proj_kernel,
        grid=(n // TM_IN,),
        in_specs=[
            pl.BlockSpec((TM_IN, D_MODEL), lambda i: (i, 0)),
            pl.BlockSpec((1, D_MODEL), lambda i: (0, 0)),
            pl.BlockSpec((D_MODEL, wcols), lambda i: (0, 0)),
        ],
        out_specs=[
            pl.BlockSpec((TM_IN, 2 * GM_WIDTH), lambda i: (i, 0)),
            pl.BlockSpec((TM_IN, LAT_WIDTH), lambda i: (i, 0)),
            pl.BlockSpec((TM_IN, 2 * D_MODEL), lambda i: (i, 0)),
        ],
        out_shape=[
            jax.ShapeDtypeStruct((n, 2 * GM_WIDTH), BF16),
            jax.ShapeDtypeStruct((n, LAT_WIDTH), BF16),
            jax.ShapeDtypeStruct((n, 2 * D_MODEL), BF16),
        ],
        compiler_params=_cparams(("parallel",)),
        name="in_proj",
    )(x2, g, w1)


def _qkv_prep_kernel(lat_ref, qg_ref, wq_ref, wqs_ref, kvg_ref, wk_ref, wvt_ref,
                     aq_ref, bq_ref, ak_ref, bk_ref, gkn_ref, q_ref, k_ref, vt_ref):
    lat = lat_ref[...].astype(F32)
    cq = _rms(lat[:, :Q_LORA], qg_ref[...]).astype(BF16)
    ckv = _rms(lat[:, Q_LORA:Q_LORA + KV_LORA], kvg_ref[...]).astype(BF16)
    kr = lat[:, Q_LORA + KV_LORA:Q_LORA + KV_LORA + LANES]
    kr_sw = lat[:, Q_LORA + KV_LORA + LANES:]
    kr_ssq = jnp.sum(kr * kr, axis=-1, keepdims=True)
    kr_rot = kr * ak_ref[...] + kr_sw * bk_ref[...]
    aq, bq, gkn = aq_ref[...], bq_ref[...], gkn_ref[...]
    q_all = jnp.dot(cq, wq_ref[...], preferred_element_type=F32)
    q_sw = jnp.dot(cq, wqs_ref[...], preferred_element_type=F32)
    k_all = jnp.dot(ckv, wk_ref[...], preferred_element_type=F32)
    for h in range(MLA_HEADS):
        sl = slice(h * HEAD_SLOT, (h + 1) * HEAD_SLOT)
        qh = q_all[:, sl]
        rq = lax.rsqrt(jnp.sum(qh * qh, axis=-1, keepdims=True) * (1.0 / QK_DIM) + EPS)
        q_ref[:, sl] = ((qh * aq + q_sw[:, sl] * bq) * rq).astype(BF16)
        kh = k_all[:, sl]
        rk = lax.rsqrt((jnp.sum(kh * kh, axis=-1, keepdims=True) + kr_ssq) * (1.0 / QK_DIM) + EPS)
        k_ref[:, sl] = ((kh * gkn + kr_rot) * rk).astype(BF16)
    vt_ref[...] = lax.dot_general(wvt_ref[...], ckv, (((1,), (1,)), ((), ())),
                                  preferred_element_type=F32).astype(BF16)


def _qkv_prep(lat, qg, wq, wqs, kvg, wk, wvt, aq, bq, ak, bk, gkn, batch, seq):
    n = lat.shape[0]
    tps = seq // TM_QKV
    full = lambda shape: pl.BlockSpec(shape, lambda i: (0,) * len(shape))
    rope = pl.BlockSpec((TM_QKV, HEAD_SLOT), lambda i: (i % tps, 0))
    return pl.pallas_call(
        _qkv_prep_kernel,
        grid=(n // TM_QKV,),
        in_specs=[
            pl.BlockSpec((TM_QKV, LAT_WIDTH), lambda i: (i, 0)),
            full((1, Q_LORA)), full((Q_LORA, QK_WIDTH)), full((Q_LORA, QK_WIDTH)),
            full((1, KV_LORA)), full((KV_LORA, QK_WIDTH)), full((V_WIDTH, KV_LORA)),
            rope, rope, rope, rope, full((1, HEAD_SLOT)),
        ],
        out_specs=[
            pl.BlockSpec((TM_QKV, QK_WIDTH), lambda i: (i, 0)),
            pl.BlockSpec((TM_QKV, QK_WIDTH), lambda i: (i, 0)),
            pl.BlockSpec((None, None, V_WIDTH, TM_QKV), lambda i: (i // tps, i % tps, 0, 0)),
        ],
        out_shape=[
            jax.ShapeDtypeStruct((n, QK_WIDTH), BF16),
            jax.ShapeDtypeStruct((n, QK_WIDTH), BF16),
            jax.ShapeDtypeStruct((batch, tps, V_WIDTH, TM_QKV), BF16),
        ],
        compiler_params=_cparams(("parallel",)),
        name="qkv_prep",
    )(lat, qg, wq, wqs, kvg, wk, wvt, aq, bq, ak, bk, gkn)


def _attention_kernel(q_ref, k_ref, vt_ref, ot_ref, *, seq):
    nblk = seq // ATT_T
    key_pos = lax.broadcasted_iota(jnp.int32, (ATT_T, ATT_T), 0)
    qry_pos = lax.broadcasted_iota(jnp.int32, (ATT_T, ATT_T), 1)
    causal = key_pos <= qry_pos
    nt = (((1,), (1,)), ((), ()))

    ones_rows = jnp.ones((ATT_SUM_ROWS, ATT_T), BF16)

    def update(state, g, q, kj, diagonal):
        m, acc = state
        k = k_ref[kj * ATT_T:(kj + 1) * ATT_T, g * HEAD_SLOT:(g + 1) * HEAD_SLOT]
        st = lax.dot_general(k, q, nt, preferred_element_type=F32)
        if diagonal:
            st = jnp.where(causal, st, -1e30)
        m_new = jnp.maximum(m, jnp.max(st, axis=0, keepdims=True))
        alpha = jnp.exp2(m - m_new)
        p = jnp.exp2((st - m_new).astype(BF16))
        v_aug = jnp.concatenate([vt_ref[kj, g * V_DIM:(g + 1) * V_DIM, :], ones_rows], axis=0)
        return m_new, alpha * acc + jnp.dot(v_aug, p, preferred_element_type=F32)

    for qi in range(nblk):
        qs = [q_ref[qi * ATT_T:(qi + 1) * ATT_T, g * HEAD_SLOT:(g + 1) * HEAD_SLOT] for g in range(ATT_G)]
        init = (jnp.full((1, ATT_T), -1e30, F32), jnp.zeros((V_DIM + ATT_SUM_ROWS, ATT_T), F32))
        states = [init for _ in range(ATT_G)]
        for kj in range(qi + 1):
            states = [update(states[g], g, qs[g], kj, kj == qi) for g in range(ATT_G)]
        for g in range(ATT_G):
            acc = states[g][1]
            o = acc[:V_DIM] / acc[V_DIM:V_DIM + 1]
            ot_ref[g * V_DIM:(g + 1) * V_DIM, qi * ATT_T:(qi + 1) * ATT_T] = o.astype(BF16)


def _attention(q3, k3, vt4):
    batch, seq, _ = q3.shape
    nblk = seq // ATT_T
    return pl.pallas_call(
        functools.partial(_attention_kernel, seq=seq),
        grid=(batch, MLA_HEADS // ATT_G),
        in_specs=[
            pl.BlockSpec((None, seq, ATT_G * HEAD_SLOT), lambda b, h: (b, 0, h)),
            pl.BlockSpec((None, seq, ATT_G * HEAD_SLOT), lambda b, h: (b, 0, h)),
            pl.BlockSpec((None, nblk, ATT_G * V_DIM, ATT_T), lambda b, h: (b, 0, h, 0)),
        ],
        out_specs=pl.BlockSpec((None, ATT_G * V_DIM, seq), lambda b, h: (b, h, 0)),
        out_shape=jax.ShapeDtypeStruct((batch, V_WIDTH, seq), BF16),
        compiler_params=_cparams(("parallel", "parallel")),
        name="attention",
    )(q3, k3, vt4)


def _gmlp_kernel(uv_ref, vg_ref, vb_ref, ws_ref, bst_ref, a_ref):
    v = jax.nn.gelu(uv_ref[:, GM_WIDTH:].astype(F32))
    mu = jnp.mean(v, axis=-1, keepdims=True)
    vc = v - mu
    vn = (vc * lax.rsqrt(jnp.mean(vc * vc, axis=-1, keepdims=True) + EPS) * vg_ref[...] + vb_ref[...]).astype(BF16)
    row = lax.broadcasted_iota(jnp.int32, (GM_CHUNK, GM_CHUNK), 0)
    col = lax.broadcasted_iota(jnp.int32, (GM_CHUNK, GM_CHUNK), 1)
    tril = col <= row
    bst = bst_ref[...]
    for g in range(GM_GROUPS):
        ws = jnp.where(tril, ws_ref[g], 0.0).astype(BF16)
        bias = bst[:, g:g + 1]
        cs = slice(g * GM_GROUP_CH, (g + 1) * GM_GROUP_CH)
        for c in range(TM_GMLP // GM_CHUNK):
            rs = slice(c * GM_CHUNK, (c + 1) * GM_CHUNK)
            mixed = jnp.dot(ws, vn[rs, cs], preferred_element_type=F32) + bias
            u = jax.nn.gelu(uv_ref[rs, cs].astype(F32))
            a_ref[rs, cs] = (u * mixed).astype(BF16)


def _gmlp(uv, vg, vb, ws, bst):
    n = uv.shape[0]
    full = lambda shape: pl.BlockSpec(shape, lambda i: (0,) * len(shape))
    return pl.pallas_call(
        _gmlp_kernel,
        grid=(n // TM_GMLP,),
        in_specs=[
            pl.BlockSpec((TM_GMLP, 2 * GM_WIDTH), lambda i: (i, 0)),
            full((1, GM_WIDTH)), full((1, GM_WIDTH)),
            full((GM_GROUPS, GM_CHUNK, GM_CHUNK)), full((GM_CHUNK, GM_GROUPS)),
        ],
        out_specs=pl.BlockSpec((TM_GMLP, GM_WIDTH), lambda i: (i, 0)),
        out_shape=jax.ShapeDtypeStruct((n, GM_WIDTH), BF16),
        compiler_params=_cparams(("parallel",)),
        name="gmlp",
    )(uv, vg, vb, ws, bst)


def _split_bf16(v):
    hi = v.astype(BF16)
    return hi, (v - hi.astype(F32)).astype(BF16)


def _merge_kernel(a_ref, ot_ref, gate_ref, x_ref, wa_ref, wb_ref, wo_ref, fg_ref, *rest, with_router):
    if with_router:
        wr_ref, xn_ref, lg_ref = rest
    else:
        xn_ref, h2_ref = rest
    ya = jnp.dot(a_ref[...], wa_ref[...], preferred_element_type=F32)
    yb = lax.dot_general(ot_ref[...], wb_ref[...], (((0,), (0,)), ((), ())), preferred_element_type=F32)
    ga = jax.nn.sigmoid(gate_ref[:, :D_MODEL].astype(F32))
    gb = jax.nn.sigmoid(gate_ref[:, D_MODEL:].astype(F32))
    merged = (ga * ya + gb * yb).astype(BF16)
    xn = x_ref[...] + jnp.dot(merged, wo_ref[...], preferred_element_type=F32)
    xn_ref[...] = xn
    h2 = _rms(xn, fg_ref[...])
    if with_router:
        h_hi, h_lo = _split_bf16(h2)
        w_hi, w_lo = _split_bf16(wr_ref[...])
        lg_ref[...] = (jnp.dot(h_hi, w_hi, preferred_element_type=F32)
                       + (jnp.dot(h_lo, w_hi, preferred_element_type=F32)
                          + jnp.dot(h_hi, w_lo, preferred_element_type=F32)))
    else:
        h2_ref[...] = h2.astype(BF16)


def _merge(a, ot, gates, x2, wa, wb, wo, fg, wr, seq):
    n = x2.shape[0]
    tps = seq // TM_MERGE
    with_router = wr is not None
    full = lambda shape: pl.BlockSpec(shape, lambda i: (0,) * len(shape))
    row = lambda w: pl.BlockSpec((TM_MERGE, w), lambda i: (i, 0))
    in_specs = [
        row(GM_WIDTH),
        pl.BlockSpec((None, V_WIDTH, TM_MERGE), lambda i: (i // tps, 0, i % tps)),
        row(2 * D_MODEL), row(D_MODEL),
        full((GM_WIDTH, D_MODEL)), full((V_WIDTH, D_MODEL)), full((D_MODEL, D_MODEL)), full((1, D_MODEL)),
    ]
    args = [a, ot, gates, x2, wa, wb, wo, fg]
    if with_router:
        in_specs.append(full((D_MODEL, LANES)))
        out_specs = [row(D_MODEL), row(LANES)]
        out_shape = [jax.ShapeDtypeStruct((n, D_MODEL), F32), jax.ShapeDtypeStruct((n, LANES), F32)]
        args.append(wr)
    else:
        out_specs = [row(D_MODEL), row(D_MODEL)]
        out_shape = [jax.ShapeDtypeStruct((n, D_MODEL), F32), jax.ShapeDtypeStruct((n, D_MODEL), BF16)]
    return pl.pallas_call(
        functools.partial(_merge_kernel, with_router=with_router),
        grid=(n // TM_MERGE,),
        in_specs=in_specs, out_specs=out_specs, out_shape=out_shape,
        compiler_params=_cparams(("parallel",)),
        name="merge_router" if with_router else "merge",
    )(*args)


def _round_up_tile(c):
    return (c + (TM_FFN - 1)) & (-TM_FFN)


def _route_kernel(lg_ref, br_ref, imeta_ref, wcol_ref, cnt_ref, run_ref):
    @pl.when(pl.program_id(0) == 0)
    def _():
        run_ref[...] = jnp.zeros_like(run_ref)

    lt = (lg_ref[...] + br_ref[...]).T[:N_EXPERTS, :]
    sub = lax.broadcasted_iota(jnp.int32, lt.shape, 0)
    m1 = jnp.max(lt, axis=0, keepdims=True)
    i1 = jnp.min(jnp.where(lt == m1, sub, N_EXPERTS), axis=0, keepdims=True)
    rest = jnp.where(sub == i1, -jnp.inf, lt)
    m2 = jnp.max(rest, axis=0, keepdims=True)
    i2 = jnp.min(jnp.where(rest == m2, sub, N_EXPERTS), axis=0, keepdims=True)
    e2 = jnp.exp(m2 - m1)
    w1 = 1.0 / (1.0 + e2)
    w2 = e2 / (1.0 + e2)

    onehot = jnp.where((sub == i1) | (sub == i2), 1.0, 0.0)
    src = lax.broadcasted_iota(jnp.int32, (TM_ROUTE, TM_ROUTE), 0)
    dst = lax.broadcasted_iota(jnp.int32, (TM_ROUTE, TM_ROUTE), 1)
    earlier = jnp.where(src < dst, 1.0, 0.0).astype(BF16)
    seen = jnp.dot(onehot.astype(BF16), earlier, preferred_element_type=F32) + run_ref[:, :1]
    r1 = jnp.sum(jnp.where(sub == i1, seen, 0.0), axis=0, keepdims=True).astype(jnp.int32)
    r2 = jnp.sum(jnp.where(sub == i2, seen, 0.0), axis=0, keepdims=True).astype(jnp.int32)
    imeta_ref[...] = jnp.where(sub == 0, i1, jnp.where(sub == 1, i2, jnp.where(sub == 2, r1, jnp.where(sub == 3, r2, 0))))

    total = run_ref[...] + jnp.sum(onehot, axis=1, keepdims=True)
    run_ref[...] = total
    cnt_ref[...] = total.astype(jnp.int32)

    sub_w = lax.broadcasted_iota(jnp.int32, (LANES, TM_ROUTE), 0)
    wcol_ref[...] = jnp.where(sub_w == 0, w1, jnp.where(sub_w == 1, w2, 0.0)).T


def _route(logits, br):
    n = logits.shape[0]
    return pl.pallas_call(
        _route_kernel,
        grid=(n // TM_ROUTE,),
        in_specs=[pl.BlockSpec((TM_ROUTE, LANES), lambda i: (i, 0)),
                  pl.BlockSpec((1, LANES), lambda i: (0, 0))],
        out_specs=[pl.BlockSpec((N_EXPERTS, TM_ROUTE), lambda i: (0, i)),
                   pl.BlockSpec((TM_ROUTE, LANES), lambda i: (i, 0)),
                   pl.BlockSpec((N_EXPERTS, LANES), lambda i: (0, 0))],
        out_shape=[jax.ShapeDtypeStruct((N_EXPERTS, n), jnp.int32),
                   jax.ShapeDtypeStruct((n, LANES), F32),
                   jax.ShapeDtypeStruct((N_EXPERTS, LANES), jnp.int32)],
        scratch_shapes=[pltpu.VMEM((N_EXPERTS, LANES), F32)],
        compiler_params=_cparams(("arbitrary",)),
        name="route",
    )(logits, br)


def _positions_kernel(cnt_ref, imeta_ref, pos_ref):
    im = imeta_ref[...]
    i1, i2, r1, r2 = im[0:1], im[1:2], im[2:3], im[3:4]
    start = jnp.int32(0)
    p1 = jnp.zeros_like(i1)
    p2 = jnp.zeros_like(i2)
    for e in range(N_EXPERTS):
        p1 = jnp.where(i1 == e, start, p1)
        p2 = jnp.where(i2 == e, start, p2)
        start = start + _round_up_tile(cnt_ref[e])
    pos_ref[...] = jnp.concatenate([p1 + r1, p2 + r2], axis=0)


def _positions(cnt, imeta):
    n = imeta.shape[1]
    return pl.pallas_call(
        _positions_kernel,
        grid_spec=pltpu.PrefetchScalarGridSpec(
            num_scalar_prefetch=1,
            grid=(n // TM_ROUTE,),
            in_specs=[pl.BlockSpec((N_EXPERTS, TM_ROUTE), lambda i, c: (0, i))],
            out_specs=pl.BlockSpec((2, TM_ROUTE), lambda i, c: (0, i)),
        ),
        out_shape=jax.ShapeDtypeStruct((2, n), jnp.int32),
        compiler_params=_cparams(("parallel",)),
        name="positions",
    )(cnt, imeta)


def _scatter_kernel(pos_ref, cnt_ref, xn_ref, fg_ref, xs_ref, hbuf, zbuf, sem, zsem, *, n_tokens):
    n_tiles = xs_ref.shape[0] // TM_FFN

    @pl.when(pl.program_id(0) == 0)
    def _():
        zbuf[...] = jnp.zeros_like(zbuf)

        def zero_tile(t):
            rows = pl.ds(pl.multiple_of(t * TM_FFN, TM_FFN), TM_FFN)
            return pltpu.make_async_copy(zbuf, xs_ref.at[rows], zsem)

        def each_zero_tile(action):
            end = jnp.int32(0)
            for e in range(N_EXPERTS):
                tiles = _round_up_tile(cnt_ref[e]) // TM_FFN
                end = end + tiles

                @pl.when(tiles > 0)
                def _(end=end):
                    action(zero_tile(end - 1))

            def tail(t, carry):
                action(zero_tile(t))
                return carry

            lax.fori_loop(end, n_tiles, tail, 0)

        each_zero_tile(lambda copy: copy.start())
        each_zero_tile(lambda copy: copy.wait())

    base = pl.program_id(0) * TM_SCAT
    hbuf[...] = _rms(xn_ref[...], fg_ref[...])

    def issue(r, carry):
        for k in range(2):
            dst = pos_ref[k * n_tokens + base + r]
            pltpu.make_async_copy(hbuf.at[pl.ds(r, 1)], xs_ref.at[pl.ds(dst, 1)], sem).start()
        return carry

    lax.fori_loop(0, TM_SCAT, issue, 0, unroll=8)
    for k in range(2):
        pltpu.make_async_copy(hbuf, xs_ref.at[pl.ds(0, TM_SCAT)], sem).wait()


def _scatter(pos, cnt, xn, fg):
    n = xn.shape[0]
    rows = 2 * n + N_EXPERTS * TM_FFN
    return pl.pallas_call(
        functools.partial(_scatter_kernel, n_tokens=n),
        grid_spec=pltpu.PrefetchScalarGridSpec(
            num_scalar_prefetch=2,
            grid=(n // TM_SCAT,),
            in_specs=[pl.BlockSpec((TM_SCAT, D_MODEL), lambda i, p, c: (i, 0)),
                      pl.BlockSpec((1, D_MODEL), lambda i, p, c: (0, 0))],
            out_specs=pl.BlockSpec(memory_space=pl.ANY),
            scratch_shapes=[pltpu.VMEM((TM_SCAT, D_MODEL), F32), pltpu.VMEM((TM_FFN, D_MODEL), F32),
                            pltpu.SemaphoreType.DMA(()), pltpu.SemaphoreType.DMA(())],
        ),
        out_shape=jax.ShapeDtypeStruct((rows, D_MODEL), F32),
        compiler_params=_cparams(("arbitrary",)),
        name="moe_scatter",
    )(pos, cnt, xn, fg)


def _combine_kernel(pos_ref, xn_ref, wcol_ref, ys_ref, o_ref, buf, sem, *, n_tokens):
    base = pl.program_id(0) * TM_COMB

    def issue(r, carry):
        for k in range(2):
            src = pos_ref[k * n_tokens + base + r]
            pltpu.make_async_copy(ys_ref.at[pl.ds(src, 1)], buf.at[k, pl.ds(r, 1)], sem).start()
        return carry

    lax.fori_loop(0, TM_COMB, issue, 0, unroll=8)
    for k in range(2):
        pltpu.make_async_copy(ys_ref.at[pl.ds(0, TM_COMB)], buf.at[k], sem).wait()
    w = wcol_ref[...]
    o_ref[...] = xn_ref[...] + (w[:, 0:1] * buf[0] + w[:, 1:2] * buf[1])


def _combine(pos, xn, wcol, ys):
    n = xn.shape[0]
    return pl.pallas_call(
        functools.partial(_combine_kernel, n_tokens=n),
        grid_spec=pltpu.PrefetchScalarGridSpec(
            num_scalar_prefetch=1,
            grid=(n // TM_COMB,),
            in_specs=[pl.BlockSpec((TM_COMB, D_MODEL), lambda i, p: (i, 0)),
                      pl.BlockSpec((TM_COMB, LANES), lambda i, p: (i, 0)),
                      pl.BlockSpec(memory_space=pl.ANY)],
            out_specs=pl.BlockSpec((TM_COMB, D_MODEL), lambda i, p: (i, 0)),
            scratch_shapes=[pltpu.VMEM((2, TM_COMB, D_MODEL), F32), pltpu.SemaphoreType.DMA(())],
        ),
        out_shape=jax.ShapeDtypeStruct((n, D_MODEL), F32),
        compiler_params=_cparams(("arbitrary",)),
        name="moe_combine",
    )(pos, xn, wcol, ys)


def _swiglu_partial(h, wg_ref, wu_ref, wd_ref):
    g = jnp.dot(h, wg_ref[...], preferred_element_type=F32)
    u = jnp.dot(h, wu_ref[...], preferred_element_type=F32)
    act = (g * jax.nn.sigmoid(g) * u).astype(BF16)
    return jnp.dot(act, wd_ref[...], preferred_element_type=F32)


def _ffn_dense_kernel(h_ref, x_ref, wg_ref, wu_ref, wd_ref, o_ref):
    @pl.when(pl.program_id(1) == 0)
    def _():
        o_ref[...] = x_ref[...]

    o_ref[...] += _swiglu_partial(h_ref[...], wg_ref, wu_ref, wd_ref)


def _ffn_dense(h2, x2, wgu, wd):
    n = h2.shape[0]
    nf = FF // TF_FFN
    return pl.pallas_call(
        _ffn_dense_kernel,
        grid=(n // TM_FFN, nf),
        in_specs=[
            pl.BlockSpec((TM_FFN, D_MODEL), lambda i, j: (i, 0)),
            pl.BlockSpec((TM_FFN, D_MODEL), lambda i, j: (i, 0)),
            pl.BlockSpec((D_MODEL, TF_FFN), lambda i, j: (0, j)),
            pl.BlockSpec((D_MODEL, TF_FFN), lambda i, j: (0, j + nf)),
            pl.BlockSpec((TF_FFN, D_MODEL), lambda i, j: (j, 0)),
        ],
        out_specs=pl.BlockSpec((TM_FFN, D_MODEL), lambda i, j: (i, 0)),
        out_shape=jax.ShapeDtypeStruct((n, D_MODEL), F32),
        compiler_params=_cparams(("parallel", "arbitrary")),
        name="ffn_dense",
    )(h2, x2, wgu, wgu, wd)


def _tile_plan(i, cnt_ref):
    end = jnp.int32(0)
    expert = jnp.int32(0)
    for e in range(N_EXPERTS):
        end = end + _round_up_tile(cnt_ref[e])
        expert = expert + (end <= i * TM_FFN).astype(jnp.int32)
    return jnp.minimum(expert, N_EXPERTS - 1), end // TM_FFN


def _ffn_grouped_kernel(cnt_ref, xs_ref, wg_ref, wu_ref, wd_ref, o_ref):
    i, j = pl.program_id(0), pl.program_id(1)
    _, used = _tile_plan(i, cnt_ref)

    @pl.when(i < used)
    def _():
        y = _swiglu_partial(xs_ref[...].astype(BF16), wg_ref, wu_ref, wd_ref)

        @pl.when(j == 0)
        def _():
            o_ref[...] = y

        @pl.when(j > 0)
        def _():
            o_ref[...] += y

    @pl.when((i >= used) & (j == 0))
    def _():
        o_ref[...] = jnp.zeros_like(o_ref)


def _ffn_grouped(cnt, xs, wgu, wd):
    rows = xs.shape[0]
    nf = FF // TF_FFN

    def tile(i, c):
        return jnp.minimum(i, _tile_plan(i, c)[1] - 1)

    def half(i, j, c):
        return jnp.where(i < _tile_plan(i, c)[1], j, nf - 1)

    return pl.pallas_call(
        _ffn_grouped_kernel,
        grid_spec=pltpu.PrefetchScalarGridSpec(
            num_scalar_prefetch=1,
            grid=(rows // TM_FFN, nf),
            in_specs=[
                pl.BlockSpec((TM_FFN, D_MODEL), lambda i, j, c: (tile(i, c), 0)),
                pl.BlockSpec((None, D_MODEL, TF_FFN), lambda i, j, c: (_tile_plan(i, c)[0], 0, half(i, j, c))),
                pl.BlockSpec((None, D_MODEL, TF_FFN), lambda i, j, c: (_tile_plan(i, c)[0], 0, half(i, j, c) + nf)),
                pl.BlockSpec((None, TF_FFN, D_MODEL), lambda i, j, c: (_tile_plan(i, c)[0], half(i, j, c), 0)),
            ],
            out_specs=pl.BlockSpec((TM_FFN, D_MODEL), lambda i, j, c: (i, 0)),
        ),
        out_shape=jax.ShapeDtypeStruct((rows, D_MODEL), F32),
        compiler_params=_cparams(("arbitrary", "arbitrary")),
        name="ffn_grouped",
    )(cnt, xs, wgu, wgu, wd)


def _moe(xn, logits, br, fg, wgu, wd):
    n = xn.shape[0]
    imeta, wcol, cnt_lanes = _route(logits, br)
    cnt = cnt_lanes[:, 0]
    pos = _positions(cnt, imeta).reshape(2 * n)
    xs = _scatter(pos, cnt, xn, fg)
    ys = _ffn_grouped(cnt, xs, wgu, wd)
    return _combine(pos, xn, wcol, ys)


def _rope_tables(seq):
    pos = jnp.arange(seq, dtype=F32)
    inv_freq = ROPE_THETA ** (-jnp.arange(0, QK_ROPE, 2, dtype=F32) / QK_ROPE)
    ang = pos[:, None] * inv_freq[None, :]
    return jnp.cos(ang), jnp.sin(ang)


def _rotary_slot_tables(cos, sin, gain, nope_gain, scale):
    seq = cos.shape[0]
    half = QK_ROPE // 2
    g1, g2 = gain[QK_NOPE:QK_NOPE + half], gain[QK_NOPE + half:QK_DIM]
    z = lambda w: jnp.zeros((seq, w), F32)
    tail = HEAD_SLOT - QK_DIM
    a = jnp.concatenate([jnp.broadcast_to(nope_gain, (seq, QK_NOPE)), g1 * cos, g2 * cos, z(tail)], axis=1)
    b = jnp.concatenate([z(QK_NOPE), -g2 * sin, g1 * sin, z(tail)], axis=1)
    return a * scale, b * scale


def _swap_rotary_halves(w):
    half = QK_ROPE // 2
    sw = jnp.concatenate([w[..., half:], w[..., :half]], axis=-1)
    pad = [(0, 0)] * (w.ndim - 1) + [(QK_NOPE, HEAD_SLOT - QK_DIM)]
    return jnp.pad(sw, pad)


def _head_slots(w, width):
    k = w.shape[0]
    w3 = w.reshape(k, MLA_HEADS, width)
    return jnp.pad(w3, ((0, 0), (0, 0), (0, HEAD_SLOT - width))).reshape(k, QK_WIDTH)


def _pad_lanes(v, width=LANES):
    return jnp.pad(v, (0, width - v.shape[0])).reshape(1, width)


def kernel(x, mix_norm_g, w_in, gm_v_norm_g, gm_v_norm_b, gm_w_spatial, gm_b_spatial, gm_w_proj, mla_q_lat_g, mla_w_uq, mla_kv_lat_g, mla_w_ukv, mla_q_norm_g, mla_k_norm_g, mla_w_proj, w_out, ffn_norm_g, dense_w_gu, dense_w_down, moe_w_router, moe_b_router, moe_w_gu, moe_w_down):
    batch, seq, d = x.shape
    n = batch * seq
    depth = w_in.shape[0]
    cos, sin = _rope_tables(seq)
    x2 = x.reshape(n, d)
    q_scale = float(np.log2(np.e) / np.sqrt(QK_DIM))

    o_v = GM_WIDTH
    o_cq = 2 * GM_WIDTH
    o_ckv = o_cq + Q_LORA
    o_kr = o_ckv + KV_LORA
    o_gate = o_kr + QK_ROPE

    for l in range(depth):
        wl = w_in[l]
        w_kr = jnp.pad(wl[:, o_kr:o_gate], ((0, 0), (QK_NOPE, LANES - QK_NOPE - QK_ROPE)))
        w1 = jnp.concatenate([wl[:, :o_kr], w_kr, _swap_rotary_halves(wl[:, o_kr:o_gate]), wl[:, o_gate:]],
                             axis=1).astype(BF16)
        uv, lat, gates = _in_proj(x2, mix_norm_g[l].reshape(1, d), w1)

        wq = _head_slots(mla_w_uq[l], QK_DIM).astype(BF16)
        wq3 = mla_w_uq[l].reshape(Q_LORA, MLA_HEADS, QK_DIM)
        wqs = _swap_rotary_halves(wq3[:, :, QK_NOPE:]).reshape(Q_LORA, QK_WIDTH).astype(BF16)
        wkv3 = mla_w_ukv[l].reshape(KV_LORA, MLA_HEADS, QK_NOPE + V_DIM)
        wk = _head_slots(wkv3[:, :, :QK_NOPE].reshape(KV_LORA, MLA_HEADS * QK_NOPE), QK_NOPE).astype(BF16)
        wvt = wkv3[:, :, QK_NOPE:].reshape(KV_LORA, V_WIDTH).T.astype(BF16)
        gq, gk = mla_q_norm_g[l], mla_k_norm_g[l]
        aq, bq = _rotary_slot_tables(cos, sin, gq, gq[:QK_NOPE], q_scale)
        ak, bk = _rotary_slot_tables(cos, sin, gk, jnp.zeros((QK_NOPE,), F32), 1.0)
        q, k, vt = _qkv_prep(lat, mla_q_lat_g[l].reshape(1, Q_LORA), wq, wqs,
                             mla_kv_lat_g[l].reshape(1, KV_LORA), wk, wvt,
                             aq, bq, ak, bk, _pad_lanes(gk[:QK_NOPE]), batch, seq)
        ot = _attention(q.reshape(batch, seq, QK_WIDTH), k.reshape(batch, seq, QK_WIDTH), vt)

        a = _gmlp(uv, gm_v_norm_g[l].reshape(1, GM_WIDTH), gm_v_norm_b[l].reshape(1, GM_WIDTH),
                  gm_w_spatial[l], gm_b_spatial[l].T)

        is_moe = l % 2 == 1
        m = l // 2
        wr = jnp.pad(moe_w_router[m], ((0, 0), (0, LANES - N_EXPERTS))) if is_moe else None
        outs = _merge(a, ot, gates, x2, gm_w_proj[l].astype(BF16), mla_w_proj[l].astype(BF16),
                      w_out[l].astype(BF16), ffn_norm_g[l].reshape(1, d), wr, seq)
        if is_moe:
            xn, logits = outs
            x2 = _moe(xn, logits, _pad_lanes(moe_b_router[m]), ffn_norm_g[l].reshape(1, d),
                      moe_w_gu[m].astype(BF16), moe_w_down[m].astype(BF16))
        else:
            xn, h2 = outs
            x2 = _ffn_dense(h2, xn, dense_w_gu[m].astype(BF16), dense_w_down[m].astype(BF16))
    return x2.reshape(batch, seq, d)
```

```python
import functools

import jax
import jax.numpy as jnp
import numpy as np
from jax import lax
from jax.experimental import pallas as pl
from jax.experimental.pallas import tpu as pltpu

F32 = jnp.float32
BF16 = jnp.bfloat16

EPS = 1e-6
LANES = 128

D_MODEL = 1024
GM_GROUPS = 8
GM_GROUP_CH = 128
GM_WIDTH = 1024
GM_CHUNK = 128
MLA_HEADS = 16
QK_NOPE = 64
QK_ROPE = 32
QK_DIM = 96
V_DIM = 64
Q_LORA = 512
KV_LORA = 256
ROPE_THETA = 10000.0
HEAD_SLOT = LANES
QK_WIDTH = MLA_HEADS * HEAD_SLOT
V_WIDTH = MLA_HEADS * V_DIM
LAT_WIDTH = Q_LORA + KV_LORA + 2 * LANES
N_EXPERTS = 8
FF = 2816

TM_IN = 256
TM_QKV = 512
TM_GMLP = 512
TM_MERGE = 512
TM_FFN = 512
TF_FFN = 1408
TM_ROUTE = 512
TM_SCAT = 256
TM_COMB = 256
ATT_T = 512
ATT_G = 2

VMEM_LIMIT = 56 * 1024 * 1024


def _cparams(sem):
    return pltpu.CompilerParams(dimension_semantics=sem, vmem_limit_bytes=VMEM_LIMIT)


def _rms(xf, g):
    return xf * lax.rsqrt(jnp.mean(xf * xf, axis=-1, keepdims=True) + EPS) * g


def _in_proj_kernel(x_ref, g_ref, w_ref, uv_ref, lat_ref, gate_ref):
    h = _rms(x_ref[...], g_ref[...]).astype(BF16)
    o_lat = 2 * GM_WIDTH
    o_gate = o_lat + LAT_WIDTH
    uv_ref[...] = jnp.dot(h, w_ref[:, :o_lat], preferred_element_type=F32).astype(BF16)
    lat_ref[...] = jnp.dot(h, w_ref[:, o_lat:o_gate], preferred_element_type=F32).astype(BF16)
    gate_ref[...] = jnp.dot(h, w_ref[:, o_gate:], preferred_element_type=F32).astype(BF16)


def _in_proj(x2, g, w1):
    n = x2.shape[0]
    wcols = w1.shape[1]
    return pl.pallas_call(
        _in_proj_kernel,
        grid=(n // TM_IN,),
        in_specs=[
            pl.BlockSpec((TM_IN, D_MODEL), lambda i: (i, 0)),
            pl.BlockSpec((1, D_MODEL), lambda i: (0, 0)),
            pl.BlockSpec((D_MODEL, wcols), lambda i: (0, 0)),
        ],
        out_specs=[
            pl.BlockSpec((TM_IN, 2 * GM_WIDTH), lambda i: (i, 0)),
            pl.BlockSpec((TM_IN, LAT_WIDTH), lambda i: (i, 0)),
            pl.BlockSpec((TM_IN, 2 * D_MODEL), lambda i: (i, 0)),
        ],
        out_shape=[
            jax.ShapeDtypeStruct((n, 2 * GM_WIDTH), BF16),
            jax.ShapeDtypeStruct((n, LAT_WIDTH), BF16),
            jax.ShapeDtypeStruct((n, 2 * D_MODEL), BF16),
        ],
        compiler_params=_cparams(("parallel",)),
        name="in_proj",
    )(x2, g, w1)


def _qkv_prep_kernel(lat_ref, qg_ref, wq_ref, wqs_ref, kvg_ref, wk_ref, wv_ref, vone_ref,
                     aq_ref, bq_ref, ak_ref, bk_ref, gkn_ref, q_ref, k_ref, v_ref):
    lat = lat_ref[...].astype(F32)
    cq = _rms(lat[:, :Q_LORA], qg_ref[...]).astype(BF16)
    ckv = _rms(lat[:, Q_LORA:Q_LORA + KV_LORA], kvg_ref[...]).astype(BF16)
    kr = lat[:, Q_LORA + KV_LORA:Q_LORA + KV_LORA + LANES]
    kr_sw = lat[:, Q_LORA + KV_LORA + LANES:]
    kr_ssq = jnp.sum(kr * kr, axis=-1, keepdims=True)
    kr_rot = kr * ak_ref[...] + kr_sw * bk_ref[...]
    aq, bq, gkn = aq_ref[...], bq_ref[...], gkn_ref[...]
    q_all = jnp.dot(cq, wq_ref[...], preferred_element_type=F32)
    q_sw = jnp.dot(cq, wqs_ref[...], preferred_element_type=F32)
    k_all = jnp.dot(ckv, wk_ref[...], preferred_element_type=F32)
    for h in range(MLA_HEADS):
        sl = slice(h * HEAD_SLOT, (h + 1) * HEAD_SLOT)
        qh = q_all[:, sl]
        rq = lax.rsqrt(jnp.sum(qh * qh, axis=-1, keepdims=True) * (1.0 / QK_DIM) + EPS)
        q_ref[:, sl] = ((qh * aq + q_sw[:, sl] * bq) * rq).astype(BF16)
        kh = k_all[:, sl]
        rk = lax.rsqrt((jnp.sum(kh * kh, axis=-1, keepdims=True) + kr_ssq) * (1.0 / QK_DIM) + EPS)
        k_ref[:, sl] = ((kh * gkn + kr_rot) * rk).astype(BF16)
    v_ref[...] = (jnp.dot(ckv, wv_ref[...], preferred_element_type=F32) + vone_ref[...]).astype(BF16)


def _qkv_prep(lat, qg, wq, wqs, kvg, wk, wv, vone, aq, bq, ak, bk, gkn, seq):
    n = lat.shape[0]
    tps = seq // TM_QKV
    full = lambda shape: pl.BlockSpec(shape, lambda i: (0,) * len(shape))
    rope = pl.BlockSpec((TM_QKV, HEAD_SLOT), lambda i: (i % tps, 0))
    return pl.pallas_call(
        _qkv_prep_kernel,
        grid=(n // TM_QKV,),
        in_specs=[
            pl.BlockSpec((TM_QKV, LAT_WIDTH), lambda i: (i, 0)),
            full((1, Q_LORA)), full((Q_LORA, QK_WIDTH)), full((Q_LORA, QK_WIDTH)),
            full((1, KV_LORA)), full((KV_LORA, QK_WIDTH)), full((KV_LORA, QK_WIDTH)), full((1, QK_WIDTH)),
            rope, rope, rope, rope, full((1, HEAD_SLOT)),
        ],
        out_specs=[pl.BlockSpec((TM_QKV, QK_WIDTH), lambda i: (i, 0))] * 3,
        out_shape=[jax.ShapeDtypeStruct((n, QK_WIDTH), BF16)] * 3,
        compiler_params=_cparams(("parallel",)),
        name="qkv_prep",
    )(lat, qg, wq, wqs, kvg, wk, wv, vone, aq, bq, ak, bk, gkn)


def _attention_kernel(q_ref, k_ref, v_ref, o_ref, *, seq):
    nblk = seq // ATT_T
    key_pos = lax.broadcasted_iota(jnp.int32, (ATT_T, ATT_T), 0)
    qry_pos = lax.broadcasted_iota(jnp.int32, (ATT_T, ATT_T), 1)
    causal = key_pos <= qry_pos
    nt = (((1,), (1,)), ((), ()))
    tn = (((0,), (0,)), ((), ()))

    def update(state, g, st, kj, diagonal):
        m, acc = state
        if diagonal:
            st = jnp.where(causal, st, -1e30)
        m_new = jnp.maximum(m, jnp.max(st, axis=0, keepdims=True))
        alpha = jnp.exp2(m - m_new)
        p = jnp.exp2((st - m_new).astype(BF16))
        v_blk = v_ref[kj * ATT_T:(kj + 1) * ATT_T, g * HEAD_SLOT:(g + 1) * HEAD_SLOT]
        return m_new, alpha * acc + lax.dot_general(v_blk, p, tn, preferred_element_type=F32)

    for qi in range(nblk):
        q0, q1 = qi * ATT_T, (qi + 1) * ATT_T
        strips = [lax.dot_general(k_ref[0:q1, g * HEAD_SLOT:(g + 1) * HEAD_SLOT],
                                  q_ref[q0:q1, g * HEAD_SLOT:(g + 1) * HEAD_SLOT], nt,
                                  preferred_element_type=F32) for g in range(ATT_G)]
        init = (jnp.full((1, ATT_T), -1e30, F32), jnp.zeros((HEAD_SLOT, ATT_T), F32))
        states = [init for _ in range(ATT_G)]
        for kj in range(qi + 1):
            states = [update(states[g], g, strips[g][kj * ATT_T:(kj + 1) * ATT_T], kj, kj == qi)
                      for g in range(ATT_G)]
        for g in range(ATT_G):
            acc = states[g][1].T
            o_ref[q0:q1, g * V_DIM:(g + 1) * V_DIM] = (acc[:, :V_DIM] / acc[:, V_DIM:V_DIM + 1]).astype(BF16)


def _attention(q3, k3, v3):
    batch, seq, _ = q3.shape
    slot = pl.BlockSpec((None, seq, ATT_G * HEAD_SLOT), lambda b, h: (b, 0, h))
    return pl.pallas_call(
        functools.partial(_attention_kernel, seq=seq),
        grid=(batch, MLA_HEADS // ATT_G),
        in_specs=[slot, slot, slot],
        out_specs=pl.BlockSpec((None, seq, ATT_G * V_DIM), lambda b, h: (b, 0, h)),
        out_shape=jax.ShapeDtypeStruct((batch, seq, V_WIDTH), BF16),
        compiler_params=_cparams(("parallel", "parallel")),
        name="attention",
    )(q3, k3, v3)


def _gmlp_kernel(uv_ref, vg_ref, vb_ref, ws_ref, bst_ref, a_ref):
    v = jax.nn.gelu(uv_ref[:, GM_WIDTH:].astype(F32))
    mu = jnp.mean(v, axis=-1, keepdims=True)
    vc = v - mu
    vn = (vc * lax.rsqrt(jnp.mean(vc * vc, axis=-1, keepdims=True) + EPS) * vg_ref[...] + vb_ref[...]).astype(BF16)
    row = lax.broadcasted_iota(jnp.int32, (GM_CHUNK, GM_CHUNK), 0)
    col = lax.broadcasted_iota(jnp.int32, (GM_CHUNK, GM_CHUNK), 1)
    tril = col <= row
    bst = bst_ref[...]
    for g in range(GM_GROUPS):
        ws = jnp.where(tril, ws_ref[g], 0.0).astype(BF16)
        bias = bst[:, g:g + 1]
        cs = slice(g * GM_GROUP_CH, (g + 1) * GM_GROUP_CH)
        for c in range(TM_GMLP // GM_CHUNK):
            rs = slice(c * GM_CHUNK, (c + 1) * GM_CHUNK)
            mixed = jnp.dot(ws, vn[rs, cs], preferred_element_type=F32) + bias
            u = jax.nn.gelu(uv_ref[rs, cs].astype(F32))
            a_ref[rs, cs] = (u * mixed).astype(BF16)


def _gmlp(uv, vg, vb, ws, bst):
    n = uv.shape[0]
    full = lambda shape: pl.BlockSpec(shape, lambda i: (0,) * len(shape))
    return pl.pallas_call(
        _gmlp_kernel,
        grid=(n // TM_GMLP,),
        in_specs=[
            pl.BlockSpec((TM_GMLP, 2 * GM_WIDTH), lambda i: (i, 0)),
            full((1, GM_WIDTH)), full((1, GM_WIDTH)),
            full((GM_GROUPS, GM_CHUNK, GM_CHUNK)), full((GM_CHUNK, GM_GROUPS)),
        ],
        out_specs=pl.BlockSpec((TM_GMLP, GM_WIDTH), lambda i: (i, 0)),
        out_shape=jax.ShapeDtypeStruct((n, GM_WIDTH), BF16),
        compiler_params=_cparams(("parallel",)),
        name="gmlp",
    )(uv, vg, vb, ws, bst)


def _split_bf16(v):
    hi = v.astype(BF16)
    return hi, (v - hi.astype(F32)).astype(BF16)


def _merge_kernel(a_ref, ot_ref, gate_ref, x_ref, wa_ref, wb_ref, wo_ref, fg_ref, *rest, with_router):
    if with_router:
        wr_ref, xn_ref, lg_ref = rest
    else:
        xn_ref, h2_ref = rest
    ya = jnp.dot(a_ref[...], wa_ref[...], preferred_element_type=F32)
    yb = jnp.dot(ot_ref[...], wb_ref[...], preferred_element_type=F32)
    ga = jax.nn.sigmoid(gate_ref[:, :D_MODEL].astype(F32))
    gb = jax.nn.sigmoid(gate_ref[:, D_MODEL:].astype(F32))
    merged = (ga * ya + gb * yb).astype(BF16)
    xn = x_ref[...] + jnp.dot(merged, wo_ref[...], preferred_element_type=F32)
    xn_ref[...] = xn
    h2 = _rms(xn, fg_ref[...])
    if with_router:
        h_hi, h_lo = _split_bf16(h2)
        w_hi, w_lo = _split_bf16(wr_ref[...])
        hh_hl = jnp.dot(h_hi, jnp.concatenate([w_hi, w_lo], axis=1), preferred_element_type=F32)
        lg_ref[...] = hh_hl[:, :LANES] + (hh_hl[:, LANES:] + jnp.dot(h_lo, w_hi, preferred_element_type=F32))
    else:
        h2_ref[...] = h2.astype(BF16)


def _merge(a, ot, gates, x2, wa, wb, wo, fg, wr):
    n = x2.shape[0]
    with_router = wr is not None
    full = lambda shape: pl.BlockSpec(shape, lambda i: (0,) * len(shape))
    row = lambda w: pl.BlockSpec((TM_MERGE, w), lambda i: (i, 0))
    in_specs = [
        row(GM_WIDTH),
        row(V_WIDTH),
        row(2 * D_MODEL), row(D_MODEL),
        full((GM_WIDTH, D_MODEL)), full((V_WIDTH, D_MODEL)), full((D_MODEL, D_MODEL)), full((1, D_MODEL)),
    ]
    args = [a, ot, gates, x2, wa, wb, wo, fg]
    if with_router:
        in_specs.append(full((D_MODEL, LANES)))
        out_specs = [row(D_MODEL), row(LANES)]
        out_shape = [jax.ShapeDtypeStruct((n, D_MODEL), F32), jax.ShapeDtypeStruct((n, LANES), F32)]
        args.append(wr)
    else:
        out_specs = [row(D_MODEL), row(D_MODEL)]
        out_shape = [jax.ShapeDtypeStruct((n, D_MODEL), F32), jax.ShapeDtypeStruct((n, D_MODEL), BF16)]
    return pl.pallas_call(
        functools.partial(_merge_kernel, with_router=with_router),
        grid=(n // TM_MERGE,),
        in_specs=in_specs, out_specs=out_specs, out_shape=out_shape,
        compiler_params=_cparams(("parallel",)),
        name="merge_router" if with_router else "merge",
    )(*args)


def _round_up_tile(c):
    return (c + (TM_FFN - 1)) & (-TM_FFN)


def _route_kernel(lg_ref, br_ref, imeta_ref, wcol_ref, cnt_ref, run_ref):
    @pl.when(pl.program_id(0) == 0)
    def _():
        run_ref[...] = jnp.zeros_like(run_ref)

    lt = (lg_ref[...] + br_ref[...]).T[:N_EXPERTS, :]
    sub = lax.broadcasted_iota(jnp.int32, lt.shape, 0)
    m1 = jnp.max(lt, axis=0, keepdims=True)
    i1 = jnp.min(jnp.where(lt == m1, sub, N_EXPERTS), axis=0, keepdims=True)
    rest = jnp.where(sub == i1, -jnp.inf, lt)
    m2 = jnp.max(rest, axis=0, keepdims=True)
    i2 = jnp.min(jnp.where(rest == m2, sub, N_EXPERTS), axis=0, keepdims=True)
    e2 = jnp.exp(m2 - m1)
    w1 = 1.0 / (1.0 + e2)
    w2 = e2 / (1.0 + e2)

    onehot = jnp.where((sub == i1) | (sub == i2), 1.0, 0.0)
    src = lax.broadcasted_iota(jnp.int32, (TM_ROUTE, TM_ROUTE), 0)
    dst = lax.broadcasted_iota(jnp.int32, (TM_ROUTE, TM_ROUTE), 1)
    earlier = jnp.where(src < dst, 1.0, 0.0).astype(BF16)
    seen = jnp.dot(onehot.astype(BF16), earlier, preferred_element_type=F32) + run_ref[:, :1]
    r1 = jnp.sum(jnp.where(sub == i1, seen, 0.0), axis=0, keepdims=True).astype(jnp.int32)
    r2 = jnp.sum(jnp.where(sub == i2, seen, 0.0), axis=0, keepdims=True).astype(jnp.int32)
    imeta_ref[...] = jnp.where(sub == 0, i1, jnp.where(sub == 1, i2, jnp.where(sub == 2, r1, jnp.where(sub == 3, r2, 0))))

    total = run_ref[...] + jnp.sum(onehot, axis=1, keepdims=True)
    run_ref[...] = total
    cnt_ref[...] = total.astype(jnp.int32)

    sub_w = lax.broadcasted_iota(jnp.int32, (LANES, TM_ROUTE), 0)
    wcol_ref[...] = jnp.where(sub_w == 0, w1, jnp.where(sub_w == 1, w2, 0.0)).T


def _route(logits, br):
    n = logits.shape[0]
    return pl.pallas_call(
        _route_kernel,
        grid=(n // TM_ROUTE,),
        in_specs=[pl.BlockSpec((TM_ROUTE, LANES), lambda i: (i, 0)),
                  pl.BlockSpec((1, LANES), lambda i: (0, 0))],
        out_specs=[pl.BlockSpec((N_EXPERTS, TM_ROUTE), lambda i: (0, i)),
                   pl.BlockSpec((TM_ROUTE, LANES), lambda i: (i, 0)),
                   pl.BlockSpec((N_EXPERTS, LANES), lambda i: (0, 0))],
        out_shape=[jax.ShapeDtypeStruct((N_EXPERTS, n), jnp.int32),
                   jax.ShapeDtypeStruct((n, LANES), F32),
                   jax.ShapeDtypeStruct((N_EXPERTS, LANES), jnp.int32)],
        scratch_shapes=[pltpu.VMEM((N_EXPERTS, LANES), F32)],
        compiler_params=_cparams(("arbitrary",)),
        name="route",
    )(logits, br)


def _positions_kernel(cnt_ref, imeta_ref, pos_ref):
    im = imeta_ref[...]
    i1, i2, r1, r2 = im[0:1], im[1:2], im[2:3], im[3:4]
    start = jnp.int32(0)
    p1 = jnp.zeros_like(i1)
    p2 = jnp.zeros_like(i2)
    for e in range(N_EXPERTS):
        p1 = jnp.where(i1 == e, start, p1)
        p2 = jnp.where(i2 == e, start, p2)
        start = start + _round_up_tile(cnt_ref[e])
    pos_ref[...] = jnp.concatenate([p1 + r1, p2 + r2], axis=0)


def _positions(cnt, imeta):
    n = imeta.shape[1]
    return pl.pallas_call(
        _positions_kernel,
        grid_spec=pltpu.PrefetchScalarGridSpec(
            num_scalar_prefetch=1,
            grid=(n // TM_ROUTE,),
            in_specs=[pl.BlockSpec((N_EXPERTS, TM_ROUTE), lambda i, c: (0, i))],
            out_specs=pl.BlockSpec((2, TM_ROUTE), lambda i, c: (0, i)),
        ),
        out_shape=jax.ShapeDtypeStruct((2, n), jnp.int32),
        compiler_params=_cparams(("parallel",)),
        name="positions",
    )(cnt, imeta)


def _scatter_kernel(pos_ref, cnt_ref, xn_ref, fg_ref, xs_ref, hbuf, zbuf, sem, zsem, *, n_tokens):
    n_tiles = xs_ref.shape[0] // TM_FFN

    @pl.when(pl.program_id(0) == 0)
    def _():
        zbuf[...] = jnp.zeros_like(zbuf)

        def zero_tile(t):
            rows = pl.ds(pl.multiple_of(t * TM_FFN, TM_FFN), TM_FFN)
            return pltpu.make_async_copy(zbuf, xs_ref.at[rows], zsem)

        def each_zero_tile(action):
            end = jnp.int32(0)
            for e in range(N_EXPERTS):
                tiles = _round_up_tile(cnt_ref[e]) // TM_FFN
                end = end + tiles

                @pl.when(tiles > 0)
                def _(end=end):
                    action(zero_tile(end - 1))

            def tail(t, carry):
                action(zero_tile(t))
                return carry

            lax.fori_loop(end, n_tiles, tail, 0)

        each_zero_tile(lambda copy: copy.start())
        each_zero_tile(lambda copy: copy.wait())

    base = pl.program_id(0) * TM_SCAT
    hbuf[...] = _rms(xn_ref[...], fg_ref[...])

    def issue(r, carry):
        for k in range(2):
            dst = pos_ref[k * n_tokens + base + r]
            pltpu.make_async_copy(hbuf.at[pl.ds(r, 1)], xs_ref.at[pl.ds(dst, 1)], sem).start()
        return carry

    lax.fori_loop(0, TM_SCAT, issue, 0, unroll=8)
    for k in range(2):
        pltpu.make_async_copy(hbuf, xs_ref.at[pl.ds(0, TM_SCAT)], sem).wait()


def _scatter(pos, cnt, xn, fg):
    n = xn.shape[0]
    rows = 2 * n + N_EXPERTS * TM_FFN
    return pl.pallas_call(
        functools.partial(_scatter_kernel, n_tokens=n),
        grid_spec=pltpu.PrefetchScalarGridSpec(
            num_scalar_prefetch=2,
            grid=(n // TM_SCAT,),
            in_specs=[pl.BlockSpec((TM_SCAT, D_MODEL), lambda i, p, c: (i, 0)),
                      pl.BlockSpec((1, D_MODEL), lambda i, p, c: (0, 0))],
            out_specs=pl.BlockSpec(memory_space=pl.ANY),
            scratch_shapes=[pltpu.VMEM((TM_SCAT, D_MODEL), F32), pltpu.VMEM((TM_FFN, D_MODEL), F32),
                            pltpu.SemaphoreType.DMA(()), pltpu.SemaphoreType.DMA(())],
        ),
        out_shape=jax.ShapeDtypeStruct((rows, D_MODEL), F32),
        compiler_params=_cparams(("arbitrary",)),
        name="moe_scatter",
    )(pos, cnt, xn, fg)


def _combine_kernel(pos_ref, xn_ref, wcol_ref, ys_ref, o_ref, buf, sem, *, n_tokens):
    base = pl.program_id(0) * TM_COMB

    def issue(r, carry):
        for k in range(2):
            src = pos_ref[k * n_tokens + base + r]
            pltpu.make_async_copy(ys_ref.at[pl.ds(src, 1)], buf.at[k, pl.ds(r, 1)], sem).start()
        return carry

    lax.fori_loop(0, TM_COMB, issue, 0, unroll=8)
    for k in range(2):
        pltpu.make_async_copy(ys_ref.at[pl.ds(0, TM_COMB)], buf.at[k], sem).wait()
    w = wcol_ref[...]
    o_ref[...] = xn_ref[...] + (w[:, 0:1] * buf[0] + w[:, 1:2] * buf[1])


def _combine(pos, xn, wcol, ys):
    n = xn.shape[0]
    return pl.pallas_call(
        functools.partial(_combine_kernel, n_tokens=n),
        grid_spec=pltpu.PrefetchScalarGridSpec(
            num_scalar_prefetch=1,
            grid=(n // TM_COMB,),
            in_specs=[pl.BlockSpec((TM_COMB, D_MODEL), lambda i, p: (i, 0)),
                      pl.BlockSpec((TM_COMB, LANES), lambda i, p: (i, 0)),
                      pl.BlockSpec(memory_space=pl.ANY)],
            out_specs=pl.BlockSpec((TM_COMB, D_MODEL), lambda i, p: (i, 0)),
            scratch_shapes=[pltpu.VMEM((2, TM_COMB, D_MODEL), F32), pltpu.SemaphoreType.DMA(())],
        ),
        out_shape=jax.ShapeDtypeStruct((n, D_MODEL), F32),
        compiler_params=_cparams(("arbitrary",)),
        name="moe_combine",
    )(pos, xn, wcol, ys)


def _swiglu_partial(h, wg_ref, wu_ref, wd_ref):
    g = jnp.dot(h, wg_ref[...], preferred_element_type=F32)
    u = jnp.dot(h, wu_ref[...], preferred_element_type=F32)
    act = (g * jax.nn.sigmoid(g) * u).astype(BF16)
    return jnp.dot(act, wd_ref[...], preferred_element_type=F32)


def _ffn_dense_kernel(h_ref, x_ref, wg_ref, wu_ref, wd_ref, o_ref):
    @pl.when(pl.program_id(1) == 0)
    def _():
        o_ref[...] = x_ref[...]

    o_ref[...] += _swiglu_partial(h_ref[...], wg_ref, wu_ref, wd_ref)


def _ffn_dense(h2, x2, wgu, wd):
    n = h2.shape[0]
    nf = FF // TF_FFN
    return pl.pallas_call(
        _ffn_dense_kernel,
        grid=(n // TM_FFN, nf),
        in_specs=[
            pl.BlockSpec((TM_FFN, D_MODEL), lambda i, j: (i, 0)),
            pl.BlockSpec((TM_FFN, D_MODEL), lambda i, j: (i, 0)),
            pl.BlockSpec((D_MODEL, TF_FFN), lambda i, j: (0, j)),
            pl.BlockSpec((D_MODEL, TF_FFN), lambda i, j: (0, j + nf)),
            pl.BlockSpec((TF_FFN, D_MODEL), lambda i, j: (j, 0)),
        ],
        out_specs=pl.BlockSpec((TM_FFN, D_MODEL), lambda i, j: (i, 0)),
        out_shape=jax.ShapeDtypeStruct((n, D_MODEL), F32),
        compiler_params=_cparams(("parallel", "arbitrary")),
        name="ffn_dense",
    )(h2, x2, wgu, wgu, wd)


def _tile_plan(i, cnt_ref):
    end = jnp.int32(0)
    expert = jnp.int32(0)
    for e in range(N_EXPERTS):
        end = end + _round_up_tile(cnt_ref[e])
        expert = expert + (end <= i * TM_FFN).astype(jnp.int32)
    return jnp.minimum(expert, N_EXPERTS - 1), end // TM_FFN


def _ffn_grouped_kernel(cnt_ref, xs_ref, wg_ref, wu_ref, wd_ref, o_ref):
    i, j = pl.program_id(0), pl.program_id(1)
    _, used = _tile_plan(i, cnt_ref)

    @pl.when(i < used)
    def _():
        y = _swiglu_partial(xs_ref[...].astype(BF16), wg_ref, wu_ref, wd_ref)

        @pl.when(j == 0)
        def _():
            o_ref[...] = y

        @pl.when(j > 0)
        def _():
            o_ref[...] += y

    @pl.when((i >= used) & (j == 0))
    def _():
        o_ref[...] = jnp.zeros_like(o_ref)


def _ffn_grouped(cnt, xs, wgu, wd):
    rows = xs.shape[0]
    nf = FF // TF_FFN

    def tile(i, c):
        return jnp.minimum(i, _tile_plan(i, c)[1] - 1)

    def half(i, j, c):
        return jnp.where(i < _tile_plan(i, c)[1], j, nf - 1)

    return pl.pallas_call(
        _ffn_grouped_kernel,
        grid_spec=pltpu.PrefetchScalarGridSpec(
            num_scalar_prefetch=1,
            grid=(rows // TM_FFN, nf),
            in_specs=[
                pl.BlockSpec((TM_FFN, D_MODEL), lambda i, j, c: (tile(i, c), 0)),
                pl.BlockSpec((None, D_MODEL, TF_FFN), lambda i, j, c: (_tile_plan(i, c)[0], 0, half(i, j, c))),
                pl.BlockSpec((None, D_MODEL, TF_FFN), lambda i, j, c: (_tile_plan(i, c)[0], 0, half(i, j, c) + nf)),
                pl.BlockSpec((None, TF_FFN, D_MODEL), lambda i, j, c: (_tile_plan(i, c)[0], half(i, j, c), 0)),
            ],
            out_specs=pl.BlockSpec((TM_FFN, D_MODEL), lambda i, j, c: (i, 0)),
        ),
        out_shape=jax.ShapeDtypeStruct((rows, D_MODEL), F32),
        compiler_params=_cparams(("arbitrary", "arbitrary")),
        name="ffn_grouped",
    )(cnt, xs, wgu, wgu, wd)


def _moe(xn, logits, br, fg, wgu, wd):
    n = xn.shape[0]
    imeta, wcol, cnt_lanes = _route(logits, br)
    cnt = cnt_lanes[:, 0]
    pos = _positions(cnt, imeta).reshape(2 * n)
    xs = _scatter(pos, cnt, xn, fg)
    ys = _ffn_grouped(cnt, xs, wgu, wd)
    return _combine(pos, xn, wcol, ys)


def _rope_tables(seq):
    pos = jnp.arange(seq, dtype=F32)
    inv_freq = ROPE_THETA ** (-jnp.arange(0, QK_ROPE, 2, dtype=F32) / QK_ROPE)
    ang = pos[:, None] * inv_freq[None, :]
    return jnp.cos(ang), jnp.sin(ang)


def _rotary_slot_tables(cos, sin, gain, nope_gain, scale):
    seq = cos.shape[0]
    half = QK_ROPE // 2
    g1, g2 = gain[QK_NOPE:QK_NOPE + half], gain[QK_NOPE + half:QK_DIM]
    z = lambda w: jnp.zeros((seq, w), F32)
    tail = HEAD_SLOT - QK_DIM
    a = jnp.concatenate([jnp.broadcast_to(nope_gain, (seq, QK_NOPE)), g1 * cos, g2 * cos, z(tail)], axis=1)
    b = jnp.concatenate([z(QK_NOPE), -g2 * sin, g1 * sin, z(tail)], axis=1)
    return a * scale, b * scale


def _swap_rotary_halves(w):
    half = QK_ROPE // 2
    sw = jnp.concatenate([w[..., half:], w[..., :half]], axis=-1)
    pad = [(0, 0)] * (w.ndim - 1) + [(QK_NOPE, HEAD_SLOT - QK_DIM)]
    return jnp.pad(sw, pad)


def _head_slots(w, width):
    k = w.shape[0]
    w3 = w.reshape(k, MLA_HEADS, width)
    return jnp.pad(w3, ((0, 0), (0, 0), (0, HEAD_SLOT - width))).reshape(k, QK_WIDTH)


def _pad_lanes(v, width=LANES):
    return jnp.pad(v, (0, width - v.shape[0])).reshape(1, width)


def kernel(x, mix_norm_g, w_in, gm_v_norm_g, gm_v_norm_b, gm_w_spatial, gm_b_spatial, gm_w_proj, mla_q_lat_g, mla_w_uq, mla_kv_lat_g, mla_w_ukv, mla_q_norm_g, mla_k_norm_g, mla_w_proj, w_out, ffn_norm_g, dense_w_gu, dense_w_down, moe_w_router, moe_b_router, moe_w_gu, moe_w_down):
    batch, seq, d = x.shape
    n = batch * seq
    depth = w_in.shape[0]
    cos, sin = _rope_tables(seq)
    x2 = x.reshape(n, d)
    q_scale = float(np.log2(np.e) / np.sqrt(QK_DIM))
    v_ones = jnp.tile(jnp.zeros((HEAD_SLOT,), F32).at[V_DIM].set(1.0), MLA_HEADS).reshape(1, QK_WIDTH)

    o_v = GM_WIDTH
    o_cq = 2 * GM_WIDTH
    o_ckv = o_cq + Q_LORA
    o_kr = o_ckv + KV_LORA
    o_gate = o_kr + QK_ROPE

    for l in range(depth):
        wl = w_in[l]
        w_kr = jnp.pad(wl[:, o_kr:o_gate], ((0, 0), (QK_NOPE, LANES - QK_NOPE - QK_ROPE)))
        w1 = jnp.concatenate([wl[:, :o_kr], w_kr, _swap_rotary_halves(wl[:, o_kr:o_gate]), wl[:, o_gate:]],
                             axis=1).astype(BF16)
        uv, lat, gates = _in_proj(x2, mix_norm_g[l].reshape(1, d), w1)

        wq = _head_slots(mla_w_uq[l], QK_DIM).astype(BF16)
        wq3 = mla_w_uq[l].reshape(Q_LORA, MLA_HEADS, QK_DIM)
        wqs = _swap_rotary_halves(wq3[:, :, QK_NOPE:]).reshape(Q_LORA, QK_WIDTH).astype(BF16)
        wkv3 = mla_w_ukv[l].reshape(KV_LORA, MLA_HEADS, QK_NOPE + V_DIM)
        wk = _head_slots(wkv3[:, :, :QK_NOPE].reshape(KV_LORA, MLA_HEADS * QK_NOPE), QK_NOPE).astype(BF16)
        wv = _head_slots(wkv3[:, :, QK_NOPE:].reshape(KV_LORA, V_WIDTH), V_DIM).astype(BF16)
        gq, gk = mla_q_norm_g[l], mla_k_norm_g[l]
        aq, bq = _rotary_slot_tables(cos, sin, gq, gq[:QK_NOPE], q_scale)
        ak, bk = _rotary_slot_tables(cos, sin, gk, jnp.zeros((QK_NOPE,), F32), 1.0)
        q, k, v = _qkv_prep(lat, mla_q_lat_g[l].reshape(1, Q_LORA), wq, wqs,
                            mla_kv_lat_g[l].reshape(1, KV_LORA), wk, wv, v_ones,
                            aq, bq, ak, bk, _pad_lanes(gk[:QK_NOPE]), seq)
        slots = lambda t: t.reshape(batch, seq, QK_WIDTH)
        ot = _attention(slots(q), slots(k), slots(v)).reshape(n, V_WIDTH)

        a = _gmlp(uv, gm_v_norm_g[l].reshape(1, GM_WIDTH), gm_v_norm_b[l].reshape(1, GM_WIDTH),
                  gm_w_spatial[l], gm_b_spatial[l].T)

        is_moe = l % 2 == 1
        m = l // 2
        wr = jnp.pad(moe_w_router[m], ((0, 0), (0, LANES - N_EXPERTS))) if is_moe else None
        outs = _merge(a, ot, gates, x2, gm_w_proj[l].astype(BF16), mla_w_proj[l].astype(BF16),
                      w_out[l].astype(BF16), ffn_norm_g[l].reshape(1, d), wr)
        if is_moe:
            xn, logits = outs
            x2 = _moe(xn, logits, _pad_lanes(moe_b_router[m]), ffn_norm_g[l].reshape(1, d),
                      moe_w_gu[m].astype(BF16), moe_w_down[m].astype(BF16))
        else:
            xn, h2 = outs
            x2 = _ffn_dense(h2, xn, dense_w_gu[m].astype(BF16), dense_w_down[m].astype(BF16))
    return x2.reshape(batch, seq, d)
```

```python
import functools

import jax
import jax.numpy as jnp
import numpy as np
from jax import lax
from jax.experimental import pallas as pl
from jax.experimental.pallas import tpu as pltpu

F32 = jnp.float32
BF16 = jnp.bfloat16

EPS = 1e-6
LANES = 128

D_MODEL = 1024
GM_GROUPS = 8
GM_GROUP_CH = 128
GM_WIDTH = 1024
GM_CHUNK = 128
MLA_HEADS = 16
QK_NOPE = 64
QK_ROPE = 32
QK_DIM = 96
V_DIM = 64
Q_LORA = 512
KV_LORA = 256
ROPE_THETA = 10000.0
HEAD_SLOT = LANES
QK_WIDTH = MLA_HEADS * HEAD_SLOT
V_WIDTH = MLA_HEADS * V_DIM
LAT_WIDTH = Q_LORA + KV_LORA + 2 * LANES
N_EXPERTS = 8
FF = 2816

TM_IN = 256
TM_QKV = 512
TM_GMLP = 512
TM_MERGE = 512
TM_FFN = 512
FF_CHUNKS = (768, 768, 768, 512)
TM_ROUTE = 512
TM_SCAT = 256
TM_COMB = 256
ATT_T = 512
ATT_G = 2

VMEM_LIMIT = 56 * 1024 * 1024


def _cparams(sem):
    return pltpu.CompilerParams(dimension_semantics=sem, vmem_limit_bytes=VMEM_LIMIT)


def _rms(xf, g):
    return xf * lax.rsqrt(jnp.mean(xf * xf, axis=-1, keepdims=True) + EPS) * g


def _in_proj_kernel(x_ref, g_ref, w_ref, uv_ref, lat_ref, gate_ref):
    h = _rms(x_ref[...], g_ref[...]).astype(BF16)
    o_lat = 2 * GM_WIDTH
    o_gate = o_lat + LAT_WIDTH
    uv_ref[...] = jnp.dot(h, w_ref[:, :o_lat], preferred_element_type=F32).astype(BF16)
    lat_ref[...] = jnp.dot(h, w_ref[:, o_lat:o_gate], preferred_element_type=F32).astype(BF16)
    gate_ref[...] = jnp.dot(h, w_ref[:, o_gate:], preferred_element_type=F32).astype(BF16)


def _in_proj(x2, g, w1):
    n = x2.shape[0]
    wcols = w1.shape[1]
    return pl.pallas_call(
        _in_proj_kernel,
        grid=(n // TM_IN,),
        in_specs=[
            pl.BlockSpec((TM_IN, D_MODEL), lambda i: (i, 0)),
            pl.BlockSpec((1, D_MODEL), lambda i: (0, 0)),
            pl.BlockSpec((D_MODEL, wcols), lambda i: (0, 0)),
        ],
        out_specs=[
            pl.BlockSpec((TM_IN, 2 * GM_WIDTH), lambda i: (i, 0)),
            pl.BlockSpec((TM_IN, LAT_WIDTH), lambda i: (i, 0)),
            pl.BlockSpec((TM_IN, 2 * D_MODEL), lambda i: (i, 0)),
        ],
        out_shape=[
            jax.ShapeDtypeStruct((n, 2 * GM_WIDTH), BF16),
            jax.ShapeDtypeStruct((n, LAT_WIDTH), BF16),
            jax.ShapeDtypeStruct((n, 2 * D_MODEL), BF16),
        ],
        compiler_params=_cparams(("parallel",)),
        name="in_proj",
    )(x2, g, w1)


def _qkv_prep_kernel(lat_ref, qg_ref, wq_ref, wqs_ref, kvg_ref, wk_ref, wv_ref, vone_ref,
                     aq_ref, bq_ref, ak_ref, bk_ref, gkn_ref, q_ref, k_ref, v_ref):
    lat = lat_ref[...].astype(F32)
    cq = _rms(lat[:, :Q_LORA], qg_ref[...]).astype(BF16)
    ckv = _rms(lat[:, Q_LORA:Q_LORA + KV_LORA], kvg_ref[...]).astype(BF16)
    kr = lat[:, Q_LORA + KV_LORA:Q_LORA + KV_LORA + LANES]
    kr_sw = lat[:, Q_LORA + KV_LORA + LANES:]
    kr_ssq = jnp.sum(kr * kr, axis=-1, keepdims=True)
    kr_rot = kr * ak_ref[...] + kr_sw * bk_ref[...]
    aq, bq, gkn = aq_ref[...], bq_ref[...], gkn_ref[...]
    q_all = jnp.dot(cq, wq_ref[...], preferred_element_type=F32)
    q_sw = jnp.dot(cq, wqs_ref[...], preferred_element_type=F32)
    k_all = jnp.dot(ckv, wk_ref[...], preferred_element_type=F32)
    for h in range(MLA_HEADS):
        sl = slice(h * HEAD_SLOT, (h + 1) * HEAD_SLOT)
        qh = q_all[:, sl]
        rq = lax.rsqrt(jnp.sum(qh * qh, axis=-1, keepdims=True) * (1.0 / QK_DIM) + EPS)
        q_ref[:, sl] = ((qh * aq + q_sw[:, sl] * bq) * rq).astype(BF16)
        kh = k_all[:, sl]
        rk = lax.rsqrt((jnp.sum(kh * kh, axis=-1, keepdims=True) + kr_ssq) * (1.0 / QK_DIM) + EPS)
        k_ref[:, sl] = ((kh * gkn + kr_rot) * rk).astype(BF16)
    v_ref[...] = (jnp.dot(ckv, wv_ref[...], preferred_element_type=F32) + vone_ref[...]).astype(BF16)


def _qkv_prep(lat, qg, wq, wqs, kvg, wk, wv, vone, aq, bq, ak, bk, gkn, seq):
    n = lat.shape[0]
    tps = seq // TM_QKV
    full = lambda shape: pl.BlockSpec(shape, lambda i: (0,) * len(shape))
    rope = pl.BlockSpec((TM_QKV, HEAD_SLOT), lambda i: (i % tps, 0))
    return pl.pallas_call(
        _qkv_prep_kernel,
        grid=(n // TM_QKV,),
        in_specs=[
            pl.BlockSpec((TM_QKV, LAT_WIDTH), lambda i: (i, 0)),
            full((1, Q_LORA)), full((Q_LORA, QK_WIDTH)), full((Q_LORA, QK_WIDTH)),
            full((1, KV_LORA)), full((KV_LORA, QK_WIDTH)), full((KV_LORA, QK_WIDTH)), full((1, QK_WIDTH)),
            rope, rope, rope, rope, full((1, HEAD_SLOT)),
        ],
        out_specs=[pl.BlockSpec((TM_QKV, QK_WIDTH), lambda i: (i, 0))] * 3,
        out_shape=[jax.ShapeDtypeStruct((n, QK_WIDTH), BF16)] * 3,
        compiler_params=_cparams(("parallel",)),
        name="qkv_prep",
    )(lat, qg, wq, wqs, kvg, wk, wv, vone, aq, bq, ak, bk, gkn)


def _attention_kernel(q_ref, k_ref, v_ref, o_ref, *, seq):
    nblk = seq // ATT_T
    key_pos = lax.broadcasted_iota(jnp.int32, (ATT_T, ATT_T), 0)
    qry_pos = lax.broadcasted_iota(jnp.int32, (ATT_T, ATT_T), 1)
    causal = key_pos <= qry_pos
    nt = (((1,), (1,)), ((), ()))
    tn = (((0,), (0,)), ((), ()))

    def update(state, g, st, kj, diagonal):
        m, acc = state
        if diagonal:
            st = jnp.where(causal, st, -1e30)
        m_new = jnp.maximum(m, jnp.max(st, axis=0, keepdims=True))
        alpha = jnp.exp2(m - m_new)
        p = jnp.exp2((st - m_new).astype(BF16))
        v_blk = v_ref[kj * ATT_T:(kj + 1) * ATT_T, g * HEAD_SLOT:(g + 1) * HEAD_SLOT]
        return m_new, alpha * acc + lax.dot_general(v_blk, p, tn, preferred_element_type=F32)

    for qi in range(nblk):
        q0, q1 = qi * ATT_T, (qi + 1) * ATT_T
        strips = [lax.dot_general(k_ref[0:q1, g * HEAD_SLOT:(g + 1) * HEAD_SLOT],
                                  q_ref[q0:q1, g * HEAD_SLOT:(g + 1) * HEAD_SLOT], nt,
                                  preferred_element_type=F32) for g in range(ATT_G)]
        init = (jnp.full((1, ATT_T), -1e30, F32), jnp.zeros((HEAD_SLOT, ATT_T), F32))
        states = [init for _ in range(ATT_G)]
        for kj in range(qi + 1):
            states = [update(states[g], g, strips[g][kj * ATT_T:(kj + 1) * ATT_T], kj, kj == qi)
                      for g in range(ATT_G)]
        for g in range(ATT_G):
            acc = states[g][1].T
            o_ref[q0:q1, g * V_DIM:(g + 1) * V_DIM] = (acc[:, :V_DIM] / acc[:, V_DIM:V_DIM + 1]).astype(BF16)


def _attention(q3, k3, v3):
    batch, seq, _ = q3.shape
    slot = pl.BlockSpec((None, seq, ATT_G * HEAD_SLOT), lambda b, h: (b, 0, h))
    return pl.pallas_call(
        functools.partial(_attention_kernel, seq=seq),
        grid=(batch, MLA_HEADS // ATT_G),
        in_specs=[slot, slot, slot],
        out_specs=pl.BlockSpec((None, seq, ATT_G * V_DIM), lambda b, h: (b, 0, h)),
        out_shape=jax.ShapeDtypeStruct((batch, seq, V_WIDTH), BF16),
        compiler_params=_cparams(("parallel", "parallel")),
        name="attention",
    )(q3, k3, v3)


def _gmlp_kernel(uv_ref, vg_ref, vb_ref, ws_ref, bst_ref, a_ref):
    v = jax.nn.gelu(uv_ref[:, GM_WIDTH:].astype(F32))
    mu = jnp.mean(v, axis=-1, keepdims=True)
    vc = v - mu
    vn = (vc * lax.rsqrt(jnp.mean(vc * vc, axis=-1, keepdims=True) + EPS) * vg_ref[...] + vb_ref[...]).astype(BF16)
    row = lax.broadcasted_iota(jnp.int32, (GM_CHUNK, GM_CHUNK), 0)
    col = lax.broadcasted_iota(jnp.int32, (GM_CHUNK, GM_CHUNK), 1)
    tril = col <= row
    bst = bst_ref[...]
    for g in range(GM_GROUPS):
        ws = jnp.where(tril, ws_ref[g], 0.0).astype(BF16)
        bias = bst[:, g:g + 1]
        cs = slice(g * GM_GROUP_CH, (g + 1) * GM_GROUP_CH)
        for c in range(TM_GMLP // GM_CHUNK):
            rs = slice(c * GM_CHUNK, (c + 1) * GM_CHUNK)
            mixed = jnp.dot(ws, vn[rs, cs], preferred_element_type=F32) + bias
            u = jax.nn.gelu(uv_ref[rs, cs].astype(F32))
            a_ref[rs, cs] = (u * mixed).astype(BF16)


def _gmlp(uv, vg, vb, ws, bst):
    n = uv.shape[0]
    full = lambda shape: pl.BlockSpec(shape, lambda i: (0,) * len(shape))
    return pl.pallas_call(
        _gmlp_kernel,
        grid=(n // TM_GMLP,),
        in_specs=[
            pl.BlockSpec((TM_GMLP, 2 * GM_WIDTH), lambda i: (i, 0)),
            full((1, GM_WIDTH)), full((1, GM_WIDTH)),
            full((GM_GROUPS, GM_CHUNK, GM_CHUNK)), full((GM_CHUNK, GM_GROUPS)),
        ],
        out_specs=pl.BlockSpec((TM_GMLP, GM_WIDTH), lambda i: (i, 0)),
        out_shape=jax.ShapeDtypeStruct((n, GM_WIDTH), BF16),
        compiler_params=_cparams(("parallel",)),
        name="gmlp",
    )(uv, vg, vb, ws, bst)


def _split_bf16(v):
    hi = v.astype(BF16)
    return hi, (v - hi.astype(F32)).astype(BF16)


def _merge_kernel(a_ref, ot_ref, gate_ref, x_ref, wa_ref, wb_ref, wo_ref, fg_ref, *rest, with_router):
    if with_router:
        wr_ref, xn_ref, lg_ref = rest
    else:
        xn_ref, h2_ref = rest
    ya = jnp.dot(a_ref[...], wa_ref[...], preferred_element_type=F32)
    yb = jnp.dot(ot_ref[...], wb_ref[...], preferred_element_type=F32)
    ga = jax.nn.sigmoid(gate_ref[:, :D_MODEL].astype(F32))
    gb = jax.nn.sigmoid(gate_ref[:, D_MODEL:].astype(F32))
    merged = (ga * ya + gb * yb).astype(BF16)
    xn = x_ref[...] + jnp.dot(merged, wo_ref[...], preferred_element_type=F32)
    xn_ref[...] = xn
    h2 = _rms(xn, fg_ref[...])
    if with_router:
        h_hi, h_lo = _split_bf16(h2)
        w_hi, w_lo = _split_bf16(wr_ref[...])
        hh_hl = jnp.dot(h_hi, jnp.concatenate([w_hi, w_lo], axis=1), preferred_element_type=F32)
        lg_ref[...] = hh_hl[:, :LANES] + (hh_hl[:, LANES:] + jnp.dot(h_lo, w_hi, preferred_element_type=F32))
    else:
        h2_ref[...] = h2.astype(BF16)


def _merge(a, ot, gates, x2, wa, wb, wo, fg, wr):
    n = x2.shape[0]
    with_router = wr is not None
    full = lambda shape: pl.BlockSpec(shape, lambda i: (0,) * len(shape))
    row = lambda w: pl.BlockSpec((TM_MERGE, w), lambda i: (i, 0))
    in_specs = [
        row(GM_WIDTH),
        row(V_WIDTH),
        row(2 * D_MODEL), row(D_MODEL),
        full((GM_WIDTH, D_MODEL)), full((V_WIDTH, D_MODEL)), full((D_MODEL, D_MODEL)), full((1, D_MODEL)),
    ]
    args = [a, ot, gates, x2, wa, wb, wo, fg]
    if with_router:
        in_specs.append(full((D_MODEL, LANES)))
        out_specs = [row(D_MODEL), row(LANES)]
        out_shape = [jax.ShapeDtypeStruct((n, D_MODEL), F32), jax.ShapeDtypeStruct((n, LANES), F32)]
        args.append(wr)
    else:
        out_specs = [row(D_MODEL), row(D_MODEL)]
        out_shape = [jax.ShapeDtypeStruct((n, D_MODEL), F32), jax.ShapeDtypeStruct((n, D_MODEL), BF16)]
    return pl.pallas_call(
        functools.partial(_merge_kernel, with_router=with_router),
        grid=(n // TM_MERGE,),
        in_specs=in_specs, out_specs=out_specs, out_shape=out_shape,
        compiler_params=_cparams(("parallel",)),
        name="merge_router" if with_router else "merge",
    )(*args)


def _round_up_tile(c):
    return (c + (TM_FFN - 1)) & (-TM_FFN)


def _route_kernel(lg_ref, br_ref, imeta_ref, wcol_ref, cnt_ref, run_ref):
    @pl.when(pl.program_id(0) == 0)
    def _():
        run_ref[...] = jnp.zeros_like(run_ref)

    lt = (lg_ref[...] + br_ref[...]).T[:N_EXPERTS, :]
    sub = lax.broadcasted_iota(jnp.int32, lt.shape, 0)
    m1 = jnp.max(lt, axis=0, keepdims=True)
    i1 = jnp.min(jnp.where(lt == m1, sub, N_EXPERTS), axis=0, keepdims=True)
    rest = jnp.where(sub == i1, -jnp.inf, lt)
    m2 = jnp.max(rest, axis=0, keepdims=True)
    i2 = jnp.min(jnp.where(rest == m2, sub, N_EXPERTS), axis=0, keepdims=True)
    e2 = jnp.exp(m2 - m1)
    w1 = 1.0 / (1.0 + e2)
    w2 = e2 / (1.0 + e2)

    onehot = jnp.where((sub == i1) | (sub == i2), 1.0, 0.0)
    src = lax.broadcasted_iota(jnp.int32, (TM_ROUTE, TM_ROUTE), 0)
    dst = lax.broadcasted_iota(jnp.int32, (TM_ROUTE, TM_ROUTE), 1)
    earlier = jnp.where(src < dst, 1.0, 0.0).astype(BF16)
    seen = jnp.dot(onehot.astype(BF16), earlier, preferred_element_type=F32) + run_ref[:, :1]
    r1 = jnp.sum(jnp.where(sub == i1, seen, 0.0), axis=0, keepdims=True).astype(jnp.int32)
    r2 = jnp.sum(jnp.where(sub == i2, seen, 0.0), axis=0, keepdims=True).astype(jnp.int32)
    imeta_ref[...] = jnp.where(sub == 0, i1, jnp.where(sub == 1, i2, jnp.where(sub == 2, r1, jnp.where(sub == 3, r2, 0))))

    total = run_ref[...] + jnp.sum(onehot, axis=1, keepdims=True)
    run_ref[...] = total
    cnt_ref[...] = total.astype(jnp.int32)

    sub_w = lax.broadcasted_iota(jnp.int32, (LANES, TM_ROUTE), 0)
    wcol_ref[...] = jnp.where(sub_w == 0, w1, jnp.where(sub_w == 1, w2, 0.0)).T


def _route(logits, br):
    n = logits.shape[0]
    return pl.pallas_call(
        _route_kernel,
        grid=(n // TM_ROUTE,),
        in_specs=[pl.BlockSpec((TM_ROUTE, LANES), lambda i: (i, 0)),
                  pl.BlockSpec((1, LANES), lambda i: (0, 0))],
        out_specs=[pl.BlockSpec((N_EXPERTS, TM_ROUTE), lambda i: (0, i)),
                   pl.BlockSpec((TM_ROUTE, LANES), lambda i: (i, 0)),
                   pl.BlockSpec((N_EXPERTS, LANES), lambda i: (0, 0))],
        out_shape=[jax.ShapeDtypeStruct((N_EXPERTS, n), jnp.int32),
                   jax.ShapeDtypeStruct((n, LANES), F32),
                   jax.ShapeDtypeStruct((N_EXPERTS, LANES), jnp.int32)],
        scratch_shapes=[pltpu.VMEM((N_EXPERTS, LANES), F32)],
        compiler_params=_cparams(("arbitrary",)),
        name="route",
    )(logits, br)


def _positions_kernel(cnt_ref, imeta_ref, pos_ref):
    im = imeta_ref[...]
    i1, i2, r1, r2 = im[0:1], im[1:2], im[2:3], im[3:4]
    start = jnp.int32(0)
    p1 = jnp.zeros_like(i1)
    p2 = jnp.zeros_like(i2)
    for e in range(N_EXPERTS):
        p1 = jnp.where(i1 == e, start, p1)
        p2 = jnp.where(i2 == e, start, p2)
        start = start + _round_up_tile(cnt_ref[e])
    pos_ref[...] = jnp.concatenate([p1 + r1, p2 + r2], axis=0)


def _positions(cnt, imeta):
    n = imeta.shape[1]
    return pl.pallas_call(
        _positions_kernel,
        grid_spec=pltpu.PrefetchScalarGridSpec(
            num_scalar_prefetch=1,
            grid=(n // TM_ROUTE,),
            in_specs=[pl.BlockSpec((N_EXPERTS, TM_ROUTE), lambda i, c: (0, i))],
            out_specs=pl.BlockSpec((2, TM_ROUTE), lambda i, c: (0, i)),
        ),
        out_shape=jax.ShapeDtypeStruct((2, n), jnp.int32),
        compiler_params=_cparams(("parallel",)),
        name="positions",
    )(cnt, imeta)


def _scatter_kernel(pos_ref, cnt_ref, xn_ref, fg_ref, xs_ref, hbuf, zbuf, sem, zsem, *, n_tokens):
    n_tiles = xs_ref.shape[0] // TM_FFN

    @pl.when(pl.program_id(0) == 0)
    def _():
        zbuf[...] = jnp.zeros_like(zbuf)

        def zero_tile(t):
            rows = pl.ds(pl.multiple_of(t * TM_FFN, TM_FFN), TM_FFN)
            return pltpu.make_async_copy(zbuf, xs_ref.at[rows], zsem)

        def each_zero_tile(action):
            end = jnp.int32(0)
            for e in range(N_EXPERTS):
                tiles = _round_up_tile(cnt_ref[e]) // TM_FFN
                end = end + tiles

                @pl.when(tiles > 0)
                def _(end=end):
                    action(zero_tile(end - 1))

            def tail(t, carry):
                action(zero_tile(t))
                return carry

            lax.fori_loop(end, n_tiles, tail, 0)

        each_zero_tile(lambda copy: copy.start())
        each_zero_tile(lambda copy: copy.wait())

    base = pl.program_id(0) * TM_SCAT
    hbuf[...] = _rms(xn_ref[...], fg_ref[...])

    def issue(r, carry):
        for k in range(2):
            dst = pos_ref[k * n_tokens + base + r]
            pltpu.make_async_copy(hbuf.at[pl.ds(r, 1)], xs_ref.at[pl.ds(dst, 1)], sem).start()
        return carry

    lax.fori_loop(0, TM_SCAT, issue, 0, unroll=8)
    for k in range(2):
        pltpu.make_async_copy(hbuf, xs_ref.at[pl.ds(0, TM_SCAT)], sem).wait()


def _scatter(pos, cnt, xn, fg):
    n = xn.shape[0]
    rows = 2 * n + N_EXPERTS * TM_FFN
    return pl.pallas_call(
        functools.partial(_scatter_kernel, n_tokens=n),
        grid_spec=pltpu.PrefetchScalarGridSpec(
            num_scalar_prefetch=2,
            grid=(n // TM_SCAT,),
            in_specs=[pl.BlockSpec((TM_SCAT, D_MODEL), lambda i, p, c: (i, 0)),
                      pl.BlockSpec((1, D_MODEL), lambda i, p, c: (0, 0))],
            out_specs=pl.BlockSpec(memory_space=pl.ANY),
            scratch_shapes=[pltpu.VMEM((TM_SCAT, D_MODEL), F32), pltpu.VMEM((TM_FFN, D_MODEL), F32),
                            pltpu.SemaphoreType.DMA(()), pltpu.SemaphoreType.DMA(())],
        ),
        out_shape=jax.ShapeDtypeStruct((rows, D_MODEL), F32),
        compiler_params=_cparams(("arbitrary",)),
        name="moe_scatter",
    )(pos, cnt, xn, fg)


def _combine_kernel(pos_ref, xn_ref, wcol_ref, ys_ref, o_ref, buf, sem, *, n_tokens):
    base = pl.program_id(0) * TM_COMB

    def issue(r, carry):
        for k in range(2):
            src = pos_ref[k * n_tokens + base + r]
            pltpu.make_async_copy(ys_ref.at[pl.ds(src, 1)], buf.at[k, pl.ds(r, 1)], sem).start()
        return carry

    lax.fori_loop(0, TM_COMB, issue, 0, unroll=8)
    for k in range(2):
        pltpu.make_async_copy(ys_ref.at[pl.ds(0, TM_COMB)], buf.at[k], sem).wait()
    w = wcol_ref[...]
    o_ref[...] = xn_ref[...] + (w[:, 0:1] * buf[0] + w[:, 1:2] * buf[1])


def _combine(pos, xn, wcol, ys):
    n = xn.shape[0]
    return pl.pallas_call(
        functools.partial(_combine_kernel, n_tokens=n),
        grid_spec=pltpu.PrefetchScalarGridSpec(
            num_scalar_prefetch=1,
            grid=(n // TM_COMB,),
            in_specs=[pl.BlockSpec((TM_COMB, D_MODEL), lambda i, p: (i, 0)),
                      pl.BlockSpec((TM_COMB, LANES), lambda i, p: (i, 0)),
                      pl.BlockSpec(memory_space=pl.ANY)],
            out_specs=pl.BlockSpec((TM_COMB, D_MODEL), lambda i, p: (i, 0)),
            scratch_shapes=[pltpu.VMEM((2, TM_COMB, D_MODEL), F32), pltpu.SemaphoreType.DMA(())],
        ),
        out_shape=jax.ShapeDtypeStruct((n, D_MODEL), F32),
        compiler_params=_cparams(("arbitrary",)),
        name="moe_combine",
    )(pos, xn, wcol, ys)


def _swiglu(h, wg_ref, wu_ref, wd_ref):
    y = None
    c0 = 0
    for width in FF_CHUNKS:
        cs = slice(c0, c0 + width)
        g = jnp.dot(h, wg_ref[:, cs], preferred_element_type=F32)
        u = jnp.dot(h, wu_ref[:, cs], preferred_element_type=F32)
        act = (g * jax.nn.sigmoid(g) * u).astype(BF16)
        part = jnp.dot(act, wd_ref[cs, :], preferred_element_type=F32)
        y = part if y is None else y + part
        c0 += width
    return y


def _ffn_dense_kernel(h_ref, x_ref, wg_ref, wu_ref, wd_ref, o_ref):
    o_ref[...] = x_ref[...] + _swiglu(h_ref[...], wg_ref, wu_ref, wd_ref)


def _ffn_dense(h2, x2, wgu, wd):
    n = h2.shape[0]
    return pl.pallas_call(
        _ffn_dense_kernel,
        grid=(n // TM_FFN,),
        in_specs=[
            pl.BlockSpec((TM_FFN, D_MODEL), lambda i: (i, 0)),
            pl.BlockSpec((TM_FFN, D_MODEL), lambda i: (i, 0)),
            pl.BlockSpec((D_MODEL, FF), lambda i: (0, 0)),
            pl.BlockSpec((D_MODEL, FF), lambda i: (0, 1)),
            pl.BlockSpec((FF, D_MODEL), lambda i: (0, 0)),
        ],
        out_specs=pl.BlockSpec((TM_FFN, D_MODEL), lambda i: (i, 0)),
        out_shape=jax.ShapeDtypeStruct((n, D_MODEL), F32),
        compiler_params=_cparams(("parallel",)),
        name="ffn_dense",
    )(h2, x2, wgu, wgu, wd)


def _tile_plan(i, cnt_ref):
    end = jnp.int32(0)
    expert = jnp.int32(0)
    for e in range(N_EXPERTS):
        end = end + _round_up_tile(cnt_ref[e])
        expert = expert + (end <= i * TM_FFN).astype(jnp.int32)
    return jnp.minimum(expert, N_EXPERTS - 1), end // TM_FFN


def _ffn_grouped_kernel(cnt_ref, xs_ref, wg_ref, wu_ref, wd_ref, o_ref):
    i = pl.program_id(0)
    _, used = _tile_plan(i, cnt_ref)

    @pl.when(i < used)
    def _():
        o_ref[...] = _swiglu(xs_ref[...].astype(BF16), wg_ref, wu_ref, wd_ref)

    @pl.when(i >= used)
    def _():
        o_ref[...] = jnp.zeros_like(o_ref)


def _ffn_grouped(cnt, xs, wgu, wd):
    rows = xs.shape[0]

    def tile(i, c):
        return jnp.minimum(i, _tile_plan(i, c)[1] - 1)

    return pl.pallas_call(
        _ffn_grouped_kernel,
        grid_spec=pltpu.PrefetchScalarGridSpec(
            num_scalar_prefetch=1,
            grid=(rows // TM_FFN,),
            in_specs=[
                pl.BlockSpec((TM_FFN, D_MODEL), lambda i, c: (tile(i, c), 0)),
                pl.BlockSpec((None, D_MODEL, FF), lambda i, c: (_tile_plan(i, c)[0], 0, 0)),
                pl.BlockSpec((None, D_MODEL, FF), lambda i, c: (_tile_plan(i, c)[0], 0, 1)),
                pl.BlockSpec((None, FF, D_MODEL), lambda i, c: (_tile_plan(i, c)[0], 0, 0)),
            ],
            out_specs=pl.BlockSpec((TM_FFN, D_MODEL), lambda i, c: (i, 0)),
        ),
        out_shape=jax.ShapeDtypeStruct((rows, D_MODEL), F32),
        compiler_params=_cparams(("arbitrary",)),
        name="ffn_grouped",
    )(cnt, xs, wgu, wgu, wd)


def _moe(xn, logits, br, fg, wgu, wd):
    n = xn.shape[0]
    imeta, wcol, cnt_lanes = _route(logits, br)
    cnt = cnt_lanes[:, 0]
    pos = _positions(cnt, imeta).reshape(2 * n)
    xs = _scatter(pos, cnt, xn, fg)
    ys = _ffn_grouped(cnt, xs, wgu, wd)
    return _combine(pos, xn, wcol, ys)


def _rope_tables(seq):
    pos = jnp.arange(seq, dtype=F32)
    inv_freq = ROPE_THETA ** (-jnp.arange(0, QK_ROPE, 2, dtype=F32) / QK_ROPE)
    ang = pos[:, None] * inv_freq[None, :]
    return jnp.cos(ang), jnp.sin(ang)


def _rotary_slot_tables(cos, sin, gain, nope_gain, scale):
    seq = cos.shape[0]
    half = QK_ROPE // 2
    g1, g2 = gain[QK_NOPE:QK_NOPE + half], gain[QK_NOPE + half:QK_DIM]
    z = lambda w: jnp.zeros((seq, w), F32)
    tail = HEAD_SLOT - QK_DIM
    a = jnp.concatenate([jnp.broadcast_to(nope_gain, (seq, QK_NOPE)), g1 * cos, g2 * cos, z(tail)], axis=1)
    b = jnp.concatenate([z(QK_NOPE), -g2 * sin, g1 * sin, z(tail)], axis=1)
    return a * scale, b * scale


def _swap_rotary_halves(w):
    half = QK_ROPE // 2
    sw = jnp.concatenate([w[..., half:], w[..., :half]], axis=-1)
    pad = [(0, 0)] * (w.ndim - 1) + [(QK_NOPE, HEAD_SLOT - QK_DIM)]
    return jnp.pad(sw, pad)


def _head_slots(w, width):
    k = w.shape[0]
    w3 = w.reshape(k, MLA_HEADS, width)
    return jnp.pad(w3, ((0, 0), (0, 0), (0, HEAD_SLOT - width))).reshape(k, QK_WIDTH)


def _pad_lanes(v, width=LANES):
    return jnp.pad(v, (0, width - v.shape[0])).reshape(1, width)


def kernel(x, mix_norm_g, w_in, gm_v_norm_g, gm_v_norm_b, gm_w_spatial, gm_b_spatial, gm_w_proj, mla_q_lat_g, mla_w_uq, mla_kv_lat_g, mla_w_ukv, mla_q_norm_g, mla_k_norm_g, mla_w_proj, w_out, ffn_norm_g, dense_w_gu, dense_w_down, moe_w_router, moe_b_router, moe_w_gu, moe_w_down):
    batch, seq, d = x.shape
    n = batch * seq
    depth = w_in.shape[0]
    cos, sin = _rope_tables(seq)
    x2 = x.reshape(n, d)
    q_scale = float(np.log2(np.e) / np.sqrt(QK_DIM))
    v_ones = jnp.tile(jnp.zeros((HEAD_SLOT,), F32).at[V_DIM].set(1.0), MLA_HEADS).reshape(1, QK_WIDTH)

    o_v = GM_WIDTH
    o_cq = 2 * GM_WIDTH
    o_ckv = o_cq + Q_LORA
    o_kr = o_ckv + KV_LORA
    o_gate = o_kr + QK_ROPE

    for l in range(depth):
        wl = w_in[l]
        w_kr = jnp.pad(wl[:, o_kr:o_gate], ((0, 0), (QK_NOPE, LANES - QK_NOPE - QK_ROPE)))
        w1 = jnp.concatenate([wl[:, :o_kr], w_kr, _swap_rotary_halves(wl[:, o_kr:o_gate]), wl[:, o_gate:]],
                             axis=1).astype(BF16)
        uv, lat, gates = _in_proj(x2, mix_norm_g[l].reshape(1, d), w1)

        wq = _head_slots(mla_w_uq[l], QK_DIM).astype(BF16)
        wq3 = mla_w_uq[l].reshape(Q_LORA, MLA_HEADS, QK_DIM)
        wqs = _swap_rotary_halves(wq3[:, :, QK_NOPE:]).reshape(Q_LORA, QK_WIDTH).astype(BF16)
        wkv3 = mla_w_ukv[l].reshape(KV_LORA, MLA_HEADS, QK_NOPE + V_DIM)
        wk = _head_slots(wkv3[:, :, :QK_NOPE].reshape(KV_LORA, MLA_HEADS * QK_NOPE), QK_NOPE).astype(BF16)
        wv = _head_slots(wkv3[:, :, QK_NOPE:].reshape(KV_LORA, V_WIDTH), V_DIM).astype(BF16)
        gq, gk = mla_q_norm_g[l], mla_k_norm_g[l]
        aq, bq = _rotary_slot_tables(cos, sin, gq, gq[:QK_NOPE], q_scale)
        ak, bk = _rotary_slot_tables(cos, sin, gk, jnp.zeros((QK_NOPE,), F32), 1.0)
        q, k, v = _qkv_prep(lat, mla_q_lat_g[l].reshape(1, Q_LORA), wq, wqs,
                            mla_kv_lat_g[l].reshape(1, KV_LORA), wk, wv, v_ones,
                            aq, bq, ak, bk, _pad_lanes(gk[:QK_NOPE]), seq)
        slots = lambda t: t.reshape(batch, seq, QK_WIDTH)
        ot = _attention(slots(q), slots(k), slots(v)).reshape(n, V_WIDTH)

        a = _gmlp(uv, gm_v_norm_g[l].reshape(1, GM_WIDTH), gm_v_norm_b[l].reshape(1, GM_WIDTH),
                  gm_w_spatial[l], gm_b_spatial[l].T)

        is_moe = l % 2 == 1
        m = l // 2
        wr = jnp.pad(moe_w_router[m], ((0, 0), (0, LANES - N_EXPERTS))) if is_moe else None
        outs = _merge(a, ot, gates, x2, gm_w_proj[l].astype(BF16), mla_w_proj[l].astype(BF16),
                      w_out[l].astype(BF16), ffn_norm_g[l].reshape(1, d), wr)
        if is_moe:
            xn, logits = outs
            x2 = _moe(xn, logits, _pad_lanes(moe_b_router[m]), ffn_norm_g[l].reshape(1, d),
                      moe_w_gu[m].astype(BF16), moe_w_down[m].astype(BF16))
        else:
            xn, h2 = outs
            x2 = _ffn_dense(h2, xn, dense_w_gu[m].astype(BF16), dense_w_down[m].astype(BF16))
    return x2.reshape(batch, seq, d)
```

```python
import functools

import jax
import jax.numpy as jnp
import numpy as np
from jax import lax
from jax.experimental import pallas as pl
from jax.experimental.pallas import tpu as pltpu

F32 = jnp.float32
BF16 = jnp.bfloat16

EPS = 1e-6
LANES = 128

D_MODEL = 1024
GM_GROUPS = 8
GM_GROUP_CH = 128
GM_WIDTH = 1024
GM_CHUNK = 128
MLA_HEADS = 16
QK_NOPE = 64
QK_ROPE = 32
QK_DIM = 96
V_DIM = 64
Q_LORA = 512
KV_LORA = 256
ROPE_THETA = 10000.0
HEAD_SLOT = LANES
QK_WIDTH = MLA_HEADS * HEAD_SLOT
V_WIDTH = MLA_HEADS * V_DIM
LAT_WIDTH = Q_LORA + KV_LORA + 2 * LANES
N_EXPERTS = 8
FF = 2816

TM_IN = 256
TM_MERGE = 512
TM_FFN = 512
FF_CHUNKS = (768, 768, 768, 512)
TM_ROUTE = 512
TM_SCAT = 256
TM_COMB = 256
ATT_T = 512
ATT_G = 2

VMEM_LIMIT = 56 * 1024 * 1024


def _cparams(sem):
    return pltpu.CompilerParams(dimension_semantics=sem, vmem_limit_bytes=VMEM_LIMIT)


def _rms(xf, g):
    return xf * lax.rsqrt(jnp.mean(xf * xf, axis=-1, keepdims=True) + EPS) * g


def _in_qkv_kernel(x_ref, g_ref, w_ref, qg_ref, wq_ref, wqs_ref, kvg_ref, wk_ref, wv_ref, vone_ref,
                   aq_ref, bq_ref, ak_ref, bk_ref, gkn_ref, uv_ref, gate_ref, q_ref, k_ref, v_ref):
    h = _rms(x_ref[...], g_ref[...]).astype(BF16)
    o_lat = 2 * GM_WIDTH
    o_gate = o_lat + LAT_WIDTH
    lat = jnp.dot(h, w_ref[:, o_lat:o_gate], preferred_element_type=F32)
    uv_ref[...] = jnp.dot(h, w_ref[:, :o_lat], preferred_element_type=F32).astype(BF16)
    gate_ref[...] = jnp.dot(h, w_ref[:, o_gate:], preferred_element_type=F32).astype(BF16)

    cq = _rms(lat[:, :Q_LORA], qg_ref[...]).astype(BF16)
    ckv = _rms(lat[:, Q_LORA:Q_LORA + KV_LORA], kvg_ref[...]).astype(BF16)
    kr = lat[:, Q_LORA + KV_LORA:Q_LORA + KV_LORA + LANES]
    kr_sw = lat[:, Q_LORA + KV_LORA + LANES:]
    kr_ssq = jnp.sum(kr * kr, axis=-1, keepdims=True)
    kr_rot = kr * ak_ref[...] + kr_sw * bk_ref[...]
    aq, bq, gkn = aq_ref[...], bq_ref[...], gkn_ref[...]
    q_all = jnp.dot(cq, wq_ref[...], preferred_element_type=F32)
    q_sw = jnp.dot(cq, wqs_ref[...], preferred_element_type=F32)
    k_all = jnp.dot(ckv, wk_ref[...], preferred_element_type=F32)
    for hd in range(MLA_HEADS):
        sl = slice(hd * HEAD_SLOT, (hd + 1) * HEAD_SLOT)
        qh = q_all[:, sl]
        rq = lax.rsqrt(jnp.sum(qh * qh, axis=-1, keepdims=True) * (1.0 / QK_DIM) + EPS)
        q_ref[:, sl] = ((qh * aq + q_sw[:, sl] * bq) * rq).astype(BF16)
        kh = k_all[:, sl]
        rk = lax.rsqrt((jnp.sum(kh * kh, axis=-1, keepdims=True) + kr_ssq) * (1.0 / QK_DIM) + EPS)
        k_ref[:, sl] = ((kh * gkn + kr_rot) * rk).astype(BF16)
    v_ref[...] = (jnp.dot(ckv, wv_ref[...], preferred_element_type=F32) + vone_ref[...]).astype(BF16)


def _in_qkv(x2, g, w1, qg, wq, wqs, kvg, wk, wv, vone, aq, bq, ak, bk, gkn, seq):
    n = x2.shape[0]
    tps = seq // TM_IN
    full = lambda shape: pl.BlockSpec(shape, lambda i: (0,) * len(shape), pipeline_mode=pl.Buffered(1))
    rope = pl.BlockSpec((TM_IN, HEAD_SLOT), lambda i: (i % tps, 0))
    row = lambda w: pl.BlockSpec((TM_IN, w), lambda i: (i, 0))
    return pl.pallas_call(
        _in_qkv_kernel,
        grid=(n // TM_IN,),
        in_specs=[
            row(D_MODEL), full((1, D_MODEL)), full((D_MODEL, w1.shape[1])),
            full((1, Q_LORA)), full((Q_LORA, QK_WIDTH)), full((Q_LORA, QK_WIDTH)),
            full((1, KV_LORA)), full((KV_LORA, QK_WIDTH)), full((KV_LORA, QK_WIDTH)), full((1, QK_WIDTH)),
            rope, rope, rope, rope, full((1, HEAD_SLOT)),
        ],
        out_specs=[row(2 * GM_WIDTH), row(2 * D_MODEL), row(QK_WIDTH), row(QK_WIDTH), row(QK_WIDTH)],
        out_shape=[jax.ShapeDtypeStruct((n, 2 * GM_WIDTH), BF16), jax.ShapeDtypeStruct((n, 2 * D_MODEL), BF16)]
        + [jax.ShapeDtypeStruct((n, QK_WIDTH), BF16)] * 3,
        compiler_params=_cparams(("parallel",)),
        name="in_qkv",
    )(x2, g, w1, qg, wq, wqs, kvg, wk, wv, vone, aq, bq, ak, bk, gkn)


def _attention_kernel(q_ref, k_ref, v_ref, o_ref, *, seq):
    nblk = seq // ATT_T
    key_pos = lax.broadcasted_iota(jnp.int32, (ATT_T, ATT_T), 0)
    qry_pos = lax.broadcasted_iota(jnp.int32, (ATT_T, ATT_T), 1)
    causal = key_pos <= qry_pos
    nt = (((1,), (1,)), ((), ()))
    tn = (((0,), (0,)), ((), ()))

    def update(state, g, st, kj, diagonal):
        m, acc = state
        if diagonal:
            st = jnp.where(causal, st, -1e30)
        m_new = jnp.maximum(m, jnp.max(st, axis=0, keepdims=True))
        alpha = jnp.exp2(m - m_new)
        p = jnp.exp2((st - m_new).astype(BF16))
        v_blk = v_ref[kj * ATT_T:(kj + 1) * ATT_T, g * HEAD_SLOT:(g + 1) * HEAD_SLOT]
        return m_new, alpha * acc + lax.dot_general(v_blk, p, tn, preferred_element_type=F32)

    for qi in range(nblk):
        q0, q1 = qi * ATT_T, (qi + 1) * ATT_T
        strips = [lax.dot_general(k_ref[0:q1, g * HEAD_SLOT:(g + 1) * HEAD_SLOT],
                                  q_ref[q0:q1, g * HEAD_SLOT:(g + 1) * HEAD_SLOT], nt,
                                  preferred_element_type=F32) for g in range(ATT_G)]
        init = (jnp.full((1, ATT_T), -1e30, F32), jnp.zeros((HEAD_SLOT, ATT_T), F32))
        states = [init for _ in range(ATT_G)]
        for kj in range(qi + 1):
            states = [update(states[g], g, strips[g][kj * ATT_T:(kj + 1) * ATT_T], kj, kj == qi)
                      for g in range(ATT_G)]
        for g in range(ATT_G):
            acc = states[g][1].T
            o_ref[q0:q1, g * V_DIM:(g + 1) * V_DIM] = (acc[:, :V_DIM] / acc[:, V_DIM:V_DIM + 1]).astype(BF16)


def _attention(q3, k3, v3):
    batch, seq, _ = q3.shape
    slot = pl.BlockSpec((None, seq, ATT_G * HEAD_SLOT), lambda b, h: (b, 0, h))
    return pl.pallas_call(
        functools.partial(_attention_kernel, seq=seq),
        grid=(batch, MLA_HEADS // ATT_G),
        in_specs=[slot, slot, slot],
        out_specs=pl.BlockSpec((None, seq, ATT_G * V_DIM), lambda b, h: (b, 0, h)),
        out_shape=jax.ShapeDtypeStruct((batch, seq, V_WIDTH), BF16),
        compiler_params=_cparams(("parallel", "parallel")),
        name="attention",
    )(q3, k3, v3)


def _split_bf16(v):
    hi = v.astype(BF16)
    return hi, (v - hi.astype(F32)).astype(BF16)


def _gmlp_tile(uv_ref, vg_ref, vb_ref, ws_ref, bst_ref, a_ref):
    v = jax.nn.gelu(uv_ref[:, GM_WIDTH:].astype(F32))
    mu = jnp.mean(v, axis=-1, keepdims=True)
    vc = v - mu
    vn = (vc * lax.rsqrt(jnp.mean(vc * vc, axis=-1, keepdims=True) + EPS) * vg_ref[...] + vb_ref[...]).astype(BF16)
    row = lax.broadcasted_iota(jnp.int32, (GM_CHUNK, GM_CHUNK), 0)
    col = lax.broadcasted_iota(jnp.int32, (GM_CHUNK, GM_CHUNK), 1)
    tril = col <= row
    bst = bst_ref[...]
    for g in range(GM_GROUPS):
        ws = jnp.where(tril, ws_ref[g], 0.0).astype(BF16)
        bias = bst[:, g:g + 1]
        cs = slice(g * GM_GROUP_CH, (g + 1) * GM_GROUP_CH)
        for c in range(TM_MERGE // GM_CHUNK):
            rs = slice(c * GM_CHUNK, (c + 1) * GM_CHUNK)
            mixed = jnp.dot(ws, vn[rs, cs], preferred_element_type=F32) + bias
            u = jax.nn.gelu(uv_ref[rs, cs].astype(F32))
            a_ref[rs, cs] = (u * mixed).astype(BF16)


def _merge_kernel(uv_ref, vg_ref, vb_ref, ws_ref, bst_ref, ot_ref, gate_ref, x_ref, wa_ref, wb_ref, wo_ref,
                  fg_ref, *rest, with_router):
    if with_router:
        wr_ref, xn_ref, lg_ref, a_ref = rest
    else:
        xn_ref, h2_ref, a_ref = rest
    yb = jnp.dot(ot_ref[...], wb_ref[...], preferred_element_type=F32)
    _gmlp_tile(uv_ref, vg_ref, vb_ref, ws_ref, bst_ref, a_ref)
    ya = jnp.dot(a_ref[...], wa_ref[...], preferred_element_type=F32)
    ga =jax.nn.sigmoid(gate_ref[:, :D_MODEL].astype(F32))
    gb = jax.nn.sigmoid(gate_ref[:, D_MODEL:].astype(F32))
    merged = (ga * ya + gb * yb).astype(BF16)
    xn = x_ref[...] + jnp.dot(merged, wo_ref[...], preferred_element_type=F32)
    xn_ref[...] = xn
    h2 = _rms(xn, fg_ref[...])
    if with_router:
        h_hi, h_lo = _split_bf16(h2)
        w_hi, w_lo = _split_bf16(wr_ref[...])
        hh_hl = jnp.dot(h_hi, jnp.concatenate([w_hi, w_lo], axis=1), preferred_element_type=F32)
        lg_ref[...] = hh_hl[:, :LANES] + (hh_hl[:, LANES:] + jnp.dot(h_lo, w_hi, preferred_element_type=F32))
    else:
        h2_ref[...] = h2.astype(BF16)


def _merge(uv, vg, vb, ws, bst, ot, gates, x2, wa, wb, wo, fg, wr):
    n = x2.shape[0]
    with_router = wr is not None
    full = lambda shape: pl.BlockSpec(shape, lambda i: (0,) * len(shape))
    row = lambda w: pl.BlockSpec((TM_MERGE, w), lambda i: (i, 0))
    in_specs = [
        row(2 * GM_WIDTH), full((1, GM_WIDTH)), full((1, GM_WIDTH)),
        full((GM_GROUPS, GM_CHUNK, GM_CHUNK)), full((GM_CHUNK, GM_GROUPS)),
        row(V_WIDTH), row(2 * D_MODEL), row(D_MODEL),
        full((GM_WIDTH, D_MODEL)), full((V_WIDTH, D_MODEL)), full((D_MODEL, D_MODEL)), full((1, D_MODEL)),
    ]
    args = [uv, vg, vb, ws, bst, ot, gates, x2, wa, wb, wo, fg]
    if with_router:
        in_specs.append(full((D_MODEL, LANES)))
        out_specs = [row(D_MODEL), row(LANES)]
        out_shape = [jax.ShapeDtypeStruct((n, D_MODEL), F32), jax.ShapeDtypeStruct((n, LANES), F32)]
        args.append(wr)
    else:
        out_specs = [row(D_MODEL), row(D_MODEL)]
        out_shape = [jax.ShapeDtypeStruct((n, D_MODEL), F32), jax.ShapeDtypeStruct((n, D_MODEL), BF16)]
    return pl.pallas_call(
        functools.partial(_merge_kernel, with_router=with_router),
        grid=(n // TM_MERGE,),
        in_specs=in_specs, out_specs=out_specs, out_shape=out_shape,
        scratch_shapes=[pltpu.VMEM((TM_MERGE, GM_WIDTH), BF16)],
        compiler_params=_cparams(("parallel",)),
        name="merge_router" if with_router else "merge",
    )(*args)


def _round_up_tile(c):
    return (c + (TM_FFN - 1)) & (-TM_FFN)


def _route_kernel(lg_ref, br_ref, imeta_ref, wcol_ref, cnt_ref, run_ref):
    @pl.when(pl.program_id(0) == 0)
    def _():
        run_ref[...] = jnp.zeros_like(run_ref)

    lt = (lg_ref[...] + br_ref[...]).T[:N_EXPERTS, :]
    sub = lax.broadcasted_iota(jnp.int32, lt.shape, 0)
    m1 = jnp.max(lt, axis=0, keepdims=True)
    i1 = jnp.min(jnp.where(lt == m1, sub, N_EXPERTS), axis=0, keepdims=True)
    rest = jnp.where(sub == i1, -jnp.inf, lt)
    m2 = jnp.max(rest, axis=0, keepdims=True)
    i2 = jnp.min(jnp.where(rest == m2, sub, N_EXPERTS), axis=0, keepdims=True)
    e2 = jnp.exp(m2 - m1)
    w1 = 1.0 / (1.0 + e2)
    w2 = e2 / (1.0 + e2)

    onehot = jnp.where((sub == i1) | (sub == i2), 1.0, 0.0)
    src = lax.broadcasted_iota(jnp.int32, (TM_ROUTE, TM_ROUTE), 0)
    dst = lax.broadcasted_iota(jnp.int32, (TM_ROUTE, TM_ROUTE), 1)
    earlier = jnp.where(src < dst, 1.0, 0.0).astype(BF16)
    seen = jnp.dot(onehot.astype(BF16), earlier, preferred_element_type=F32) + run_ref[:, :1]
    r1 = jnp.sum(jnp.where(sub == i1, seen, 0.0), axis=0, keepdims=True).astype(jnp.int32)
    r2 = jnp.sum(jnp.where(sub == i2, seen, 0.0), axis=0, keepdims=True).astype(jnp.int32)
    imeta_ref[...] = jnp.where(sub == 0, i1, jnp.where(sub == 1, i2, jnp.where(sub == 2, r1, jnp.where(sub == 3, r2, 0))))

    total = run_ref[...] + jnp.sum(onehot, axis=1, keepdims=True)
    run_ref[...] = total
    cnt_ref[...] = total.astype(jnp.int32)

    sub_w = lax.broadcasted_iota(jnp.int32, (LANES, TM_ROUTE), 0)
    wcol_ref[...] = jnp.where(sub_w == 0, w1, jnp.where(sub_w == 1, w2, 0.0)).T


def _route(logits, br):
    n = logits.shape[0]
    return pl.pallas_call(
        _route_kernel,
        grid=(n // TM_ROUTE,),
        in_specs=[pl.BlockSpec((TM_ROUTE, LANES), lambda i: (i, 0)),
                  pl.BlockSpec((1, LANES), lambda i: (0, 0))],
        out_specs=[pl.BlockSpec((N_EXPERTS, TM_ROUTE), lambda i: (0, i)),
                   pl.BlockSpec((TM_ROUTE, LANES), lambda i: (i, 0)),
                   pl.BlockSpec((N_EXPERTS, LANES), lambda i: (0, 0))],
        out_shape=[jax.ShapeDtypeStruct((N_EXPERTS, n), jnp.int32),
                   jax.ShapeDtypeStruct((n, LANES), F32),
                   jax.ShapeDtypeStruct((N_EXPERTS, LANES), jnp.int32)],
        scratch_shapes=[pltpu.VMEM((N_EXPERTS, LANES), F32)],
        compiler_params=_cparams(("arbitrary",)),
        name="route",
    )(logits, br)


def _positions_kernel(cnt_ref, imeta_ref, pos_ref):
    im = imeta_ref[...]
    i1, i2, r1, r2 = im[0:1], im[1:2], im[2:3], im[3:4]
    start = jnp.int32(0)
    p1 = jnp.zeros_like(i1)
    p2 = jnp.zeros_like(i2)
    for e in range(N_EXPERTS):
        p1 = jnp.where(i1 == e, start, p1)
        p2 = jnp.where(i2 == e, start, p2)
        start = start + _round_up_tile(cnt_ref[e])
    pos_ref[...] = jnp.concatenate([p1 + r1, p2 + r2], axis=0)


def _positions(cnt, imeta):
    n = imeta.shape[1]
    return pl.pallas_call(
        _positions_kernel,
        grid_spec=pltpu.PrefetchScalarGridSpec(
            num_scalar_prefetch=1,
            grid=(n // TM_ROUTE,),
            in_specs=[pl.BlockSpec((N_EXPERTS, TM_ROUTE), lambda i, c: (0, i))],
            out_specs=pl.BlockSpec((2, TM_ROUTE), lambda i, c: (0, i)),
        ),
        out_shape=jax.ShapeDtypeStruct((2, n), jnp.int32),
        compiler_params=_cparams(("parallel",)),
        name="positions",
    )(cnt, imeta)


def _scatter_kernel(pos_ref, cnt_ref, xn_ref, fg_ref, xs_ref, hbuf, zbuf, sem, zsem, *, n_tokens):
    n_tiles = xs_ref.shape[0] // TM_FFN

    @pl.when(pl.program_id(0) == 0)
    def _():
        zbuf[...] = jnp.zeros_like(zbuf)

        def zero_tile(t):
            rows = pl.ds(pl.multiple_of(t * TM_FFN, TM_FFN), TM_FFN)
            return pltpu.make_async_copy(zbuf, xs_ref.at[rows], zsem)

        def each_zero_tile(action):
            end = jnp.int32(0)
            for e in range(N_EXPERTS):
                tiles = _round_up_tile(cnt_ref[e]) // TM_FFN
                end = end + tiles

                @pl.when(tiles > 0)
                def _(end=end):
                    action(zero_tile(end - 1))

            def tail(t, carry):
                action(zero_tile(t))
                return carry

            lax.fori_loop(end, n_tiles, tail, 0)

        each_zero_tile(lambda copy: copy.start())
        each_zero_tile(lambda copy: copy.wait())

    base = pl.program_id(0) * TM_SCAT
    hbuf[...] = _rms(xn_ref[...], fg_ref[...])

    def issue(r, carry):
        for k in range(2):
            dst = pos_ref[k * n_tokens + base + r]
            pltpu.make_async_copy(hbuf.at[pl.ds(r, 1)], xs_ref.at[pl.ds(dst, 1)], sem).start()
        return carry

    lax.fori_loop(0, TM_SCAT, issue, 0, unroll=8)
    for k in range(2):
        pltpu.make_async_copy(hbuf, xs_ref.at[pl.ds(0, TM_SCAT)], sem).wait()


def _scatter(pos, cnt, xn, fg):
    n = xn.shape[0]
    rows = 2 * n + N_EXPERTS * TM_FFN
    return pl.pallas_call(
        functools.partial(_scatter_kernel, n_tokens=n),
        grid_spec=pltpu.PrefetchScalarGridSpec(
            num_scalar_prefetch=2,
            grid=(n // TM_SCAT,),
            in_specs=[pl.BlockSpec((TM_SCAT, D_MODEL), lambda i, p, c: (i, 0)),
                      pl.BlockSpec((1, D_MODEL), lambda i, p, c: (0, 0))],
            out_specs=pl.BlockSpec(memory_space=pl.ANY),
            scratch_shapes=[pltpu.VMEM((TM_SCAT, D_MODEL), F32), pltpu.VMEM((TM_FFN, D_MODEL), F32),
                            pltpu.SemaphoreType.DMA(()), pltpu.SemaphoreType.DMA(())],
        ),
        out_shape=jax.ShapeDtypeStruct((rows, D_MODEL), F32),
        compiler_params=_cparams(("arbitrary",)),
        name="moe_scatter",
    )(pos, cnt, xn, fg)


def _combine_kernel(pos_ref, xn_ref, wcol_ref, ys_ref, o_ref, buf, sem, *, n_tokens):
    base = pl.program_id(0) * TM_COMB

    def issue(r, carry):
        for k in range(2):
            src = pos_ref[k * n_tokens + base + r]
            pltpu.make_async_copy(ys_ref.at[pl.ds(src, 1)], buf.at[k, pl.ds(r, 1)], sem).start()
        return carry

    lax.fori_loop(0, TM_COMB, issue, 0, unroll=8)
    for k in range(2):
        pltpu.make_async_copy(ys_ref.at[pl.ds(0, TM_COMB)], buf.at[k], sem).wait()
    w = wcol_ref[...]
    o_ref[...] = xn_ref[...] + (w[:, 0:1] * buf[0] + w[:, 1:2] * buf[1])


def _combine(pos, xn, wcol, ys):
    n = xn.shape[0]
    return pl.pallas_call(
        functools.partial(_combine_kernel, n_tokens=n),
        grid_spec=pltpu.PrefetchScalarGridSpec(
            num_scalar_prefetch=1,
            grid=(n // TM_COMB,),
            in_specs=[pl.BlockSpec((TM_COMB, D_MODEL), lambda i, p: (i, 0)),
                      pl.BlockSpec((TM_COMB, LANES), lambda i, p: (i, 0)),
                      pl.BlockSpec(memory_space=pl.ANY)],
            out_specs=pl.BlockSpec((TM_COMB, D_MODEL), lambda i, p: (i, 0)),
            scratch_shapes=[pltpu.VMEM((2, TM_COMB, D_MODEL), F32), pltpu.SemaphoreType.DMA(())],
        ),
        out_shape=jax.ShapeDtypeStruct((n, D_MODEL), F32),
        compiler_params=_cparams(("arbitrary",)),
        name="moe_combine",
    )(pos, xn, wcol, ys)


def _swiglu(h, wg_ref, wu_ref, wd_ref):
    y = None
    c0 = 0
    for width in FF_CHUNKS:
        cs = slice(c0, c0 + width)
        g = jnp.dot(h, wg_ref[:, cs], preferred_element_type=F32)
        u = jnp.dot(h, wu_ref[:, cs], preferred_element_type=F32)
        act = (g * jax.nn.sigmoid(g) * u).astype(BF16)
        part = jnp.dot(act, wd_ref[cs, :], preferred_element_type=F32)
        y = part if y is None else y + part
        c0 += width
    return y


def _ffn_dense_kernel(h_ref, x_ref, wg_ref, wu_ref, wd_ref, o_ref):
    o_ref[...] = x_ref[...] + _swiglu(h_ref[...], wg_ref, wu_ref, wd_ref)


def _ffn_dense(h2, x2, wgu, wd):
    n = h2.shape[0]
    return pl.pallas_call(
        _ffn_dense_kernel,
        grid=(n // TM_FFN,),
        in_specs=[
            pl.BlockSpec((TM_FFN, D_MODEL), lambda i: (i, 0)),
            pl.BlockSpec((TM_FFN, D_MODEL), lambda i: (i, 0)),
            pl.BlockSpec((D_MODEL, FF), lambda i: (0, 0)),
            pl.BlockSpec((D_MODEL, FF), lambda i: (0, 1)),
            pl.BlockSpec((FF, D_MODEL), lambda i: (0, 0)),
        ],
        out_specs=pl.BlockSpec((TM_FFN, D_MODEL), lambda i: (i, 0)),
        out_shape=jax.ShapeDtypeStruct((n, D_MODEL), F32),
        compiler_params=_cparams(("parallel",)),
        name="ffn_dense",
    )(h2, x2, wgu, wgu, wd)


def _tile_plan(i, cnt_ref):
    end = jnp.int32(0)
    expert = jnp.int32(0)
    for e in range(N_EXPERTS):
        end = end + _round_up_tile(cnt_ref[e])
        expert = expert + (end <= i * TM_FFN).astype(jnp.int32)
    return jnp.minimum(expert, N_EXPERTS - 1), end // TM_FFN


def _ffn_grouped_kernel(cnt_ref, xs_ref, wg_ref, wu_ref, wd_ref, o_ref):
    i = pl.program_id(0)
    _, used = _tile_plan(i, cnt_ref)

    @pl.when(i < used)
    def _():
        o_ref[...] = _swiglu(xs_ref[...].astype(BF16), wg_ref, wu_ref, wd_ref)

    @pl.when(i >= used)
    def _():
        o_ref[...] = jnp.zeros_like(o_ref)


def _ffn_grouped(cnt, xs, wgu, wd):
    rows = xs.shape[0]

    def tile(i, c):
        return jnp.minimum(i, _tile_plan(i, c)[1] - 1)

    return pl.pallas_call(
        _ffn_grouped_kernel,
        grid_spec=pltpu.PrefetchScalarGridSpec(
            num_scalar_prefetch=1,
            grid=(rows // TM_FFN,),
            in_specs=[
                pl.BlockSpec((TM_FFN, D_MODEL), lambda i, c: (tile(i, c), 0)),
                pl.BlockSpec((None, D_MODEL, FF), lambda i, c: (_tile_plan(i, c)[0], 0, 0)),
                pl.BlockSpec((None, D_MODEL, FF), lambda i, c: (_tile_plan(i, c)[0], 0, 1)),
                pl.BlockSpec((None, FF, D_MODEL), lambda i, c: (_tile_plan(i, c)[0], 0, 0)),
            ],
            out_specs=pl.BlockSpec((TM_FFN, D_MODEL), lambda i, c: (i, 0)),
        ),
        out_shape=jax.ShapeDtypeStruct((rows, D_MODEL), F32),
        compiler_params=_cparams(("arbitrary",)),
        name="ffn_grouped",
    )(cnt, xs, wgu, wgu, wd)


def _moe(xn, logits, br, fg, wgu, wd):
    n = xn.shape[0]
    imeta, wcol, cnt_lanes = _route(logits, br)
    cnt = cnt_lanes[:, 0]
    pos = _positions(cnt, imeta).reshape(2 * n)
    xs = _scatter(pos, cnt, xn, fg)
    ys = _ffn_grouped(cnt, xs, wgu, wd)
    return _combine(pos, xn, wcol, ys)


def _rope_tables(seq):
    pos = jnp.arange(seq, dtype=F32)
    inv_freq = ROPE_THETA ** (-jnp.arange(0, QK_ROPE, 2, dtype=F32) / QK_ROPE)
    ang = pos[:, None] * inv_freq[None, :]
    return jnp.cos(ang), jnp.sin(ang)


def _rotary_slot_tables(cos, sin, gain, nope_gain, scale):
    seq = cos.shape[0]
    half = QK_ROPE // 2
    g1, g2 = gain[QK_NOPE:QK_NOPE + half], gain[QK_NOPE + half:QK_DIM]
    z = lambda w: jnp.zeros((seq, w), F32)
    tail = HEAD_SLOT - QK_DIM
    a = jnp.concatenate([jnp.broadcast_to(nope_gain, (seq, QK_NOPE)), g1 * cos, g2 * cos, z(tail)], axis=1)
    b = jnp.concatenate([z(QK_NOPE), -g2 * sin, g1 * sin, z(tail)], axis=1)
    return a * scale, b * scale


def _swap_rotary_halves(w):
    half = QK_ROPE // 2
    sw = jnp.concatenate([w[..., half:], w[..., :half]], axis=-1)
    pad = [(0, 0)] * (w.ndim - 1) + [(QK_NOPE, HEAD_SLOT - QK_DIM)]
    return jnp.pad(sw, pad)


def _head_slots(w, width):
    k = w.shape[0]
    w3 = w.reshape(k, MLA_HEADS, width)
    return jnp.pad(w3, ((0, 0), (0, 0), (0, HEAD_SLOT - width))).reshape(k, QK_WIDTH)


def _pad_lanes(v, width=LANES):
    return jnp.pad(v, (0, width - v.shape[0])).reshape(1, width)


def kernel(x, mix_norm_g, w_in, gm_v_norm_g, gm_v_norm_b, gm_w_spatial, gm_b_spatial, gm_w_proj, mla_q_lat_g, mla_w_uq, mla_kv_lat_g, mla_w_ukv, mla_q_norm_g, mla_k_norm_g, mla_w_proj, w_out, ffn_norm_g, dense_w_gu, dense_w_down, moe_w_router, moe_b_router, moe_w_gu, moe_w_down):
    batch, seq, d = x.shape
    n = batch * seq
    depth = w_in.shape[0]
    cos, sin = _rope_tables(seq)
    x2 = x.reshape(n, d)
    q_scale = float(np.log2(np.e) / np.sqrt(QK_DIM))
    v_ones = jnp.tile(jnp.zeros((HEAD_SLOT,), F32).at[V_DIM].set(1.0), MLA_HEADS).reshape(1, QK_WIDTH)

    o_v = GM_WIDTH
    o_cq = 2 * GM_WIDTH
    o_ckv = o_cq + Q_LORA
    o_kr = o_ckv + KV_LORA
    o_gate = o_kr + QK_ROPE

    for l in range(depth):
        wl = w_in[l]
        w_kr = jnp.pad(wl[:, o_kr:o_gate], ((0, 0), (QK_NOPE, LANES - QK_NOPE - QK_ROPE)))
        w1 = jnp.concatenate([wl[:, :o_kr], w_kr, _swap_rotary_halves(wl[:, o_kr:o_gate]), wl[:, o_gate:]],
                             axis=1).astype(BF16)
        wq = _head_slots(mla_w_uq[l], QK_DIM).astype(BF16)
        wq3 = mla_w_uq[l].reshape(Q_LORA, MLA_HEADS, QK_DIM)
        wqs = _swap_rotary_halves(wq3[:, :, QK_NOPE:]).reshape(Q_LORA, QK_WIDTH).astype(BF16)
        wkv3 = mla_w_ukv[l].reshape(KV_LORA, MLA_HEADS, QK_NOPE + V_DIM)
        wk = _head_slots(wkv3[:, :, :QK_NOPE].reshape(KV_LORA, MLA_HEADS * QK_NOPE), QK_NOPE).astype(BF16)
        wv = _head_slots(wkv3[:, :, QK_NOPE:].reshape(KV_LORA, V_WIDTH), V_DIM).astype(BF16)
        gq, gk = mla_q_norm_g[l], mla_k_norm_g[l]
        aq, bq = _rotary_slot_tables(cos, sin, gq, gq[:QK_NOPE], q_scale)
        ak, bk = _rotary_slot_tables(cos, sin, gk, jnp.zeros((QK_NOPE,), F32), 1.0)
        uv, gates, q, k, v = _in_qkv(x2, mix_norm_g[l].reshape(1, d), w1,
                                     mla_q_lat_g[l].reshape(1, Q_LORA), wq, wqs,
                                     mla_kv_lat_g[l].reshape(1, KV_LORA), wk, wv, v_ones,
                                     aq, bq, ak, bk, _pad_lanes(gk[:QK_NOPE]), seq)
        slots = lambda t: t.reshape(batch, seq, QK_WIDTH)
        ot = _attention(slots(q), slots(k), slots(v)).reshape(n, V_WIDTH)

        is_moe = l % 2 == 1
        m = l // 2
        wr = jnp.pad(moe_w_router[m], ((0, 0), (0, LANES - N_EXPERTS))) if is_moe else None
        outs = _merge(uv, gm_v_norm_g[l].reshape(1, GM_WIDTH), gm_v_norm_b[l].reshape(1, GM_WIDTH),
                      gm_w_spatial[l], gm_b_spatial[l].T, ot, gates, x2,
                      gm_w_proj[l].astype(BF16), mla_w_proj[l].astype(BF16),
                      w_out[l].astype(BF16), ffn_norm_g[l].reshape(1, d), wr)
        if is_moe:
            xn, logits = outs
            x2 = _moe(xn, logits, _pad_lanes(moe_b_router[m]), ffn_norm_g[l].reshape(1, d),
                      moe_w_gu[m].astype(BF16), moe_w_down[m].astype(BF16))
        else:
            xn, h2 = outs
            x2 = _ffn_dense(h2, xn, dense_w_gu[m].astype(BF16), dense_w_down[m].astype(BF16))
    return x2.reshape(batch, seq, d)
```

```python
import functools

import jax
import jax.numpy as jnp
import numpy as np
from jax import lax
from jax.experimental import pallas as pl
from jax.experimental.pallas import tpu as pltpu

F32 = jnp.float32
BF16 = jnp.bfloat16

EPS = 1e-6
LANES = 128

D_MODEL = 1024
GM_GROUPS = 8
GM_GROUP_CH = 128
GM_WIDTH = 1024
GM_CHUNK = 128
MLA_HEADS = 16
QK_NOPE = 64
QK_ROPE = 32
QK_DIM = 96
V_DIM = 64
Q_LORA = 512
KV_LORA = 256
ROPE_THETA = 10000.0
HEAD_SLOT = LANES
QK_WIDTH = MLA_HEADS * HEAD_SLOT
V_WIDTH = MLA_HEADS * V_DIM
N_EXPERTS = 8
FF = 2816

TM_IN = 256
TM_MERGE = 512
TM_FFN = 512
FF_CHUNKS = (768, 768, 768, 512)
TM_ROUTE = 512
TM_SCAT = 256
TM_COMB = 256
ATT_T = 512
ATT_G = 2

VMEM_LIMIT = 56 * 1024 * 1024


def _cparams(sem):
    return pltpu.CompilerParams(dimension_semantics=sem, vmem_limit_bytes=VMEM_LIMIT)


def _rms(xf, g):
    return xf * lax.rsqrt(jnp.mean(xf * xf, axis=-1, keepdims=True) + EPS) * g


def _in_qkv_kernel(x_ref, g_ref, w_ref, wkr_ref, wgate_ref, qg_ref, wq_ref, kvg_ref, wk_ref, wv_ref, vone_ref,
                   tq_ref, ta_ref, tb_ref, gq_ref, ga_ref, gb_ref, gkn_ref,
                   uv_ref, gate_ref, q_ref, k_ref, v_ref):
    h = _rms(x_ref[...], g_ref[...]).astype(BF16)
    o_lat = 2 * GM_WIDTH
    lat = jnp.dot(h, w_ref[:, o_lat:], preferred_element_type=F32)
    krx = jnp.dot(h, wkr_ref[...], preferred_element_type=F32)
    uv_ref[...] = jnp.dot(h, w_ref[:, :o_lat], preferred_element_type=F32).astype(BF16)
    gate_ref[...] = jnp.dot(h, wgate_ref[...], preferred_element_type=F32).astype(BF16)

    cq = _rms(lat[:, :Q_LORA], qg_ref[...]).astype(BF16)
    ckv = _rms(lat[:, Q_LORA:], kvg_ref[...]).astype(BF16)
    kr = krx[:, :LANES]
    kr_sw = krx[:, LANES:]
    kr_ssq = 0.5 * jnp.sum(kr * kr, axis=-1, keepdims=True)
    kr_rot = kr * (ta_ref[...] * ga_ref[...]) + kr_sw * (tb_ref[...] * gb_ref[...])
    tq, gkn = tq_ref[...] * gq_ref[...], gkn_ref[...]
    q_all = jnp.dot(cq, wq_ref[...], preferred_element_type=F32)
    k_all = jnp.dot(ckv, wk_ref[...], preferred_element_type=F32)
    first_copy = lax.broadcasted_iota(jnp.int32, (1, HEAD_SLOT), 1) < QK_DIM
    for hd in range(MLA_HEADS):
        sl = slice(hd * HEAD_SLOT, (hd + 1) * HEAD_SLOT)
        qh = q_all[:, sl]
        ssq = jnp.sum(qh * jnp.where(first_copy, qh, 0.0), axis=-1, keepdims=True)
        q_ref[:, sl] = (qh * tq * lax.rsqrt(ssq * (1.0 / QK_DIM) + EPS)).astype(BF16)
        kh = k_all[:, sl]
        rk = lax.rsqrt((jnp.sum(kh * kh, axis=-1, keepdims=True) + kr_ssq) * (1.0 / QK_DIM) + EPS)
        k_ref[:, sl] = ((kh * gkn + kr_rot) * rk).astype(BF16)
    v_ref[...] = (jnp.dot(ckv, wv_ref[...], preferred_element_type=F32) + vone_ref[...]).astype(BF16)


def _in_qkv(x2, g, w_main, w_kr, w_gate, qg, wq, kvg, wk, wv, vone, tq, ta, tb, gq, ga, gb, gkn, seq):
    n = x2.shape[0]
    tps = seq // TM_IN
    full = lambda shape: pl.BlockSpec(shape, lambda i: (0,) * len(shape), pipeline_mode=pl.Buffered(1))
    rope = pl.BlockSpec((TM_IN, HEAD_SLOT), lambda i: (i % tps, 0))
    row = lambda w: pl.BlockSpec((TM_IN, w), lambda i: (i, 0))
    lane_row = full((1, HEAD_SLOT))
    return pl.pallas_call(
        _in_qkv_kernel,
        grid=(n // TM_IN,),
        in_specs=[
            row(D_MODEL), full((1, D_MODEL)), full(w_main.shape), full(w_kr.shape), full(w_gate.shape),
            full((1, Q_LORA)), full((Q_LORA, QK_WIDTH)),
            full((1, KV_LORA)), full((KV_LORA, QK_WIDTH)), full((KV_LORA, QK_WIDTH)), full((1, QK_WIDTH)),
            rope, rope, rope, lane_row, lane_row, lane_row, lane_row,
        ],
        out_specs=[row(2 * GM_WIDTH), row(2 * D_MODEL), row(QK_WIDTH), row(QK_WIDTH), row(QK_WIDTH)],
        out_shape=[jax.ShapeDtypeStruct((n, 2 * GM_WIDTH), BF16), jax.ShapeDtypeStruct((n, 2 * D_MODEL), BF16)]
        + [jax.ShapeDtypeStruct((n, QK_WIDTH), BF16)] * 3,
        compiler_params=_cparams(("parallel",)),
        name="in_qkv",
    )(x2, g, w_main, w_kr, w_gate, qg, wq, kvg, wk, wv, vone, tq, ta, tb, gq, ga, gb, gkn)


def _attention_kernel(q_ref, k_ref, v_ref, o_ref, *, seq):
    nblk = seq // ATT_T
    key_pos = lax.broadcasted_iota(jnp.int32, (ATT_T, ATT_T), 0)
    qry_pos = lax.broadcasted_iota(jnp.int32, (ATT_T, ATT_T), 1)
    causal = key_pos <= qry_pos
    nt = (((1,), (1,)), ((), ()))
    tn = (((0,), (0,)), ((), ()))

    def update(state, g, st, kj, diagonal):
        m, acc = state
        if diagonal:
            st = jnp.where(causal, st, -1e30)
        m_new = jnp.maximum(m, jnp.max(st, axis=0, keepdims=True))
        alpha = jnp.exp2(m - m_new)
        p = jnp.exp2((st - m_new).astype(BF16))
        v_blk = v_ref[kj * ATT_T:(kj + 1) * ATT_T, g * HEAD_SLOT:(g + 1) * HEAD_SLOT]
        return m_new, alpha * acc + lax.dot_general(v_blk, p, tn, preferred_element_type=F32)

    for qi in range(nblk):
        q0, q1 = qi * ATT_T, (qi + 1) * ATT_T
        strips = [lax.dot_general(k_ref[0:q1, g * HEAD_SLOT:(g + 1) * HEAD_SLOT],
                                  q_ref[q0:q1, g * HEAD_SLOT:(g + 1) * HEAD_SLOT], nt,
                                  preferred_element_type=F32) for g in range(ATT_G)]
        init = (jnp.full((1, ATT_T), -1e30, F32), jnp.zeros((HEAD_SLOT, ATT_T), F32))
        states = [init for _ in range(ATT_G)]
        for kj in range(qi + 1):
            states = [update(states[g], g, strips[g][kj * ATT_T:(kj + 1) * ATT_T], kj, kj == qi)
                      for g in range(ATT_G)]
        for g in range(ATT_G):
            acc = states[g][1].T
            o_ref[q0:q1, g * V_DIM:(g + 1) * V_DIM] = (acc[:, :V_DIM] / acc[:, V_DIM:V_DIM + 1]).astype(BF16)


def _attention(q3, k3, v3):
    batch, seq, _ = q3.shape
    slot = pl.BlockSpec((None, seq, ATT_G * HEAD_SLOT), lambda b, h: (b, 0, h))
    return pl.pallas_call(
        functools.partial(_attention_kernel, seq=seq),
        grid=(batch, MLA_HEADS // ATT_G),
        in_specs=[slot, slot, slot],
        out_specs=pl.BlockSpec((None, seq, ATT_G * V_DIM), lambda b, h: (b, 0, h)),
        out_shape=jax.ShapeDtypeStruct((batch, seq, V_WIDTH), BF16),
        compiler_params=_cparams(("parallel", "parallel")),
        name="attention",
    )(q3, k3, v3)


def _split_bf16(v):
    hi = v.astype(BF16)
    return hi, (v - hi.astype(F32)).astype(BF16)


def _gmlp_tile(uv_ref, vg_ref, vb_ref, ws_ref, bst_ref, a_ref):
    v = jax.nn.gelu(uv_ref[:, GM_WIDTH:].astype(F32))
    mu = jnp.mean(v, axis=-1, keepdims=True)
    vc = v - mu
    vn = (vc * lax.rsqrt(jnp.mean(vc * vc, axis=-1, keepdims=True) + EPS) * vg_ref[...] + vb_ref[...]).astype(BF16)
    row = lax.broadcasted_iota(jnp.int32, (GM_CHUNK, GM_CHUNK), 0)
    col = lax.broadcasted_iota(jnp.int32, (GM_CHUNK, GM_CHUNK), 1)
    tril = col <= row
    bst = bst_ref[...]
    for g in range(GM_GROUPS):
        ws = jnp.where(tril, ws_ref[g], 0.0).astype(BF16)
        bias = bst[:, g:g + 1]
        cs = slice(g * GM_GROUP_CH, (g + 1) * GM_GROUP_CH)
        for c in range(TM_MERGE // GM_CHUNK):
            rs = slice(c * GM_CHUNK, (c + 1) * GM_CHUNK)
            mixed = jnp.dot(ws, vn[rs, cs], preferred_element_type=F32) + bias
            u = jax.nn.gelu(uv_ref[rs, cs].astype(F32))
            a_ref[rs, cs] = (u * mixed).astype(BF16)


def _merge_kernel(uv_ref, vg_ref, vb_ref, ws_ref, bst_ref, ot_ref, gate_ref, x_ref, wa_ref, wb_ref, wo_ref,
                  fg_ref, *rest, with_router):
    if with_router:
        wr_ref, xn_ref, lg_ref, a_ref = rest
    else:
        xn_ref, h2_ref, a_ref = rest
    yb = jnp.dot(ot_ref[...], wb_ref[...], preferred_element_type=F32)
    _gmlp_tile(uv_ref, vg_ref, vb_ref, ws_ref, bst_ref, a_ref)
    ya = jnp.dot(a_ref[...], wa_ref[...], preferred_element_type=F32)
    ga =jax.nn.sigmoid(gate_ref[:, :D_MODEL].astype(F32))
    gb = jax.nn.sigmoid(gate_ref[:, D_MODEL:].astype(F32))
    merged = (ga * ya + gb * yb).astype(BF16)
    xn = x_ref[...] + jnp.dot(merged, wo_ref[...], preferred_element_type=F32)
    xn_ref[...] = xn
    h2 = _rms(xn, fg_ref[...])
    if with_router:
        h_hi, h_lo = _split_bf16(h2)
        w_hi, w_lo = _split_bf16(wr_ref[...])
        hh_hl = jnp.dot(h_hi, jnp.concatenate([w_hi, w_lo], axis=1), preferred_element_type=F32)
        lg_ref[...] = hh_hl[:, :LANES] + (hh_hl[:, LANES:] + jnp.dot(h_lo, w_hi, preferred_element_type=F32))
    else:
        h2_ref[...] = h2.astype(BF16)


def _merge(uv, vg, vb, ws, bst, ot, gates, x2, wa, wb, wo, fg, wr):
    n = x2.shape[0]
    with_router = wr is not None
    full = lambda shape: pl.BlockSpec(shape, lambda i: (0,) * len(shape))
    row = lambda w: pl.BlockSpec((TM_MERGE, w), lambda i: (i, 0))
    in_specs = [
        row(2 * GM_WIDTH), full((1, GM_WIDTH)), full((1, GM_WIDTH)),
        full((GM_GROUPS, GM_CHUNK, GM_CHUNK)), full((GM_CHUNK, GM_GROUPS)),
        row(V_WIDTH), row(2 * D_MODEL), row(D_MODEL),
        full((GM_WIDTH, D_MODEL)), full((V_WIDTH, D_MODEL)), full((D_MODEL, D_MODEL)), full((1, D_MODEL)),
    ]
    args = [uv, vg, vb, ws, bst, ot, gates, x2, wa, wb, wo, fg]
    if with_router:
        in_specs.append(full((D_MODEL, LANES)))
        out_specs = [row(D_MODEL), row(LANES)]
        out_shape = [jax.ShapeDtypeStruct((n, D_MODEL), F32), jax.ShapeDtypeStruct((n, LANES), F32)]
        args.append(wr)
    else:
        out_specs = [row(D_MODEL), row(D_MODEL)]
        out_shape = [jax.ShapeDtypeStruct((n, D_MODEL), F32), jax.ShapeDtypeStruct((n, D_MODEL), BF16)]
    return pl.pallas_call(
        functools.partial(_merge_kernel, with_router=with_router),
        grid=(n // TM_MERGE,),
        in_specs=in_specs, out_specs=out_specs, out_shape=out_shape,
        scratch_shapes=[pltpu.VMEM((TM_MERGE, GM_WIDTH), BF16)],
        compiler_params=_cparams(("parallel",)),
        name="merge_router" if with_router else "merge",
    )(*args)


def _round_up_tile(c):
    return (c + (TM_FFN - 1)) & (-TM_FFN)


def _route_kernel(lg_ref, br_ref, imeta_ref, wcol_ref, cnt_ref, run_ref):
    @pl.when(pl.program_id(0) == 0)
    def _():
        run_ref[...] = jnp.zeros_like(run_ref)

    lt = (lg_ref[...] + br_ref[...]).T[:N_EXPERTS, :]
    sub = lax.broadcasted_iota(jnp.int32, lt.shape, 0)
    m1 = jnp.max(lt, axis=0, keepdims=True)
    i1 = jnp.min(jnp.where(lt == m1, sub, N_EXPERTS), axis=0, keepdims=True)
    rest = jnp.where(sub == i1, -jnp.inf, lt)
    m2 = jnp.max(rest, axis=0, keepdims=True)
    i2 = jnp.min(jnp.where(rest == m2, sub, N_EXPERTS), axis=0, keepdims=True)
    e2 = jnp.exp(m2 - m1)
    w1 = 1.0 / (1.0 + e2)
    w2 = e2 / (1.0 + e2)

    onehot = jnp.where((sub == i1) | (sub == i2), 1.0, 0.0)
    src = lax.broadcasted_iota(jnp.int32, (TM_ROUTE, TM_ROUTE), 0)
    dst = lax.broadcasted_iota(jnp.int32, (TM_ROUTE, TM_ROUTE), 1)
    earlier = jnp.where(src < dst, 1.0, 0.0).astype(BF16)
    seen = jnp.dot(onehot.astype(BF16), earlier, preferred_element_type=F32) + run_ref[:, :1]
    r1 = jnp.sum(jnp.where(sub == i1, seen, 0.0), axis=0, keepdims=True).astype(jnp.int32)
    r2 = jnp.sum(jnp.where(sub == i2, seen, 0.0), axis=0, keepdims=True).astype(jnp.int32)
    imeta_ref[...] = jnp.where(sub == 0, i1, jnp.where(sub == 1, i2, jnp.where(sub == 2, r1, jnp.where(sub == 3, r2, 0))))

    total = run_ref[...] + jnp.sum(onehot, axis=1, keepdims=True)
    run_ref[...] = total
    cnt_ref[...] = total.astype(jnp.int32)

    sub_w = lax.broadcasted_iota(jnp.int32, (LANES, TM_ROUTE), 0)
    wcol_ref[...] = jnp.where(sub_w == 0, w1, jnp.where(sub_w == 1, w2, 0.0)).T


def _route(logits, br):
    n = logits.shape[0]
    return pl.pallas_call(
        _route_kernel,
        grid=(n // TM_ROUTE,),
        in_specs=[pl.BlockSpec((TM_ROUTE, LANES), lambda i: (i, 0)),
                  pl.BlockSpec((1, LANES), lambda i: (0, 0))],
        out_specs=[pl.BlockSpec((N_EXPERTS, TM_ROUTE), lambda i: (0, i)),
                   pl.BlockSpec((TM_ROUTE, LANES), lambda i: (i, 0)),
                   pl.BlockSpec((N_EXPERTS, LANES), lambda i: (0, 0))],
        out_shape=[jax.ShapeDtypeStruct((N_EXPERTS, n), jnp.int32),
                   jax.ShapeDtypeStruct((n, LANES), F32),
                   jax.ShapeDtypeStruct((N_EXPERTS, LANES), jnp.int32)],
        scratch_shapes=[pltpu.VMEM((N_EXPERTS, LANES), F32)],
        compiler_params=_cparams(("arbitrary",)),
        name="route",
    )(logits, br)


def _positions_kernel(cnt_ref, imeta_ref, pos_ref):
    im = imeta_ref[...]
    i1, i2, r1, r2 = im[0:1], im[1:2], im[2:3], im[3:4]
    start = jnp.int32(0)
    p1 = jnp.zeros_like(i1)
    p2 = jnp.zeros_like(i2)
    for e in range(N_EXPERTS):
        p1 = jnp.where(i1 == e, start, p1)
        p2 = jnp.where(i2 == e, start, p2)
        start = start + _round_up_tile(cnt_ref[e])
    pos_ref[...] = jnp.concatenate([p1 + r1, p2 + r2], axis=0)


def _positions(cnt, imeta):
    n = imeta.shape[1]
    return pl.pallas_call(
        _positions_kernel,
        grid_spec=pltpu.PrefetchScalarGridSpec(
            num_scalar_prefetch=1,
            grid=(n // TM_ROUTE,),
            in_specs=[pl.BlockSpec((N_EXPERTS, TM_ROUTE), lambda i, c: (0, i))],
            out_specs=pl.BlockSpec((2, TM_ROUTE), lambda i, c: (0, i)),
        ),
        out_shape=jax.ShapeDtypeStruct((2, n), jnp.int32),
        compiler_params=_cparams(("parallel",)),
        name="positions",
    )(cnt, imeta)


def _scatter_kernel(pos_ref, cnt_ref, xn_ref, fg_ref, xs_ref, hbuf, zbuf, sem, zsem, *, n_tokens):
    n_tiles = xs_ref.shape[0] // TM_FFN

    @pl.when(pl.program_id(0) == 0)
    def _():
        zbuf[...] = jnp.zeros_like(zbuf)

        def zero_tile(t):
            rows = pl.ds(pl.multiple_of(t * TM_FFN, TM_FFN), TM_FFN)
            return pltpu.make_async_copy(zbuf, xs_ref.at[rows], zsem)

        def each_zero_tile(action):
            end = jnp.int32(0)
            for e in range(N_EXPERTS):
                tiles = _round_up_tile(cnt_ref[e]) // TM_FFN
                end = end + tiles

                @pl.when(tiles > 0)
                def _(end=end):
                    action(zero_tile(end - 1))

            def tail(t, carry):
                action(zero_tile(t))
                return carry

            lax.fori_loop(end, n_tiles, tail, 0)

        each_zero_tile(lambda copy: copy.start())
        each_zero_tile(lambda copy: copy.wait())

    base = pl.program_id(0) * TM_SCAT
    hbuf[...] = _rms(xn_ref[...], fg_ref[...])

    def issue(r, carry):
        for k in range(2):
            dst = pos_ref[k * n_tokens + base + r]
            pltpu.make_async_copy(hbuf.at[pl.ds(r, 1)], xs_ref.at[pl.ds(dst, 1)], sem).start()
        return carry

    lax.fori_loop(0, TM_SCAT, issue, 0, unroll=8)
    for k in range(2):
        pltpu.make_async_copy(hbuf, xs_ref.at[pl.ds(0, TM_SCAT)], sem).wait()


def _scatter(pos, cnt, xn, fg):
    n = xn.shape[0]
    rows = 2 * n + N_EXPERTS * TM_FFN
    return pl.pallas_call(
        functools.partial(_scatter_kernel, n_tokens=n),
        grid_spec=pltpu.PrefetchScalarGridSpec(
            num_scalar_prefetch=2,
            grid=(n // TM_SCAT,),
            in_specs=[pl.BlockSpec((TM_SCAT, D_MODEL), lambda i, p, c: (i, 0)),
                      pl.BlockSpec((1, D_MODEL), lambda i, p, c: (0, 0))],
            out_specs=pl.BlockSpec(memory_space=pl.ANY),
            scratch_shapes=[pltpu.VMEM((TM_SCAT, D_MODEL), F32), pltpu.VMEM((TM_FFN, D_MODEL), F32),
                            pltpu.SemaphoreType.DMA(()), pltpu.SemaphoreType.DMA(())],
        ),
        out_shape=jax.ShapeDtypeStruct((rows, D_MODEL), F32),
        compiler_params=_cparams(("arbitrary",)),
        name="moe_scatter",
    )(pos, cnt, xn, fg)


def _combine_kernel(pos_ref, xn_ref, wcol_ref, ys_ref, o_ref, buf, sem, *, n_tokens):
    base = pl.program_id(0) * TM_COMB

    def issue(r, carry):
        for k in range(2):
            src = pos_ref[k * n_tokens + base + r]
            pltpu.make_async_copy(ys_ref.at[pl.ds(src, 1)], buf.at[k, pl.ds(r, 1)], sem).start()
        return carry

    lax.fori_loop(0, TM_COMB, issue, 0, unroll=8)
    for k in range(2):
        pltpu.make_async_copy(ys_ref.at[pl.ds(0, TM_COMB)], buf.at[k], sem).wait()
    w = wcol_ref[...]
    o_ref[...] = xn_ref[...] + (w[:, 0:1] * buf[0] + w[:, 1:2] * buf[1])


def _combine(pos, xn, wcol, ys):
    n = xn.shape[0]
    return pl.pallas_call(
        functools.partial(_combine_kernel, n_tokens=n),
        grid_spec=pltpu.PrefetchScalarGridSpec(
            num_scalar_prefetch=1,
            grid=(n // TM_COMB,),
            in_specs=[pl.BlockSpec((TM_COMB, D_MODEL), lambda i, p: (i, 0)),
                      pl.BlockSpec((TM_COMB, LANES), lambda i, p: (i, 0)),
                      pl.BlockSpec(memory_space=pl.ANY)],
            out_specs=pl.BlockSpec((TM_COMB, D_MODEL), lambda i, p: (i, 0)),
            scratch_shapes=[pltpu.VMEM((2, TM_COMB, D_MODEL), F32), pltpu.SemaphoreType.DMA(())],
        ),
        out_shape=jax.ShapeDtypeStruct((n, D_MODEL), F32),
        compiler_params=_cparams(("arbitrary",)),
        name="moe_combine",
    )(pos, xn, wcol, ys)


def _swiglu(h, wg_ref, wu_ref, wd_ref):
    y = None
    c0 = 0
    for width in FF_CHUNKS:
        cs = slice(c0, c0 + width)
        g = jnp.dot(h, wg_ref[:, cs], preferred_element_type=F32)
        u = jnp.dot(h, wu_ref[:, cs], preferred_element_type=F32)
        act = (g * jax.nn.sigmoid(g) * u).astype(BF16)
        part = jnp.dot(act, wd_ref[cs, :], preferred_element_type=F32)
        y = part if y is None else y + part
        c0 += width
    return y


def _ffn_dense_kernel(h_ref, x_ref, wg_ref, wu_ref, wd_ref, o_ref):
    o_ref[...] = x_ref[...] + _swiglu(h_ref[...], wg_ref, wu_ref, wd_ref)


def _ffn_dense(h2, x2, wgu, wd):
    n = h2.shape[0]
    return pl.pallas_call(
        _ffn_dense_kernel,
        grid=(n // TM_FFN,),
        in_specs=[
            pl.BlockSpec((TM_FFN, D_MODEL), lambda i: (i, 0)),
            pl.BlockSpec((TM_FFN, D_MODEL), lambda i: (i, 0)),
            pl.BlockSpec((D_MODEL, FF), lambda i: (0, 0)),
            pl.BlockSpec((D_MODEL, FF), lambda i: (0, 1)),
            pl.BlockSpec((FF, D_MODEL), lambda i: (0, 0)),
        ],
        out_specs=pl.BlockSpec((TM_FFN, D_MODEL), lambda i: (i, 0)),
        out_shape=jax.ShapeDtypeStruct((n, D_MODEL), F32),
        compiler_params=_cparams(("parallel",)),
        name="ffn_dense",
    )(h2, x2, wgu, wgu, wd)


def _tile_plan(i, cnt_ref):
    end = jnp.int32(0)
    expert = jnp.int32(0)
    for e in range(N_EXPERTS):
        end = end + _round_up_tile(cnt_ref[e])
        expert = expert + (end <= i * TM_FFN).astype(jnp.int32)
    return jnp.minimum(expert, N_EXPERTS - 1), end // TM_FFN


def _ffn_grouped_kernel(cnt_ref, xs_ref, wg_ref, wu_ref, wd_ref, o_ref):
    i = pl.program_id(0)
    _, used = _tile_plan(i, cnt_ref)

    @pl.when(i < used)
    def _():
        o_ref[...] = _swiglu(xs_ref[...].astype(BF16), wg_ref, wu_ref, wd_ref)

    @pl.when(i >= used)
    def _():
        o_ref[...] = jnp.zeros_like(o_ref)


def _ffn_grouped(cnt, xs, wgu, wd):
    rows = xs.shape[0]

    def tile(i, c):
        return jnp.minimum(i, _tile_plan(i, c)[1] - 1)

    return pl.pallas_call(
        _ffn_grouped_kernel,
        grid_spec=pltpu.PrefetchScalarGridSpec(
            num_scalar_prefetch=1,
            grid=(rows // TM_FFN,),
            in_specs=[
                pl.BlockSpec((TM_FFN, D_MODEL), lambda i, c: (tile(i, c), 0)),
                pl.BlockSpec((None, D_MODEL, FF), lambda i, c: (_tile_plan(i, c)[0], 0, 0)),
                pl.BlockSpec((None, D_MODEL, FF), lambda i, c: (_tile_plan(i, c)[0], 0, 1)),
                pl.BlockSpec((None, FF, D_MODEL), lambda i, c: (_tile_plan(i, c)[0], 0, 0)),
            ],
            out_specs=pl.BlockSpec((TM_FFN, D_MODEL), lambda i, c: (i, 0)),
        ),
        out_shape=jax.ShapeDtypeStruct((rows, D_MODEL), F32),
        compiler_params=_cparams(("arbitrary",)),
        name="ffn_grouped",
    )(cnt, xs, wgu, wgu, wd)


def _moe(xn, logits, br, fg, wgu, wd):
    n = xn.shape[0]
    imeta, wcol, cnt_lanes = _route(logits, br)
    cnt = cnt_lanes[:, 0]
    pos = _positions(cnt, imeta).reshape(2 * n)
    xs = _scatter(pos, cnt, xn, fg)
    ys = _ffn_grouped(cnt, xs, wgu, wd)
    return _combine(pos, xn, wcol, ys)


def _rope_tables(seq):
    pos = jnp.arange(seq, dtype=F32)
    inv_freq = ROPE_THETA ** (-jnp.arange(0, QK_ROPE, 2, dtype=F32) / QK_ROPE)
    ang = pos[:, None] * inv_freq[None, :]
    return jnp.cos(ang), jnp.sin(ang)


def _swap_halves(w):
    half = QK_ROPE // 2
    return jnp.concatenate([w[..., half:], w[..., :half]], axis=-1)


def _head_slots(w, width):
    k = w.shape[0]
    w3 = w.reshape(k, MLA_HEADS, width)
    return jnp.pad(w3, ((0, 0), (0, 0), (0, HEAD_SLOT - width))).reshape(k, QK_WIDTH)


def _pad_lanes(v, width=LANES):
    return jnp.pad(v, (0, width - v.shape[0])).reshape(1, width)


def kernel(x, mix_norm_g, w_in, gm_v_norm_g, gm_v_norm_b, gm_w_spatial, gm_b_spatial, gm_w_proj, mla_q_lat_g, mla_w_uq, mla_kv_lat_g, mla_w_ukv, mla_q_norm_g, mla_k_norm_g, mla_w_proj, w_out, ffn_norm_g, dense_w_gu, dense_w_down, moe_w_router, moe_b_router, moe_w_gu, moe_w_down):
    batch, seq, d = x.shape
    n = batch * seq
    depth = w_in.shape[0]
    cos, sin = _rope_tables(seq)
    x2 = x.reshape(n, d)
    q_scale = float(np.log2(np.e) / np.sqrt(QK_DIM))
    v_ones = jnp.tile(jnp.zeros((HEAD_SLOT,), F32).at[V_DIM].set(1.0), MLA_HEADS).reshape(1, QK_WIDTH)

    o_kr = 2 * GM_WIDTH + Q_LORA + KV_LORA
    o_gate = o_kr + QK_ROPE

    ones_nope, zeros_nope = jnp.ones((seq, QK_NOPE), F32), jnp.zeros((seq, QK_NOPE), F32)
    cc, ss = jnp.concatenate([cos, cos], axis=1), jnp.concatenate([-sin, sin], axis=1)
    tq = jnp.concatenate([ones_nope, cc, ss], axis=1)
    ta = jnp.concatenate([zeros_nope, cc, cc], axis=1)
    tb = jnp.concatenate([zeros_nope, ss, ss], axis=1)

    for l in range(depth):
        wl = w_in[l]
        w_kr = wl[:, o_kr:o_gate]
        w_krx = jnp.pad(jnp.concatenate([w_kr, w_kr], axis=1), ((0, 0), (QK_NOPE, 0)))
        w_krs = jnp.pad(jnp.concatenate([_swap_halves(w_kr)] * 2, axis=1), ((0, 0), (QK_NOPE, 0)))
        w_kr2 = jnp.concatenate([w_krx, w_krs], axis=1).astype(BF16)
        wq3 = mla_w_uq[l].reshape(Q_LORA, MLA_HEADS, QK_DIM)
        wq = jnp.concatenate([wq3, _swap_halves(wq3[:, :, QK_NOPE:])], axis=2).reshape(Q_LORA, QK_WIDTH).astype(BF16)
        wkv3 = mla_w_ukv[l].reshape(KV_LORA, MLA_HEADS, QK_NOPE + V_DIM)
        wk = _head_slots(wkv3[:, :, :QK_NOPE].reshape(KV_LORA, MLA_HEADS * QK_NOPE), QK_NOPE).astype(BF16)
        wv = _head_slots(wkv3[:, :, QK_NOPE:].reshape(KV_LORA, V_WIDTH), V_DIM).astype(BF16)
        gq, gk = mla_q_norm_g[l], mla_k_norm_g[l]
        g12 = lambda g: g[QK_NOPE:]
        g21 = lambda g: _swap_halves(g[QK_NOPE:])
        gq_row = (jnp.concatenate([gq[:QK_NOPE], g12(gq), g21(gq)]) * q_scale).reshape(1, HEAD_SLOT)
        ga_row = _pad_lanes(jnp.concatenate([jnp.zeros((QK_NOPE,), F32), g12(gk), g12(gk)]))
        gb_row = _pad_lanes(jnp.concatenate([jnp.zeros((QK_NOPE,), F32), g21(gk), g21(gk)]))
        uv, gates, q, k, v = _in_qkv(x2, mix_norm_g[l].reshape(1, d),
                                     wl[:, :o_kr].astype(BF16), w_kr2, wl[:, o_gate:].astype(BF16),
                                     mla_q_lat_g[l].reshape(1, Q_LORA), wq,
                                     mla_kv_lat_g[l].reshape(1, KV_LORA), wk, wv, v_ones,
                                     tq, ta, tb, gq_row, ga_row, gb_row, _pad_lanes(gk[:QK_NOPE]), seq)
        slots = lambda t: t.reshape(batch, seq, QK_WIDTH)
        ot = _attention(slots(q), slots(k), slots(v)).reshape(n, V_WIDTH)

        is_moe = l % 2 == 1
        m = l // 2
        wr = jnp.pad(moe_w_router[m], ((0, 0), (0, LANES - N_EXPERTS))) if is_moe else None
        outs = _merge(uv, gm_v_norm_g[l].reshape(1, GM_WIDTH), gm_v_norm_b[l].reshape(1, GM_WIDTH),
                      gm_w_spatial[l], gm_b_spatial[l].T, ot, gates, x2,
                      gm_w_proj[l].astype(BF16), mla_w_proj[l].astype(BF16),
                      w_out[l].astype(BF16), ffn_norm_g[l].reshape(1, d), wr)
        if is_moe:
            xn, logits = outs
            x2 = _moe(xn, logits, _pad_lanes(moe_b_router[m]), ffn_norm_g[l].reshape(1, d),
                      moe_w_gu[m].astype(BF16), moe_w_down[m].astype(BF16))
        else:
            xn, h2 = outs
            x2 = _ffn_dense(h2, xn, dense_w_gu[m].astype(BF16), dense_w_down[m].astype(BF16))
    return x2.reshape(batch, seq, d)
```

```python
import functools

import jax
import jax.numpy as jnp
import numpy as np
from jax import lax
from jax.experimental import pallas as pl
from jax.experimental.pallas import tpu as pltpu

F32 = jnp.float32
BF16 = jnp.bfloat16

EPS = 1e-6
LANES = 128

D_MODEL = 1024
GM_GROUPS = 8
GM_GROUP_CH = 128
GM_WIDTH = 1024
GM_CHUNK = 128
MLA_HEADS = 16
QK_NOPE = 64
QK_ROPE = 32
QK_DIM = 96
V_DIM = 64
Q_LORA = 512
KV_LORA = 256
ROPE_THETA = 10000.0
HEAD_SLOT = LANES
QK_WIDTH = MLA_HEADS * HEAD_SLOT
V_WIDTH = MLA_HEADS * V_DIM
N_EXPERTS = 8
FF = 2816

TM_IN = 256
TM_MERGE = 512
TM_FFN = 512
FF_CHUNKS = (768, 768, 768, 512)
TM_ROUTE = 512
TM_SCAT = 256
TM_COMB = 256
ATT_T = 512
ATT_G = 2
ATT_SUM_ROWS = 16

VMEM_LIMIT = 56 * 1024 * 1024


def _cparams(sem):
    return pltpu.CompilerParams(dimension_semantics=sem, vmem_limit_bytes=VMEM_LIMIT)


def _rms(xf, g):
    return xf * lax.rsqrt(jnp.mean(xf * xf, axis=-1, keepdims=True) + EPS) * g


def _in_qkv_kernel(x_ref, g_ref, w_ref, wkr_ref, wgate_ref, qg_ref, wq_ref, kvg_ref, wk_ref, wvt_ref,
                   tq_ref, ta_ref, tb_ref, gq_ref, ga_ref, gb_ref, gkn_ref,
                   uv_ref, gate_ref, q_ref, k_ref, vt_ref):
    h = _rms(x_ref[...], g_ref[...]).astype(BF16)
    o_lat = 2 * GM_WIDTH
    lat = jnp.dot(h, w_ref[:, o_lat:], preferred_element_type=F32)
    krx = jnp.dot(h, wkr_ref[...], preferred_element_type=F32)
    uv_ref[...] = jnp.dot(h, w_ref[:, :o_lat], preferred_element_type=F32).astype(BF16)
    gate_ref[...] = jnp.dot(h, wgate_ref[...], preferred_element_type=F32).astype(BF16)

    cq = _rms(lat[:, :Q_LORA], qg_ref[...]).astype(BF16)
    ckv = _rms(lat[:, Q_LORA:], kvg_ref[...]).astype(BF16)
    kr = krx[:, :LANES]
    kr_sw = krx[:, LANES:]
    kr_ssq = 0.5 * jnp.sum(kr * kr, axis=-1, keepdims=True)
    kr_rot = kr * (ta_ref[...] * ga_ref[...]) + kr_sw * (tb_ref[...] * gb_ref[...])
    tq, gkn = tq_ref[...] * gq_ref[...], gkn_ref[...]
    q_all = jnp.dot(cq, wq_ref[...], preferred_element_type=F32)
    k_all = jnp.dot(ckv, wk_ref[...], preferred_element_type=F32)
    first_copy = lax.broadcasted_iota(jnp.int32, (1, HEAD_SLOT), 1) < QK_DIM
    for hd in range(MLA_HEADS):
        sl = slice(hd * HEAD_SLOT, (hd + 1) * HEAD_SLOT)
        qh = q_all[:, sl]
        ssq = jnp.sum(qh * jnp.where(first_copy, qh, 0.0), axis=-1, keepdims=True)
        q_ref[:, sl] = (qh * tq * lax.rsqrt(ssq * (1.0 / QK_DIM) + EPS)).astype(BF16)
        kh = k_all[:, sl]
        rk = lax.rsqrt((jnp.sum(kh * kh, axis=-1, keepdims=True) + kr_ssq) * (1.0 / QK_DIM) + EPS)
        k_ref[:, sl] = ((kh * gkn + kr_rot) * rk).astype(BF16)
    vt_ref[...] = lax.dot_general(wvt_ref[...], ckv, (((1,), (1,)), ((), ())),
                                  preferred_element_type=F32).astype(BF16)


def _in_qkv(x2, g, w_main, w_kr, w_gate, qg, wq, kvg, wk, wvt, tq, ta, tb, gq, ga, gb, gkn, batch, seq):
    n = x2.shape[0]
    tps = seq // TM_IN
    full = lambda shape: pl.BlockSpec(shape, lambda i: (0,) * len(shape), pipeline_mode=pl.Buffered(1))
    rope = pl.BlockSpec((TM_IN, HEAD_SLOT), lambda i: (i % tps, 0))
    row = lambda w: pl.BlockSpec((TM_IN, w), lambda i: (i, 0))
    lane_row = full((1, HEAD_SLOT))
    return pl.pallas_call(
        _in_qkv_kernel,
        grid=(n // TM_IN,),
        in_specs=[
            row(D_MODEL), full((1, D_MODEL)), full(w_main.shape), full(w_kr.shape), full(w_gate.shape),
            full((1, Q_LORA)), full((Q_LORA, QK_WIDTH)),
            full((1, KV_LORA)), full((KV_LORA, QK_WIDTH)), full((V_WIDTH, KV_LORA)),
            rope, rope, rope, lane_row, lane_row, lane_row, lane_row,
        ],
        out_specs=[row(2 * GM_WIDTH), row(2 * D_MODEL), row(QK_WIDTH), row(QK_WIDTH),
                   pl.BlockSpec((None, None, V_WIDTH, TM_IN), lambda i: (i // tps, i % tps, 0, 0))],
        out_shape=[jax.ShapeDtypeStruct((n, 2 * GM_WIDTH), BF16), jax.ShapeDtypeStruct((n, 2 * D_MODEL), BF16),
                   jax.ShapeDtypeStruct((n, QK_WIDTH), BF16), jax.ShapeDtypeStruct((n, QK_WIDTH), BF16),
                   jax.ShapeDtypeStruct((batch, tps, V_WIDTH, TM_IN), BF16)],
        compiler_params=_cparams(("parallel",)),
        name="in_qkv",
    )(x2, g, w_main, w_kr, w_gate, qg, wq, kvg, wk, wvt, tq, ta, tb, gq, ga, gb, gkn)


def _attention_kernel(q_ref, k_ref, vt_ref, o_ref, *, seq):
    nblk = seq // ATT_T
    vblocks = ATT_T // TM_IN
    key_pos = lax.broadcasted_iota(jnp.int32, (ATT_T, ATT_T), 0)
    qry_pos = lax.broadcasted_iota(jnp.int32, (ATT_T, ATT_T), 1)
    causal = key_pos <= qry_pos
    nt = (((1,), (1,)), ((), ()))
    ones_rows = jnp.ones((ATT_SUM_ROWS, ATT_T), BF16)

    def update(state, g, st, kj, diagonal):
        m, acc = state
        if diagonal:
            st = jnp.where(causal, st, -1e30)
        m_new = jnp.maximum(m, jnp.max(st, axis=0, keepdims=True))
        alpha = jnp.exp2(m - m_new)
        p = jnp.exp2((st - m_new).astype(BF16))
        vt = [vt_ref[kj * vblocks + b, g * V_DIM:(g + 1) * V_DIM, :] for b in range(vblocks)]
        v_aug = jnp.concatenate([jnp.concatenate(vt, axis=1), ones_rows], axis=0)
        return m_new, alpha * acc + jnp.dot(v_aug, p, preferred_element_type=F32)

    for qi in range(nblk):
        q0, q1 = qi * ATT_T, (qi + 1) * ATT_T
        strips = [lax.dot_general(k_ref[0:q1, g * HEAD_SLOT:(g + 1) * HEAD_SLOT],
                                  q_ref[q0:q1, g * HEAD_SLOT:(g + 1) * HEAD_SLOT], nt,
                                  preferred_element_type=F32) for g in range(ATT_G)]
        acc_rows = V_DIM + ATT_SUM_ROWS
        init = (jnp.full((1, ATT_T), -1e30, F32), jnp.zeros((acc_rows, ATT_T), F32))
        states = [init for _ in range(ATT_G)]
        for kj in range(qi + 1):
            states = [update(states[g], g, strips[g][kj * ATT_T:(kj + 1) * ATT_T], kj, kj == qi)
                      for g in range(ATT_G)]
        pad_rows = jnp.zeros((HEAD_SLOT - acc_rows, ATT_T), F32)
        for g in range(ATT_G):
            acc = jnp.concatenate([states[g][1], pad_rows], axis=0).T
            o_ref[q0:q1, g * V_DIM:(g + 1) * V_DIM] = (acc[:, :V_DIM] / acc[:, V_DIM:V_DIM + 1]).astype(BF16)


def _attention(q3, k3, vt4):
    batch, seq, _ = q3.shape
    slot = pl.BlockSpec((None, seq, ATT_G * HEAD_SLOT), lambda b, h: (b, 0, h))
    return pl.pallas_call(
        functools.partial(_attention_kernel, seq=seq),
        grid=(batch, MLA_HEADS // ATT_G),
        in_specs=[slot, slot,
                  pl.BlockSpec((None, seq // TM_IN, ATT_G * V_DIM, TM_IN), lambda b, h: (b, 0, h, 0))],
        out_specs=pl.BlockSpec((None, seq, ATT_G * V_DIM), lambda b, h: (b, 0, h)),
        out_shape=jax.ShapeDtypeStruct((batch, seq, V_WIDTH), BF16),
        compiler_params=_cparams(("parallel", "parallel")),
        name="attention",
    )(q3, k3, vt4)


def _split_bf16(v):
    hi = v.astype(BF16)
    return hi, (v - hi.astype(F32)).astype(BF16)


def _gelu_tanh(x):
    c1 = -2.0 * float(np.sqrt(2.0 / np.pi) * np.log2(np.e))
    c2 = c1 * 0.044715
    return x / (1.0 + jnp.exp2(x * (c1 + c2 * (x * x))))


def _gmlp_tile(uv_ref, vg_ref, vb_ref, ws_ref, bst_ref, a_ref):
    v = _gelu_tanh(uv_ref[:, GM_WIDTH:].astype(F32))
    mu = jnp.mean(v, axis=-1, keepdims=True)
    vc = v - mu
    vn = (vc * lax.rsqrt(jnp.mean(vc * vc, axis=-1, keepdims=True) + EPS) * vg_ref[...] + vb_ref[...]).astype(BF16)
    row = lax.broadcasted_iota(jnp.int32, (GM_CHUNK, GM_CHUNK), 0)
    col = lax.broadcasted_iota(jnp.int32, (GM_CHUNK, GM_CHUNK), 1)
    tril = col <= row
    bst = bst_ref[...]
    for g in range(GM_GROUPS):
        ws = jnp.where(tril, ws_ref[g], 0.0).astype(BF16)
        bias = bst[:, g:g + 1]
        cs = slice(g * GM_GROUP_CH, (g + 1) * GM_GROUP_CH)
        for c in range(TM_MERGE // GM_CHUNK):
            rs = slice(c * GM_CHUNK, (c + 1) * GM_CHUNK)
            mixed = jnp.dot(ws, vn[rs, cs], preferred_element_type=F32) + bias
            u = _gelu_tanh(uv_ref[rs, cs].astype(F32))
            a_ref[rs, cs] = (u * mixed).astype(BF16)


def _merge_kernel(uv_ref, vg_ref, vb_ref, ws_ref, bst_ref, ot_ref, gate_ref, x_ref, wa_ref, wb_ref, wo_ref,
                  fg_ref, *rest, with_router):
    if with_router:
        wr_ref, xn_ref, lg_ref, a_ref = rest
    else:
        xn_ref, h2_ref, a_ref = rest
    yb = jnp.dot(ot_ref[...], wb_ref[...], preferred_element_type=F32)
    _gmlp_tile(uv_ref, vg_ref, vb_ref, ws_ref, bst_ref, a_ref)
    ya = jnp.dot(a_ref[...], wa_ref[...], preferred_element_type=F32)
    ga =jax.nn.sigmoid(gate_ref[:, :D_MODEL].astype(F32))
    gb = jax.nn.sigmoid(gate_ref[:, D_MODEL:].astype(F32))
    merged = (ga * ya + gb * yb).astype(BF16)
    xn = x_ref[...] + jnp.dot(merged, wo_ref[...], preferred_element_type=F32)
    xn_ref[...] = xn
    h2 = _rms(xn, fg_ref[...])
    if with_router:
        h_hi, h_lo = _split_bf16(h2)
        w_hi, w_lo = _split_bf16(wr_ref[...])
        hh_hl = jnp.dot(h_hi, jnp.concatenate([w_hi, w_lo], axis=1), preferred_element_type=F32)
        lg_ref[...] = hh_hl[:, :LANES] + (hh_hl[:, LANES:] + jnp.dot(h_lo, w_hi, preferred_element_type=F32))
    else:
        h2_ref[...] = h2.astype(BF16)


def _merge(uv, vg, vb, ws, bst, ot, gates, x2, wa, wb, wo, fg, wr):
    n = x2.shape[0]
    with_router = wr is not None
    full = lambda shape: pl.BlockSpec(shape, lambda i: (0,) * len(shape))
    row = lambda w: pl.BlockSpec((TM_MERGE, w), lambda i: (i, 0))
    in_specs = [
        row(2 * GM_WIDTH), full((1, GM_WIDTH)), full((1, GM_WIDTH)),
        full((GM_GROUPS, GM_CHUNK, GM_CHUNK)), full((GM_CHUNK, GM_GROUPS)),
        row(V_WIDTH), row(2 * D_MODEL), row(D_MODEL),
        full((GM_WIDTH, D_MODEL)), full((V_WIDTH, D_MODEL)), full((D_MODEL, D_MODEL)), full((1, D_MODEL)),
    ]
    args = [uv, vg, vb, ws, bst, ot, gates, x2, wa, wb, wo, fg]
    if with_router:
        in_specs.append(full((D_MODEL, LANES)))
        out_specs = [row(D_MODEL), row(LANES)]
        out_shape = [jax.ShapeDtypeStruct((n, D_MODEL), F32), jax.ShapeDtypeStruct((n, LANES), F32)]
        args.append(wr)
    else:
        out_specs = [row(D_MODEL), row(D_MODEL)]
        out_shape = [jax.ShapeDtypeStruct((n, D_MODEL), F32), jax.ShapeDtypeStruct((n, D_MODEL), BF16)]
    return pl.pallas_call(
        functools.partial(_merge_kernel, with_router=with_router),
        grid=(n // TM_MERGE,),
        in_specs=in_specs, out_specs=out_specs, out_shape=out_shape,
        scratch_shapes=[pltpu.VMEM((TM_MERGE, GM_WIDTH), BF16)],
        compiler_params=_cparams(("parallel",)),
        name="merge_router" if with_router else "merge",
    )(*args)


def _round_up_tile(c):
    return (c + (TM_FFN - 1)) & (-TM_FFN)


def _route_kernel(lg_ref, br_ref, imeta_ref, wcol_ref, cnt_ref, run_ref):
    @pl.when(pl.program_id(0) == 0)
    def _():
        run_ref[...] = jnp.zeros_like(run_ref)

    lt = (lg_ref[...] + br_ref[...]).T[:N_EXPERTS, :]
    sub = lax.broadcasted_iota(jnp.int32, lt.shape, 0)
    m1 = jnp.max(lt, axis=0, keepdims=True)
    i1 = jnp.min(jnp.where(lt == m1, sub, N_EXPERTS), axis=0, keepdims=True)
    rest = jnp.where(sub == i1, -jnp.inf, lt)
    m2 = jnp.max(rest, axis=0, keepdims=True)
    i2 = jnp.min(jnp.where(rest == m2, sub, N_EXPERTS), axis=0, keepdims=True)
    e2 = jnp.exp(m2 - m1)
    w1 = 1.0 / (1.0 + e2)
    w2 = e2 / (1.0 + e2)

    onehot = jnp.where((sub == i1) | (sub == i2), 1.0, 0.0)
    src = lax.broadcasted_iota(jnp.int32, (TM_ROUTE, TM_ROUTE), 0)
    dst = lax.broadcasted_iota(jnp.int32, (TM_ROUTE, TM_ROUTE), 1)
    earlier = jnp.where(src < dst, 1.0, 0.0).astype(BF16)
    seen = jnp.dot(onehot.astype(BF16), earlier, preferred_element_type=F32) + run_ref[:, :1]
    r1 = jnp.sum(jnp.where(sub == i1, seen, 0.0), axis=0, keepdims=True).astype(jnp.int32)
    r2 = jnp.sum(jnp.where(sub == i2, seen, 0.0), axis=0, keepdims=True).astype(jnp.int32)
    imeta_ref[...] = jnp.where(sub == 0, i1, jnp.where(sub == 1, i2, jnp.where(sub == 2, r1, jnp.where(sub == 3, r2, 0))))

    total = run_ref[...] + jnp.sum(onehot, axis=1, keepdims=True)
    run_ref[...] = total
    cnt_ref[...] = total.astype(jnp.int32)

    sub_w = lax.broadcasted_iota(jnp.int32, (LANES, TM_ROUTE), 0)
    wcol_ref[...] = jnp.where(sub_w == 0, w1, jnp.where(sub_w == 1, w2, 0.0)).T


def _route(logits, br):
    n = logits.shape[0]
    return pl.pallas_call(
        _route_kernel,
        grid=(n // TM_ROUTE,),
        in_specs=[pl.BlockSpec((TM_ROUTE, LANES), lambda i: (i, 0)),
                  pl.BlockSpec((1, LANES), lambda i: (0, 0))],
        out_specs=[pl.BlockSpec((N_EXPERTS, TM_ROUTE), lambda i: (0, i)),
                   pl.BlockSpec((TM_ROUTE, LANES), lambda i: (i, 0)),
                   pl.BlockSpec((N_EXPERTS, LANES), lambda i: (0, 0))],
        out_shape=[jax.ShapeDtypeStruct((N_EXPERTS, n), jnp.int32),
                   jax.ShapeDtypeStruct((n, LANES), F32),
                   jax.ShapeDtypeStruct((N_EXPERTS, LANES), jnp.int32)],
        scratch_shapes=[pltpu.VMEM((N_EXPERTS, LANES), F32)],
        compiler_params=_cparams(("arbitrary",)),
        name="route",
    )(logits, br)


def _positions_kernel(cnt_ref, imeta_ref, pos_ref):
    im = imeta_ref[...]
    i1, i2, r1, r2 = im[0:1], im[1:2], im[2:3], im[3:4]
    start = jnp.int32(0)
    p1 = jnp.zeros_like(i1)
    p2 = jnp.zeros_like(i2)
    for e in range(N_EXPERTS):
        p1 = jnp.where(i1 == e, start, p1)
        p2 = jnp.where(i2 == e, start, p2)
        start = start + _round_up_tile(cnt_ref[e])
    pos_ref[...] = jnp.concatenate([p1 + r1, p2 + r2], axis=0)


def _positions(cnt, imeta):
    n = imeta.shape[1]
    return pl.pallas_call(
        _positions_kernel,
        grid_spec=pltpu.PrefetchScalarGridSpec(
            num_scalar_prefetch=1,
            grid=(n // TM_ROUTE,),
            in_specs=[pl.BlockSpec((N_EXPERTS, TM_ROUTE), lambda i, c: (0, i))],
            out_specs=pl.BlockSpec((2, TM_ROUTE), lambda i, c: (0, i)),
        ),
        out_shape=jax.ShapeDtypeStruct((2, n), jnp.int32),
        compiler_params=_cparams(("parallel",)),
        name="positions",
    )(cnt, imeta)


def _scatter_kernel(pos_ref, cnt_ref, xn_ref, fg_ref, xs_ref, hbuf, zbuf, sem, zsem, *, n_tokens):
    n_tiles = xs_ref.shape[0] // TM_FFN

    @pl.when(pl.program_id(0) == 0)
    def _():
        zbuf[...] = jnp.zeros_like(zbuf)

        def zero_tile(t):
            rows = pl.ds(pl.multiple_of(t * TM_FFN, TM_FFN), TM_FFN)
            return pltpu.make_async_copy(zbuf, xs_ref.at[rows], zsem)

        def each_zero_tile(action):
            end = jnp.int32(0)
            for e in range(N_EXPERTS):
                tiles = _round_up_tile(cnt_ref[e]) // TM_FFN
                end = end + tiles

                @pl.when(tiles > 0)
                def _(end=end):
                    action(zero_tile(end - 1))

            def tail(t, carry):
                action(zero_tile(t))
                return carry

            lax.fori_loop(end, n_tiles, tail, 0)

        each_zero_tile(lambda copy: copy.start())
        each_zero_tile(lambda copy: copy.wait())

    base = pl.program_id(0) * TM_SCAT
    hbuf[...] = _rms(xn_ref[...], fg_ref[...])

    def issue(r, carry):
        for k in range(2):
            dst = pos_ref[k * n_tokens + base + r]
            pltpu.make_async_copy(hbuf.at[pl.ds(r, 1)], xs_ref.at[pl.ds(dst, 1)], sem).start(priority=k)
        return carry

    lax.fori_loop(0, TM_SCAT, issue, 0, unroll=8)
    for k in range(2):
        pltpu.make_async_copy(hbuf, xs_ref.at[pl.ds(0, TM_SCAT)], sem).wait()


def _scatter(pos, cnt, xn, fg):
    n = xn.shape[0]
    rows = 2 * n + N_EXPERTS * TM_FFN
    return pl.pallas_call(
        functools.partial(_scatter_kernel, n_tokens=n),
        grid_spec=pltpu.PrefetchScalarGridSpec(
            num_scalar_prefetch=2,
            grid=(n // TM_SCAT,),
            in_specs=[pl.BlockSpec((TM_SCAT, D_MODEL), lambda i, p, c: (i, 0)),
                      pl.BlockSpec((1, D_MODEL), lambda i, p, c: (0, 0))],
            out_specs=pl.BlockSpec(memory_space=pl.ANY),
            scratch_shapes=[pltpu.VMEM((TM_SCAT, D_MODEL), F32), pltpu.VMEM((TM_FFN, D_MODEL), F32),
                            pltpu.SemaphoreType.DMA(()), pltpu.SemaphoreType.DMA(())],
        ),
        out_shape=jax.ShapeDtypeStruct((rows, D_MODEL), F32),
        compiler_params=_cparams(("arbitrary",)),
        name="moe_scatter",
    )(pos, cnt, xn, fg)


def _combine_kernel(pos_ref, xn_ref, wcol_ref, ys_ref, o_ref, buf, sem, *, n_tokens):
    base = pl.program_id(0) * TM_COMB

    def issue(r, carry):
        for k in range(2):
            src = pos_ref[k * n_tokens + base + r]
            pltpu.make_async_copy(ys_ref.at[pl.ds(src, 1)], buf.at[k, pl.ds(r, 1)], sem).start(priority=k)
        return carry

    lax.fori_loop(0, TM_COMB, issue, 0, unroll=8)
    for k in range(2):
        pltpu.make_async_copy(ys_ref.at[pl.ds(0, TM_COMB)], buf.at[k], sem).wait()
    w = wcol_ref[...]
    o_ref[...] = xn_ref[...] + (w[:, 0:1] * buf[0] + w[:, 1:2] * buf[1])


def _combine(pos, xn, wcol, ys):
    n = xn.shape[0]
    return pl.pallas_call(
        functools.partial(_combine_kernel, n_tokens=n),
        grid_spec=pltpu.PrefetchScalarGridSpec(
            num_scalar_prefetch=1,
            grid=(n // TM_COMB,),
            in_specs=[pl.BlockSpec((TM_COMB, D_MODEL), lambda i, p: (i, 0)),
                      pl.BlockSpec((TM_COMB, LANES), lambda i, p: (i, 0)),
                      pl.BlockSpec(memory_space=pl.ANY)],
            out_specs=pl.BlockSpec((TM_COMB, D_MODEL), lambda i, p: (i, 0)),
            scratch_shapes=[pltpu.VMEM((2, TM_COMB, D_MODEL), F32), pltpu.SemaphoreType.DMA(())],
        ),
        out_shape=jax.ShapeDtypeStruct((n, D_MODEL), F32),
        compiler_params=_cparams(("arbitrary",)),
        name="moe_combine",
    )(pos, xn, wcol, ys)


def _swiglu(h, wg_ref, wu_ref, wd_ref):
    y = None
    c0 = 0
    for width in FF_CHUNKS:
        cs = slice(c0, c0 + width)
        g = jnp.dot(h, wg_ref[:, cs], preferred_element_type=F32)
        u = jnp.dot(h, wu_ref[:, cs], preferred_element_type=F32)
        act = (g * jax.nn.sigmoid(g) * u).astype(BF16)
        part = jnp.dot(act, wd_ref[cs, :], preferred_element_type=F32)
        y = part if y is None else y + part
        c0 += width
    return y


def _ffn_dense_kernel(h_ref, x_ref, wg_ref, wu_ref, wd_ref, o_ref):
    o_ref[...] = x_ref[...] + _swiglu(h_ref[...], wg_ref, wu_ref, wd_ref)


def _ffn_dense(h2, x2, wgu, wd):
    n = h2.shape[0]
    return pl.pallas_call(
        _ffn_dense_kernel,
        grid=(n // TM_FFN,),
        in_specs=[
            pl.BlockSpec((TM_FFN, D_MODEL), lambda i: (i, 0)),
            pl.BlockSpec((TM_FFN, D_MODEL), lambda i: (i, 0)),
            pl.BlockSpec((D_MODEL, FF), lambda i: (0, 0)),
            pl.BlockSpec((D_MODEL, FF), lambda i: (0, 1)),
            pl.BlockSpec((FF, D_MODEL), lambda i: (0, 0)),
        ],
        out_specs=pl.BlockSpec((TM_FFN, D_MODEL), lambda i: (i, 0)),
        out_shape=jax.ShapeDtypeStruct((n, D_MODEL), F32),
        compiler_params=_cparams(("parallel",)),
        name="ffn_dense",
    )(h2, x2, wgu, wgu, wd)


def _tile_plan(i, cnt_ref):
    end = jnp.int32(0)
    expert = jnp.int32(0)
    for e in range(N_EXPERTS):
        end = end + _round_up_tile(cnt_ref[e])
        expert = expert + (end <= i * TM_FFN).astype(jnp.int32)
    return jnp.minimum(expert, N_EXPERTS - 1), end // TM_FFN


def _ffn_grouped_kernel(cnt_ref, xs_ref, wg_ref, wu_ref, wd_ref, o_ref):
    i = pl.program_id(0)
    _, used = _tile_plan(i, cnt_ref)

    @pl.when(i < used)
    def _():
        o_ref[...] = _swiglu(xs_ref[...].astype(BF16), wg_ref, wu_ref, wd_ref)

    @pl.when(i >= used)
    def _():
        o_ref[...] = jnp.zeros_like(o_ref)


def _ffn_grouped(cnt, xs, wgu, wd):
    rows = xs.shape[0]

    def tile(i, c):
        return jnp.minimum(i, _tile_plan(i, c)[1] - 1)

    return pl.pallas_call(
        _ffn_grouped_kernel,
        grid_spec=pltpu.PrefetchScalarGridSpec(
            num_scalar_prefetch=1,
            grid=(rows // TM_FFN,),
            in_specs=[
                pl.BlockSpec((TM_FFN, D_MODEL), lambda i, c: (tile(i, c), 0)),
                pl.BlockSpec((None, D_MODEL, FF), lambda i, c: (_tile_plan(i, c)[0], 0, 0)),
                pl.BlockSpec((None, D_MODEL, FF), lambda i, c: (_tile_plan(i, c)[0], 0, 1)),
                pl.BlockSpec((None, FF, D_MODEL), lambda i, c: (_tile_plan(i, c)[0], 0, 0)),
            ],
            out_specs=pl.BlockSpec((TM_FFN, D_MODEL), lambda i, c: (i, 0)),
        ),
        out_shape=jax.ShapeDtypeStruct((rows, D_MODEL), F32),
        compiler_params=_cparams(("arbitrary",)),
        name="ffn_grouped",
    )(cnt, xs, wgu, wgu, wd)


def _moe(xn, logits, br, fg, wgu, wd):
    n = xn.shape[0]
    imeta, wcol, cnt_lanes = _route(logits, br)
    cnt = cnt_lanes[:, 0]
    pos = _positions(cnt, imeta).reshape(2 * n)
    xs = _scatter(pos, cnt, xn, fg)
    ys = _ffn_grouped(cnt, xs, wgu, wd)
    return _combine(pos, xn, wcol, ys)


def _rope_tables(seq):
    pos = jnp.arange(seq, dtype=F32)
    inv_freq = ROPE_THETA ** (-jnp.arange(0, QK_ROPE, 2, dtype=F32) / QK_ROPE)
    ang = pos[:, None] * inv_freq[None, :]
    return jnp.cos(ang), jnp.sin(ang)


def _swap_halves(w):
    half = QK_ROPE // 2
    return jnp.concatenate([w[..., half:], w[..., :half]], axis=-1)


def _head_slots(w, width):
    k = w.shape[0]
    w3 = w.reshape(k, MLA_HEADS, width)
    return jnp.pad(w3, ((0, 0), (0, 0), (0, HEAD_SLOT - width))).reshape(k, QK_WIDTH)


def _pad_lanes(v, width=LANES):
    return jnp.pad(v, (0, width - v.shape[0])).reshape(1, width)


def kernel(x, mix_norm_g, w_in, gm_v_norm_g, gm_v_norm_b, gm_w_spatial, gm_b_spatial, gm_w_proj, mla_q_lat_g, mla_w_uq, mla_kv_lat_g, mla_w_ukv, mla_q_norm_g, mla_k_norm_g, mla_w_proj, w_out, ffn_norm_g, dense_w_gu, dense_w_down, moe_w_router, moe_b_router, moe_w_gu, moe_w_down):
    batch, seq, d = x.shape
    n = batch * seq
    depth = w_in.shape[0]
    cos, sin = _rope_tables(seq)
    x2 = x.reshape(n, d)
    q_scale = float(np.log2(np.e) / np.sqrt(QK_DIM))

    o_kr = 2 * GM_WIDTH + Q_LORA + KV_LORA
    o_gate = o_kr + QK_ROPE

    ones_nope, zeros_nope = jnp.ones((seq, QK_NOPE), F32), jnp.zeros((seq, QK_NOPE), F32)
    cc, ss = jnp.concatenate([cos, cos], axis=1), jnp.concatenate([-sin, sin], axis=1)
    tq = jnp.concatenate([ones_nope, cc, ss], axis=1)
    ta = jnp.concatenate([zeros_nope, cc, cc], axis=1)
    tb = jnp.concatenate([zeros_nope, ss, ss], axis=1)

    for l in range(depth):
        wl = w_in[l]
        w_kr = wl[:, o_kr:o_gate]
        w_krx = jnp.pad(jnp.concatenate([w_kr, w_kr], axis=1), ((0, 0), (QK_NOPE, 0)))
        w_krs = jnp.pad(jnp.concatenate([_swap_halves(w_kr)] * 2, axis=1), ((0, 0), (QK_NOPE, 0)))
        w_kr2 = jnp.concatenate([w_krx, w_krs], axis=1).astype(BF16)
        wq3 = mla_w_uq[l].reshape(Q_LORA, MLA_HEADS, QK_DIM)
        wq = jnp.concatenate([wq3, _swap_halves(wq3[:, :, QK_NOPE:])], axis=2).reshape(Q_LORA, QK_WIDTH).astype(BF16)
        wkv3 = mla_w_ukv[l].reshape(KV_LORA, MLA_HEADS, QK_NOPE + V_DIM)
        wk = _head_slots(wkv3[:, :, :QK_NOPE].reshape(KV_LORA, MLA_HEADS * QK_NOPE), QK_NOPE).astype(BF16)
        wvt = wkv3[:, :, QK_NOPE:].reshape(KV_LORA, V_WIDTH).T.astype(BF16)
        gq, gk = mla_q_norm_g[l], mla_k_norm_g[l]
        g12 = lambda g: g[QK_NOPE:]
        g21 = lambda g: _swap_halves(g[QK_NOPE:])
        gq_row = (jnp.concatenate([gq[:QK_NOPE], g12(gq), g21(gq)]) * q_scale).reshape(1, HEAD_SLOT)
        ga_row = _pad_lanes(jnp.concatenate([jnp.zeros((QK_NOPE,), F32), g12(gk), g12(gk)]))
        gb_row = _pad_lanes(jnp.concatenate([jnp.zeros((QK_NOPE,), F32), g21(gk), g21(gk)]))
        uv, gates, q, k, vt = _in_qkv(x2, mix_norm_g[l].reshape(1, d),
                                      wl[:, :o_kr].astype(BF16), w_kr2, wl[:, o_gate:].astype(BF16),
                                      mla_q_lat_g[l].reshape(1, Q_LORA), wq,
                                      mla_kv_lat_g[l].reshape(1, KV_LORA), wk, wvt,
                                      tq, ta, tb, gq_row, ga_row, gb_row, _pad_lanes(gk[:QK_NOPE]), batch, seq)
        slots = lambda t: t.reshape(batch, seq, QK_WIDTH)
        ot = _attention(slots(q), slots(k), vt).reshape(n, V_WIDTH)

        is_moe = l % 2 == 1
        m = l // 2
        wr = jnp.pad(moe_w_router[m], ((0, 0), (0, LANES - N_EXPERTS))) if is_moe else None
        outs = _merge(uv, gm_v_norm_g[l].reshape(1, GM_WIDTH), gm_v_norm_b[l].reshape(1, GM_WIDTH),
                      gm_w_spatial[l], gm_b_spatial[l].T, ot, gates, x2,
                      gm_w_proj[l].astype(BF16), mla_w_proj[l].astype(BF16),
                      w_out[l].astype(BF16), ffn_norm_g[l].reshape(1, d), wr)
        if is_moe:
            xn, logits = outs
            x2 = _moe(xn, logits, _pad_lanes(moe_b_router[m]), ffn_norm_g[l].reshape(1, d),
                      moe_w_gu[m].astype(BF16), moe_w_down[m].astype(BF16))
        else:
            xn, h2 = outs
            x2 = _ffn_dense(h2, xn, dense_w_gu[m].astype(BF16), dense_w_down[m].astype(BF16))
    return x2.reshape(batch, seq, d)
```

```python
import functools

import jax
import jax.numpy as jnp
import numpy as np
from jax import lax
from jax.experimental import pallas as pl
from jax.experimental.pallas import tpu as pltpu

F32 = jnp.float32
BF16 = jnp.bfloat16

EPS = 1e-6
LANES = 128

D_MODEL = 1024
GM_GROUPS = 8
GM_GROUP_CH = 128
GM_WIDTH = 1024
GM_CHUNK = 128
MLA_HEADS = 16
QK_NOPE = 64
QK_ROPE = 32
QK_DIM = 96
V_DIM = 64
Q_LORA = 512
KV_LORA = 256
ROPE_THETA = 10000.0
HEAD_SLOT = LANES
QK_WIDTH = MLA_HEADS * HEAD_SLOT
V_WIDTH = MLA_HEADS * V_DIM
N_EXPERTS = 8
FF = 2816

TM_IN = 256
TM_MERGE = 512
TM_FFN = 512
FF_CHUNKS = (768, 768, 768, 512)
TM_ROUTE = 512
TM_SCAT = 256
TM_COMB = 256
ATT_T = 512
ATT_G = 2
ATT_SUM_ROWS = 16

VMEM_LIMIT = 56 * 1024 * 1024


def _cparams(sem):
    return pltpu.CompilerParams(dimension_semantics=sem, vmem_limit_bytes=VMEM_LIMIT)


def _rms(xf, g):
    return xf * lax.rsqrt(jnp.mean(xf * xf, axis=-1, keepdims=True) + EPS) * g


def _in_qkv_kernel(x_ref, g_ref, w_ref, wkr_ref, wgate_ref, qg_ref, wq_ref, kvg_ref, wk_ref, wvt_ref,
                   tq_ref, ta_ref, tb_ref, gq_ref, ga_ref, gb_ref, gkn_ref,
                   uv_ref, gate_ref, q_ref, k_ref, vt_ref):
    h = _rms(x_ref[...], g_ref[...]).astype(BF16)
    o_lat = 2 * GM_WIDTH
    lat = jnp.dot(h, w_ref[:, o_lat:], preferred_element_type=F32)
    krx = jnp.dot(h, wkr_ref[...], preferred_element_type=F32)
    uv_ref[...] = _gelu_tanh(jnp.dot(h, w_ref[:, :o_lat], preferred_element_type=F32)).astype(BF16)
    gate_ref[...] = jax.nn.sigmoid(jnp.dot(h, wgate_ref[...], preferred_element_type=F32)).astype(BF16)

    cq = _rms(lat[:, :Q_LORA], qg_ref[...]).astype(BF16)
    ckv = _rms(lat[:, Q_LORA:], kvg_ref[...]).astype(BF16)
    kr = krx[:, :LANES]
    kr_sw = krx[:, LANES:]
    kr_ssq = 0.5 * jnp.sum(kr * kr, axis=-1, keepdims=True)
    kr_rot = kr * (ta_ref[...] * ga_ref[...]) + kr_sw * (tb_ref[...] * gb_ref[...])
    tq, gkn = tq_ref[...] * gq_ref[...], gkn_ref[...]
    q_all = jnp.dot(cq, wq_ref[...], preferred_element_type=F32)
    k_all = jnp.dot(ckv, wk_ref[...], preferred_element_type=F32)
    first_copy = lax.broadcasted_iota(jnp.int32, (1, HEAD_SLOT), 1) < QK_DIM
    for hd in range(MLA_HEADS):
        sl = slice(hd * HEAD_SLOT, (hd + 1) * HEAD_SLOT)
        qh = q_all[:, sl]
        ssq = jnp.sum(qh * jnp.where(first_copy, qh, 0.0), axis=-1, keepdims=True)
        q_ref[:, sl] = (qh * tq * lax.rsqrt(ssq * (1.0 / QK_DIM) + EPS)).astype(BF16)
        kh = k_all[:, sl]
        rk = lax.rsqrt((jnp.sum(kh * kh, axis=-1, keepdims=True) + kr_ssq) * (1.0 / QK_DIM) + EPS)
        k_ref[:, sl] = ((kh * gkn + kr_rot) * rk).astype(BF16)
    vt_ref[...] = lax.dot_general(wvt_ref[...], ckv, (((1,), (1,)), ((), ())),
                                  preferred_element_type=F32).astype(BF16)


def _in_qkv(x2, g, w_main, w_kr, w_gate, qg, wq, kvg, wk, wvt, tq, ta, tb, gq, ga, gb, gkn, batch, seq):
    n = x2.shape[0]
    tps = seq // TM_IN
    full = lambda shape: pl.BlockSpec(shape, lambda i: (0,) * len(shape), pipeline_mode=pl.Buffered(1))
    rope = pl.BlockSpec((TM_IN, HEAD_SLOT), lambda i: (i % tps, 0))
    row = lambda w: pl.BlockSpec((TM_IN, w), lambda i: (i, 0))
    lane_row = full((1, HEAD_SLOT))
    return pl.pallas_call(
        _in_qkv_kernel,
        grid=(n // TM_IN,),
        in_specs=[
            row(D_MODEL), full((1, D_MODEL)), full(w_main.shape), full(w_kr.shape), full(w_gate.shape),
            full((1, Q_LORA)), full((Q_LORA, QK_WIDTH)),
            full((1, KV_LORA)), full((KV_LORA, QK_WIDTH)), full((V_WIDTH, KV_LORA)),
            rope, rope, rope, lane_row, lane_row, lane_row, lane_row,
        ],
        out_specs=[row(2 * GM_WIDTH), row(2 * D_MODEL), row(QK_WIDTH), row(QK_WIDTH),
                   pl.BlockSpec((None, None, V_WIDTH, TM_IN), lambda i: (i // tps, i % tps, 0, 0))],
        out_shape=[jax.ShapeDtypeStruct((n, 2 * GM_WIDTH), BF16), jax.ShapeDtypeStruct((n, 2 * D_MODEL), BF16),
                   jax.ShapeDtypeStruct((n, QK_WIDTH), BF16), jax.ShapeDtypeStruct((n, QK_WIDTH), BF16),
                   jax.ShapeDtypeStruct((batch, tps, V_WIDTH, TM_IN), BF16)],
        compiler_params=_cparams(("parallel",)),
        name="in_qkv",
    )(x2, g, w_main, w_kr, w_gate, qg, wq, kvg, wk, wvt, tq, ta, tb, gq, ga, gb, gkn)


def _attention_kernel(q_ref, k_ref, vt_ref, o_ref, *, seq):
    nblk = seq // ATT_T
    vblocks = ATT_T // TM_IN
    key_pos = lax.broadcasted_iota(jnp.int32, (ATT_T, ATT_T), 0)
    qry_pos = lax.broadcasted_iota(jnp.int32, (ATT_T, ATT_T), 1)
    causal = key_pos <= qry_pos
    nt = (((1,), (1,)), ((), ()))
    ones_rows = jnp.ones((ATT_SUM_ROWS, ATT_T), BF16)

    def update(state, g, st, kj, diagonal):
        m, acc = state
        if diagonal:
            st = jnp.where(causal, st, -1e30)
        m_new = jnp.maximum(m, jnp.max(st, axis=0, keepdims=True))
        alpha = jnp.exp2(m - m_new)
        p = jnp.exp2((st - m_new).astype(BF16))
        vt = [vt_ref[kj * vblocks + b, g * V_DIM:(g + 1) * V_DIM, :] for b in range(vblocks)]
        v_aug = jnp.concatenate([jnp.concatenate(vt, axis=1), ones_rows], axis=0)
        return m_new, alpha * acc + jnp.dot(v_aug, p, preferred_element_type=F32)

    for qi in range(nblk):
        q0, q1 = qi * ATT_T, (qi + 1) * ATT_T
        strips = [lax.dot_general(k_ref[0:q1, g * HEAD_SLOT:(g + 1) * HEAD_SLOT],
                                  q_ref[q0:q1, g * HEAD_SLOT:(g + 1) * HEAD_SLOT], nt,
                                  preferred_element_type=F32) for g in range(ATT_G)]
        acc_rows = V_DIM + ATT_SUM_ROWS
        init = (jnp.full((1, ATT_T), -1e30, F32), jnp.zeros((acc_rows, ATT_T), F32))
        states = [init for _ in range(ATT_G)]
        for kj in range(qi + 1):
            states = [update(states[g], g, strips[g][kj * ATT_T:(kj + 1) * ATT_T], kj, kj == qi)
                      for g in range(ATT_G)]
        pad_rows = jnp.zeros((HEAD_SLOT - acc_rows, ATT_T), F32)
        for g in range(ATT_G):
            acc = jnp.concatenate([states[g][1], pad_rows], axis=0).T
            o_ref[q0:q1, g * V_DIM:(g + 1) * V_DIM] = (acc[:, :V_DIM] / acc[:, V_DIM:V_DIM + 1]).astype(BF16)


def _attention(q3, k3, vt4):
    batch, seq, _ = q3.shape
    slot = pl.BlockSpec((None, seq, ATT_G * HEAD_SLOT), lambda b, h: (b, 0, h))
    return pl.pallas_call(
        functools.partial(_attention_kernel, seq=seq),
        grid=(batch, MLA_HEADS // ATT_G),
        in_specs=[slot, slot,
                  pl.BlockSpec((None, seq // TM_IN, ATT_G * V_DIM, TM_IN), lambda b, h: (b, 0, h, 0))],
        out_specs=pl.BlockSpec((None, seq, ATT_G * V_DIM), lambda b, h: (b, 0, h)),
        out_shape=jax.ShapeDtypeStruct((batch, seq, V_WIDTH), BF16),
        compiler_params=_cparams(("parallel", "parallel")),
        name="attention",
    )(q3, k3, vt4)


def _split_bf16(v):
    hi = v.astype(BF16)
    return hi, (v - hi.astype(F32)).astype(BF16)


def _gelu_tanh(x):
    c1 = -2.0 * float(np.sqrt(2.0 / np.pi) * np.log2(np.e))
    c2 = c1 * 0.044715
    return x / (1.0 + jnp.exp2(x * (c1 + c2 * (x * x))))


def _gmlp_tile(uv_ref, vg_ref, vb_ref, ws_ref, bst_ref, a_ref):
    v = uv_ref[:, GM_WIDTH:].astype(F32)
    mu = jnp.mean(v, axis=-1, keepdims=True)
    vc = v - mu
    vn = (vc * lax.rsqrt(jnp.mean(vc * vc, axis=-1, keepdims=True) + EPS) * vg_ref[...] + vb_ref[...]).astype(BF16)
    row = lax.broadcasted_iota(jnp.int32, (GM_CHUNK, GM_CHUNK), 0)
    col = lax.broadcasted_iota(jnp.int32, (GM_CHUNK, GM_CHUNK), 1)
    tril = col <= row
    bst = bst_ref[...]
    for g in range(GM_GROUPS):
        ws = jnp.where(tril, ws_ref[g], 0.0).astype(BF16)
        bias = bst[:, g:g + 1]
        cs = slice(g * GM_GROUP_CH, (g + 1) * GM_GROUP_CH)
        for c in range(TM_MERGE // GM_CHUNK):
            rs = slice(c * GM_CHUNK, (c + 1) * GM_CHUNK)
            mixed = jnp.dot(ws, vn[rs, cs], preferred_element_type=F32) + bias
            a_ref[rs, cs] = (uv_ref[rs, cs].astype(F32) * mixed).astype(BF16)


def _merge_kernel(uv_ref, vg_ref, vb_ref, ws_ref, bst_ref, ot_ref, gate_ref, x_ref, wa_ref, wb_ref, wo_ref,
                  fg_ref, *rest, with_router):
    if with_router:
        wr_ref, xn_ref, lg_ref, a_ref = rest
    else:
        xn_ref, h2_ref, a_ref = rest
    yb = jnp.dot(ot_ref[...], wb_ref[...], preferred_element_type=F32)
    _gmlp_tile(uv_ref, vg_ref, vb_ref, ws_ref, bst_ref, a_ref)
    ya = jnp.dot(a_ref[...], wa_ref[...], preferred_element_type=F32)
    ga = gate_ref[:, :D_MODEL].astype(F32)
    gb = gate_ref[:, D_MODEL:].astype(F32)
    merged = (ga * ya + gb * yb).astype(BF16)
    xn = x_ref[...] + jnp.dot(merged, wo_ref[...], preferred_element_type=F32)
    xn_ref[...] = xn
    h2 = _rms(xn, fg_ref[...])
    if with_router:
        h_hi, h_lo = _split_bf16(h2)
        w_hi, w_lo = _split_bf16(wr_ref[...])
        hh_hl = jnp.dot(h_hi, jnp.concatenate([w_hi, w_lo], axis=1), preferred_element_type=F32)
        lg_ref[...] = hh_hl[:, :LANES] + (hh_hl[:, LANES:] + jnp.dot(h_lo, w_hi, preferred_element_type=F32))
    else:
        h2_ref[...] = h2.astype(BF16)


def _merge(uv, vg, vb, ws, bst, ot, gates, x2, wa, wb, wo, fg, wr):
    n = x2.shape[0]
    with_router = wr is not None
    full = lambda shape: pl.BlockSpec(shape, lambda i: (0,) * len(shape))
    row = lambda w: pl.BlockSpec((TM_MERGE, w), lambda i: (i, 0))
    in_specs = [
        row(2 * GM_WIDTH), full((1, GM_WIDTH)), full((1, GM_WIDTH)),
        full((GM_GROUPS, GM_CHUNK, GM_CHUNK)), full((GM_CHUNK, GM_GROUPS)),
        row(V_WIDTH), row(2 * D_MODEL), row(D_MODEL),
        full((GM_WIDTH, D_MODEL)), full((V_WIDTH, D_MODEL)), full((D_MODEL, D_MODEL)), full((1, D_MODEL)),
    ]
    args = [uv, vg, vb, ws, bst, ot, gates, x2, wa, wb, wo, fg]
    if with_router:
        in_specs.append(full((D_MODEL, LANES)))
        out_specs = [row(D_MODEL), row(LANES)]
        out_shape = [jax.ShapeDtypeStruct((n, D_MODEL), F32), jax.ShapeDtypeStruct((n, LANES), F32)]
        args.append(wr)
    else:
        out_specs = [row(D_MODEL), row(D_MODEL)]
        out_shape = [jax.ShapeDtypeStruct((n, D_MODEL), F32), jax.ShapeDtypeStruct((n, D_MODEL), BF16)]
    return pl.pallas_call(
        functools.partial(_merge_kernel, with_router=with_router),
        grid=(n // TM_MERGE,),
        in_specs=in_specs, out_specs=out_specs, out_shape=out_shape,
        scratch_shapes=[pltpu.VMEM((TM_MERGE, GM_WIDTH), BF16)],
        compiler_params=_cparams(("parallel",)),
        name="merge_router" if with_router else "merge",
    )(*args)


def _round_up_tile(c):
    return (c + (TM_FFN - 1)) & (-TM_FFN)


def _route_kernel(lg_ref, br_ref, imeta_ref, wcol_ref, cnt_ref, run_ref):
    @pl.when(pl.program_id(0) == 0)
    def _():
        run_ref[...] = jnp.zeros_like(run_ref)

    lt = (lg_ref[...] + br_ref[...]).T[:N_EXPERTS, :]
    sub = lax.broadcasted_iota(jnp.int32, lt.shape, 0)
    m1 = jnp.max(lt, axis=0, keepdims=True)
    i1 = jnp.min(jnp.where(lt == m1, sub, N_EXPERTS), axis=0, keepdims=True)
    rest = jnp.where(sub == i1, -jnp.inf, lt)
    m2 = jnp.max(rest, axis=0, keepdims=True)
    i2 = jnp.min(jnp.where(rest == m2, sub, N_EXPERTS), axis=0, keepdims=True)
    e2 = jnp.exp(m2 - m1)
    w1 = 1.0 / (1.0 + e2)
    w2 = e2 / (1.0 + e2)

    onehot = jnp.where((sub == i1) | (sub == i2), 1.0, 0.0)
    src = lax.broadcasted_iota(jnp.int32, (TM_ROUTE, TM_ROUTE), 0)
    dst = lax.broadcasted_iota(jnp.int32, (TM_ROUTE, TM_ROUTE), 1)
    earlier = jnp.where(src < dst, 1.0, 0.0).astype(BF16)
    seen = jnp.dot(onehot.astype(BF16), earlier, preferred_element_type=F32) + run_ref[:, :1]
    r1 = jnp.sum(jnp.where(sub == i1, seen, 0.0), axis=0, keepdims=True).astype(jnp.int32)
    r2 = jnp.sum(jnp.where(sub == i2, seen, 0.0), axis=0, keepdims=True).astype(jnp.int32)
    imeta_ref[...] = jnp.where(sub == 0, i1, jnp.where(sub == 1, i2, jnp.where(sub == 2, r1, jnp.where(sub == 3, r2, 0))))

    total = run_ref[...] + jnp.sum(onehot, axis=1, keepdims=True)
    run_ref[...] = total
    cnt_ref[...] = total.astype(jnp.int32)

    sub_w = lax.broadcasted_iota(jnp.int32, (LANES, TM_ROUTE), 0)
    wcol_ref[...] = jnp.where(sub_w == 0, w1, jnp.where(sub_w == 1, w2, 0.0)).T


def _route(logits, br):
    n = logits.shape[0]
    return pl.pallas_call(
        _route_kernel,
        grid=(n // TM_ROUTE,),
        in_specs=[pl.BlockSpec((TM_ROUTE, LANES), lambda i: (i, 0)),
                  pl.BlockSpec((1, LANES), lambda i: (0, 0))],
        out_specs=[pl.BlockSpec((N_EXPERTS, TM_ROUTE), lambda i: (0, i)),
                   pl.BlockSpec((TM_ROUTE, LANES), lambda i: (i, 0)),
                   pl.BlockSpec((N_EXPERTS, LANES), lambda i: (0, 0))],
        out_shape=[jax.ShapeDtypeStruct((N_EXPERTS, n), jnp.int32),
                   jax.ShapeDtypeStruct((n, LANES), F32),
                   jax.ShapeDtypeStruct((N_EXPERTS, LANES), jnp.int32)],
        scratch_shapes=[pltpu.VMEM((N_EXPERTS, LANES), F32)],
        compiler_params=_cparams(("arbitrary",)),
        name="route",
    )(logits, br)


def _positions_kernel(cnt_ref, imeta_ref, pos_ref):
    im = imeta_ref[...]
    i1, i2, r1, r2 = im[0:1], im[1:2], im[2:3], im[3:4]
    start = jnp.int32(0)
    p1 = jnp.zeros_like(i1)
    p2 = jnp.zeros_like(i2)
    for e in range(N_EXPERTS):
        p1 = jnp.where(i1 == e, start, p1)
        p2 = jnp.where(i2 == e, start, p2)
        start = start + _round_up_tile(cnt_ref[e])
    pos_ref[...] = jnp.concatenate([p1 + r1, p2 + r2], axis=0)


def _positions(cnt, imeta):
    n = imeta.shape[1]
    return pl.pallas_call(
        _positions_kernel,
        grid_spec=pltpu.PrefetchScalarGridSpec(
            num_scalar_prefetch=1,
            grid=(n // TM_ROUTE,),
            in_specs=[pl.BlockSpec((N_EXPERTS, TM_ROUTE), lambda i, c: (0, i))],
            out_specs=pl.BlockSpec((2, TM_ROUTE), lambda i, c: (0, i)),
        ),
        out_shape=jax.ShapeDtypeStruct((2, n), jnp.int32),
        compiler_params=_cparams(("parallel",)),
        name="positions",
    )(cnt, imeta)


def _scatter_kernel(pos_ref, cnt_ref, xn_ref, fg_ref, xs_ref, hbuf, zbuf, sems, zsem, *, n_tokens):
    n_tiles = xs_ref.shape[0] // TM_FFN

    @pl.when(pl.program_id(0) == 0)
    def _():
        zbuf[...] = jnp.zeros_like(zbuf)

        def zero_tile(t):
            rows = pl.ds(pl.multiple_of(t * TM_FFN, TM_FFN), TM_FFN)
            return pltpu.make_async_copy(zbuf, xs_ref.at[rows], zsem)

        def each_zero_tile(action):
            end = jnp.int32(0)
            for e in range(N_EXPERTS):
                tiles = _round_up_tile(cnt_ref[e]) // TM_FFN
                end = end + tiles

                @pl.when(tiles > 0)
                def _(end=end):
                    action(zero_tile(end - 1))

            def tail(t, carry):
                action(zero_tile(t))
                return carry

            lax.fori_loop(end, n_tiles, tail, 0)

        each_zero_tile(lambda copy: copy.start())
        each_zero_tile(lambda copy: copy.wait())

    i = pl.program_id(0)
    last = pl.num_programs(0) - 1
    slot = i % 2

    def drain(s):
        for k in range(2):
            pltpu.make_async_copy(hbuf.at[s], xs_ref.at[pl.ds(0, TM_SCAT)], sems.at[s]).wait()

    @pl.when(i >= 2)
    def _():
        drain(slot)

    base = i * TM_SCAT
    hbuf[slot] = _rms(xn_ref[...], fg_ref[...])

    def issue(r, carry):
        for k in range(2):
            dst = pos_ref[k * n_tokens + base + r]
            pltpu.make_async_copy(hbuf.at[slot, pl.ds(r, 1)], xs_ref.at[pl.ds(dst, 1)], sems.at[slot]).start()
        return carry

    lax.fori_loop(0, TM_SCAT, issue, 0, unroll=8)

    @pl.when(i == last)
    def _():
        drain(1 - slot)
        drain(slot)


def _scatter(pos, cnt, xn, fg):
    n = xn.shape[0]
    assert n // TM_SCAT >= 2
    rows = 2 * n + N_EXPERTS * TM_FFN
    return pl.pallas_call(
        functools.partial(_scatter_kernel, n_tokens=n),
        grid_spec=pltpu.PrefetchScalarGridSpec(
            num_scalar_prefetch=2,
            grid=(n // TM_SCAT,),
            in_specs=[pl.BlockSpec((TM_SCAT, D_MODEL), lambda i, p, c: (i, 0)),
                      pl.BlockSpec((1, D_MODEL), lambda i, p, c: (0, 0))],
            out_specs=pl.BlockSpec(memory_space=pl.ANY),
            scratch_shapes=[pltpu.VMEM((2, TM_SCAT, D_MODEL), F32), pltpu.VMEM((TM_FFN, D_MODEL), F32),
                            pltpu.SemaphoreType.DMA((2,)), pltpu.SemaphoreType.DMA(())],
        ),
        out_shape=jax.ShapeDtypeStruct((rows, D_MODEL), F32),
        compiler_params=_cparams(("arbitrary",)),
        name="moe_scatter",
    )(pos, cnt, xn, fg)


def _combine_kernel(pos_ref, xn_ref, wcol_ref, ys_ref, o_ref, buf, sems, *, n_tokens):
    i = pl.program_id(0)
    slot = i % 2

    def fetch_tile(tile, s):
        base = tile * TM_COMB

        def issue(r, carry):
            for k in range(2):
                src = pos_ref[k * n_tokens + base + r]
                pltpu.make_async_copy(ys_ref.at[pl.ds(src, 1)], buf.at[s, k, pl.ds(r, 1)], sems.at[s]).start()
            return carry

        lax.fori_loop(0, TM_COMB, issue, 0, unroll=8)

    @pl.when(i == 0)
    def _():
        fetch_tile(0, 0)

    @pl.when(i + 1 < pl.num_programs(0))
    def _():
        fetch_tile(i + 1, 1 - slot)

    for k in range(2):
        pltpu.make_async_copy(ys_ref.at[pl.ds(0, TM_COMB)], buf.at[slot, k], sems.at[slot]).wait()
    w = wcol_ref[...]
    o_ref[...] = xn_ref[...] + (w[:, 0:1] * buf[slot, 0] + w[:, 1:2] * buf[slot, 1])


def _combine(pos, xn, wcol, ys):
    n = xn.shape[0]
    return pl.pallas_call(
        functools.partial(_combine_kernel, n_tokens=n),
        grid_spec=pltpu.PrefetchScalarGridSpec(
            num_scalar_prefetch=1,
            grid=(n // TM_COMB,),
            in_specs=[pl.BlockSpec((TM_COMB, D_MODEL), lambda i, p: (i, 0)),
                      pl.BlockSpec((TM_COMB, LANES), lambda i, p: (i, 0)),
                      pl.BlockSpec(memory_space=pl.ANY)],
            out_specs=pl.BlockSpec((TM_COMB, D_MODEL), lambda i, p: (i, 0)),
            scratch_shapes=[pltpu.VMEM((2, 2, TM_COMB, D_MODEL), F32), pltpu.SemaphoreType.DMA((2,))],
        ),
        out_shape=jax.ShapeDtypeStruct((n, D_MODEL), F32),
        compiler_params=_cparams(("arbitrary",)),
        name="moe_combine",
    )(pos, xn, wcol, ys)


def _swiglu(h, wg_ref, wu_ref, wd_ref):
    y = None
    c0 = 0
    for width in FF_CHUNKS:
        cs = slice(c0, c0 + width)
        g = jnp.dot(h, wg_ref[:, cs], preferred_element_type=F32)
        u = jnp.dot(h, wu_ref[:, cs], preferred_element_type=F32)
        act = (g * jax.nn.sigmoid(g) * u).astype(BF16)
        part = jnp.dot(act, wd_ref[cs, :], preferred_element_type=F32)
        y = part if y is None else y + part
        c0 += width
    return y


def _ffn_dense_kernel(h_ref, x_ref, wg_ref, wu_ref, wd_ref, o_ref):
    o_ref[...] = x_ref[...] + _swiglu(h_ref[...], wg_ref, wu_ref, wd_ref)


def _ffn_dense(h2, x2, wgu, wd):
    n = h2.shape[0]
    return pl.pallas_call(
        _ffn_dense_kernel,
        grid=(n // TM_FFN,),
        in_specs=[
            pl.BlockSpec((TM_FFN, D_MODEL), lambda i: (i, 0)),
            pl.BlockSpec((TM_FFN, D_MODEL), lambda i: (i, 0)),
            pl.BlockSpec((D_MODEL, FF), lambda i: (0, 0)),
            pl.BlockSpec((D_MODEL, FF), lambda i: (0, 1)),
            pl.BlockSpec((FF, D_MODEL), lambda i: (0, 0)),
        ],
        out_specs=pl.BlockSpec((TM_FFN, D_MODEL), lambda i: (i, 0)),
        out_shape=jax.ShapeDtypeStruct((n, D_MODEL), F32),
        compiler_params=_cparams(("parallel",)),
        name="ffn_dense",
    )(h2, x2, wgu, wgu, wd)


def _tile_plan(i, cnt_ref):
    end = jnp.int32(0)
    expert = jnp.int32(0)
    for e in range(N_EXPERTS):
        end = end + _round_up_tile(cnt_ref[e])
        expert = expert + (end <= i * TM_FFN).astype(jnp.int32)
    return jnp.minimum(expert, N_EXPERTS - 1), end // TM_FFN


def _ffn_grouped_kernel(cnt_ref, xs_ref, wg_ref, wu_ref, wd_ref, o_ref):
    i = pl.program_id(0)
    _, used = _tile_plan(i, cnt_ref)

    @pl.when(i < used)
    def _():
        o_ref[...] = _swiglu(xs_ref[...].astype(BF16), wg_ref, wu_ref, wd_ref)

    @pl.when(i >= used)
    def _():
        o_ref[...] = jnp.zeros_like(o_ref)


def _ffn_grouped(cnt, xs, wgu, wd):
    rows = xs.shape[0]

    def tile(i, c):
        return jnp.minimum(i, _tile_plan(i, c)[1] - 1)

    return pl.pallas_call(
        _ffn_grouped_kernel,
        grid_spec=pltpu.PrefetchScalarGridSpec(
            num_scalar_prefetch=1,
            grid=(rows // TM_FFN,),
            in_specs=[
                pl.BlockSpec((TM_FFN, D_MODEL), lambda i, c: (tile(i, c), 0)),
                pl.BlockSpec((None, D_MODEL, FF), lambda i, c: (_tile_plan(i, c)[0], 0, 0)),
                pl.BlockSpec((None, D_MODEL, FF), lambda i, c: (_tile_plan(i, c)[0], 0, 1)),
                pl.BlockSpec((None, FF, D_MODEL), lambda i, c: (_tile_plan(i, c)[0], 0, 0)),
            ],
            out_specs=pl.BlockSpec((TM_FFN, D_MODEL), lambda i, c: (i, 0)),
        ),
        out_shape=jax.ShapeDtypeStruct((rows, D_MODEL), F32),
        compiler_params=_cparams(("arbitrary",)),
        name="ffn_grouped",
    )(cnt, xs, wgu, wgu, wd)


def _moe(xn, logits, br, fg, wgu, wd):
    n = xn.shape[0]
    imeta, wcol, cnt_lanes = _route(logits, br)
    cnt = cnt_lanes[:, 0]
    pos = _positions(cnt, imeta).reshape(2 * n)
    xs = _scatter(pos, cnt, xn, fg)
    ys = _ffn_grouped(cnt, xs, wgu, wd)
    return _combine(pos, xn, wcol, ys)


def _rope_tables(seq):
    pos = jnp.arange(seq, dtype=F32)
    inv_freq = ROPE_THETA ** (-jnp.arange(0, QK_ROPE, 2, dtype=F32) / QK_ROPE)
    ang = pos[:, None] * inv_freq[None, :]
    return jnp.cos(ang), jnp.sin(ang)


def _swap_halves(w):
    half = QK_ROPE // 2
    return jnp.concatenate([w[..., half:], w[..., :half]], axis=-1)


def _head_slots(w, width):
    k = w.shape[0]
    w3 = w.reshape(k, MLA_HEADS, width)
    return jnp.pad(w3, ((0, 0), (0, 0), (0, HEAD_SLOT - width))).reshape(k, QK_WIDTH)


def _pad_lanes(v, width=LANES):
    return jnp.pad(v, (0, width - v.shape[0])).reshape(1, width)


def kernel(x, mix_norm_g, w_in, gm_v_norm_g, gm_v_norm_b, gm_w_spatial, gm_b_spatial, gm_w_proj, mla_q_lat_g, mla_w_uq, mla_kv_lat_g, mla_w_ukv, mla_q_norm_g, mla_k_norm_g, mla_w_proj, w_out, ffn_norm_g, dense_w_gu, dense_w_down, moe_w_router, moe_b_router, moe_w_gu, moe_w_down):
    batch, seq, d = x.shape
    n = batch * seq
    depth = w_in.shape[0]
    cos, sin = _rope_tables(seq)
    x2 = x.reshape(n, d)
    q_scale = float(np.log2(np.e) / np.sqrt(QK_DIM))

    o_kr = 2 * GM_WIDTH + Q_LORA + KV_LORA
    o_gate = o_kr + QK_ROPE

    ones_nope, zeros_nope = jnp.ones((seq, QK_NOPE), F32), jnp.zeros((seq, QK_NOPE), F32)
    cc, ss = jnp.concatenate([cos, cos], axis=1), jnp.concatenate([-sin, sin], axis=1)
    tq = jnp.concatenate([ones_nope, cc, ss], axis=1)
    ta = jnp.concatenate([zeros_nope, cc, cc], axis=1)
    tb = jnp.concatenate([zeros_nope, ss, ss], axis=1)

    for l in range(depth):
        wl = w_in[l]
        w_kr = wl[:, o_kr:o_gate]
        w_krx = jnp.pad(jnp.concatenate([w_kr, w_kr], axis=1), ((0, 0), (QK_NOPE, 0)))
        w_krs = jnp.pad(jnp.concatenate([_swap_halves(w_kr)] * 2, axis=1), ((0, 0), (QK_NOPE, 0)))
        w_kr2 = jnp.concatenate([w_krx, w_krs], axis=1).astype(BF16)
        wq3 = mla_w_uq[l].reshape(Q_LORA, MLA_HEADS, QK_DIM)
        wq = jnp.concatenate([wq3, _swap_halves(wq3[:, :, QK_NOPE:])], axis=2).reshape(Q_LORA, QK_WIDTH).astype(BF16)
        wkv3 = mla_w_ukv[l].reshape(KV_LORA, MLA_HEADS, QK_NOPE + V_DIM)
        wk = _head_slots(wkv3[:, :, :QK_NOPE].reshape(KV_LORA, MLA_HEADS * QK_NOPE), QK_NOPE).astype(BF16)
        wvt = wkv3[:, :, QK_NOPE:].reshape(KV_LORA, V_WIDTH).T.astype(BF16)
        gq, gk = mla_q_norm_g[l], mla_k_norm_g[l]
        g12 = lambda g: g[QK_NOPE:]
        g21 = lambda g: _swap_halves(g[QK_NOPE:])
        gq_row = (jnp.concatenate([gq[:QK_NOPE], g12(gq), g21(gq)]) * q_scale).reshape(1, HEAD_SLOT)
        ga_row = _pad_lanes(jnp.concatenate([jnp.zeros((QK_NOPE,), F32), g12(gk), g12(gk)]))
        gb_row = _pad_lanes(jnp.concatenate([jnp.zeros((QK_NOPE,), F32), g21(gk), g21(gk)]))
        uv, gates, q, k, vt = _in_qkv(x2, mix_norm_g[l].reshape(1, d),
                                      wl[:, :o_kr].astype(BF16), w_kr2, wl[:, o_gate:].astype(BF16),
                                      mla_q_lat_g[l].reshape(1, Q_LORA), wq,
                                      mla_kv_lat_g[l].reshape(1, KV_LORA), wk, wvt,
                                      tq, ta, tb, gq_row, ga_row, gb_row, _pad_lanes(gk[:QK_NOPE]), batch, seq)
        slots = lambda t: t.reshape(batch, seq, QK_WIDTH)
        ot = _attention(slots(q), slots(k), vt).reshape(n, V_WIDTH)

        is_moe = l % 2 == 1
        m = l // 2
        wr = jnp.pad(moe_w_router[m], ((0, 0), (0, LANES - N_EXPERTS))) if is_moe else None
        outs = _merge(uv, gm_v_norm_g[l].reshape(1, GM_WIDTH), gm_v_norm_b[l].reshape(1, GM_WIDTH),
                      gm_w_spatial[l], gm_b_spatial[l].T, ot, gates, x2,
                      gm_w_proj[l].astype(BF16), mla_w_proj[l].astype(BF16),
                      w_out[l].astype(BF16), ffn_norm_g[l].reshape(1, d), wr)
        if is_moe:
            xn, logits = outs
            x2 = _moe(xn, logits, _pad_lanes(moe_b_router[m]), ffn_norm_g[l].reshape(1, d),
                      moe_w_gu[m].astype(BF16), moe_w_down[m].astype(BF16))
        else:
            xn, h2 = outs
            x2 = _ffn_dense(h2, xn, dense_w_gu[m].astype(BF16), dense_w_down[m].astype(BF16))
    return x2.reshape(batch, seq, d)
```

```python
import functools

import jax
import jax.numpy as jnp
import numpy as np
from jax import lax
from jax.experimental import pallas as pl
from jax.experimental.pallas import tpu as pltpu

F32 = jnp.float32
BF16 = jnp.bfloat16

EPS = 1e-6
LANES = 128

D_MODEL = 1024
GM_GROUPS = 8
GM_GROUP_CH = 128
GM_WIDTH = 1024
GM_CHUNK = 128
MLA_HEADS = 16
QK_NOPE = 64
QK_ROPE = 32
QK_DIM = 96
V_DIM = 64
Q_LORA = 512
KV_LORA = 256
ROPE_THETA = 10000.0
HEAD_SLOT = LANES
QK_WIDTH = MLA_HEADS * HEAD_SLOT
V_WIDTH = MLA_HEADS * V_DIM
N_EXPERTS = 8
FF = 2816

TM_IN = 256
TM_MERGE = 512
TM_FFN = 512
FF_CHUNKS = (768, 768, 768, 512)
TM_ROUTE = 1024
TM_POS = 2048
TM_SCAT = 512
TM_COMB = 512
ATT_T = 512
ATT_G = 2
ATT_SUM_ROWS = 16

VMEM_LIMIT = 56 * 1024 * 1024


def _cparams(sem):
    return pltpu.CompilerParams(dimension_semantics=sem, vmem_limit_bytes=VMEM_LIMIT)


def _rms(xf, g):
    return xf * lax.rsqrt(jnp.mean(xf * xf, axis=-1, keepdims=True) + EPS) * g


def _in_qkv_kernel(x_ref, g_ref, w_ref, wkr_ref, wgate_ref, qg_ref, wq_ref, kvg_ref, wk_ref, wvt_ref,
                   tq_ref, ta_ref, tb_ref, gq_ref, ga_ref, gb_ref, gkn_ref,
                   uv_ref, gate_ref, q_ref, k_ref, vt_ref):
    h = _rms(x_ref[...], g_ref[...]).astype(BF16)
    o_lat = 2 * GM_WIDTH
    lat = jnp.dot(h, w_ref[:, o_lat:], preferred_element_type=F32)
    krx = jnp.dot(h, wkr_ref[...], preferred_element_type=F32)
    uv_ref[...] = _gelu_tanh(jnp.dot(h, w_ref[:, :o_lat], preferred_element_type=F32)).astype(BF16)
    gate_ref[...] = jax.nn.sigmoid(jnp.dot(h, wgate_ref[...], preferred_element_type=F32)).astype(BF16)

    cq = _rms(lat[:, :Q_LORA], qg_ref[...]).astype(BF16)
    ckv = _rms(lat[:, Q_LORA:], kvg_ref[...]).astype(BF16)
    kr = krx[:, :LANES]
    kr_sw = krx[:, LANES:]
    kr_ssq = 0.5 * jnp.sum(kr * kr, axis=-1, keepdims=True)
    kr_rot = kr * (ta_ref[...] * ga_ref[...]) + kr_sw * (tb_ref[...] * gb_ref[...])
    tq, gkn = tq_ref[...] * gq_ref[...], gkn_ref[...]
    q_all = jnp.dot(cq, wq_ref[...], preferred_element_type=F32)
    k_all = jnp.dot(ckv, wk_ref[...], preferred_element_type=F32)
    first_copy = lax.broadcasted_iota(jnp.int32, (1, HEAD_SLOT), 1) < QK_DIM
    for hd in range(MLA_HEADS):
        sl = slice(hd * HEAD_SLOT, (hd + 1) * HEAD_SLOT)
        qh = q_all[:, sl]
        ssq = jnp.sum(qh * jnp.where(first_copy, qh, 0.0), axis=-1, keepdims=True)
        q_ref[:, sl] = (qh * tq * lax.rsqrt(ssq * (1.0 / QK_DIM) + EPS)).astype(BF16)
        kh = k_all[:, sl]
        rk = lax.rsqrt((jnp.sum(kh * kh, axis=-1, keepdims=True) + kr_ssq) * (1.0 / QK_DIM) + EPS)
        k_ref[:, sl] = ((kh * gkn + kr_rot) * rk).astype(BF16)
    vt_ref[...] = lax.dot_general(wvt_ref[...], ckv, (((1,), (1,)), ((), ())),
                                  preferred_element_type=F32).astype(BF16)


def _in_qkv(x2, g, w_main, w_kr, w_gate, qg, wq, kvg, wk, wvt, tq, ta, tb, gq, ga, gb, gkn, batch, seq):
    n = x2.shape[0]
    tps = seq // TM_IN
    full = lambda shape: pl.BlockSpec(shape, lambda i: (0,) * len(shape), pipeline_mode=pl.Buffered(1))
    rope = pl.BlockSpec((TM_IN, HEAD_SLOT), lambda i: (i % tps, 0))
    row = lambda w: pl.BlockSpec((TM_IN, w), lambda i: (i, 0))
    lane_row = full((1, HEAD_SLOT))
    return pl.pallas_call(
        _in_qkv_kernel,
        grid=(n // TM_IN,),
        in_specs=[
            row(D_MODEL), full((1, D_MODEL)), full(w_main.shape), full(w_kr.shape), full(w_gate.shape),
            full((1, Q_LORA)), full((Q_LORA, QK_WIDTH)),
            full((1, KV_LORA)), full((KV_LORA, QK_WIDTH)), full((V_WIDTH, KV_LORA)),
            rope, rope, rope, lane_row, lane_row, lane_row, lane_row,
        ],
        out_specs=[row(2 * GM_WIDTH), row(2 * D_MODEL), row(QK_WIDTH), row(QK_WIDTH),
                   pl.BlockSpec((None, None, V_WIDTH, TM_IN), lambda i: (i // tps, i % tps, 0, 0))],
        out_shape=[jax.ShapeDtypeStruct((n, 2 * GM_WIDTH), BF16), jax.ShapeDtypeStruct((n, 2 * D_MODEL), BF16),
                   jax.ShapeDtypeStruct((n, QK_WIDTH), BF16), jax.ShapeDtypeStruct((n, QK_WIDTH), BF16),
                   jax.ShapeDtypeStruct((batch, tps, V_WIDTH, TM_IN), BF16)],
        compiler_params=_cparams(("parallel",)),
        name="in_qkv",
    )(x2, g, w_main, w_kr, w_gate, qg, wq, kvg, wk, wvt, tq, ta, tb, gq, ga, gb, gkn)


def _attention_kernel(q_ref, k_ref, vt_ref, o_ref, *, seq):
    nblk = seq // ATT_T
    vblocks = ATT_T // TM_IN
    key_pos = lax.broadcasted_iota(jnp.int32, (ATT_T, ATT_T), 0)
    qry_pos = lax.broadcasted_iota(jnp.int32, (ATT_T, ATT_T), 1)
    causal = key_pos <= qry_pos
    nt = (((1,), (1,)), ((), ()))
    ones_rows = jnp.ones((ATT_SUM_ROWS, ATT_T), BF16)

    def update(state, g, st, kj, diagonal):
        m, acc = state
        if diagonal:
            st = jnp.where(causal, st, -1e30)
        m_new = jnp.maximum(m, jnp.max(st, axis=0, keepdims=True))
        alpha = jnp.exp2(m - m_new)
        p = jnp.exp2((st - m_new).astype(BF16))
        vt = [vt_ref[kj * vblocks + b, g * V_DIM:(g + 1) * V_DIM, :] for b in range(vblocks)]
        v_aug = jnp.concatenate([jnp.concatenate(vt, axis=1), ones_rows], axis=0)
        return m_new, alpha * acc + jnp.dot(v_aug, p, preferred_element_type=F32)

    for qi in range(nblk):
        q0, q1 = qi * ATT_T, (qi + 1) * ATT_T
        strips = [lax.dot_general(k_ref[0:q1, g * HEAD_SLOT:(g + 1) * HEAD_SLOT],
                                  q_ref[q0:q1, g * HEAD_SLOT:(g + 1) * HEAD_SLOT], nt,
                                  preferred_element_type=F32) for g in range(ATT_G)]
        acc_rows = V_DIM + ATT_SUM_ROWS
        init = (jnp.full((1, ATT_T), -1e30, F32), jnp.zeros((acc_rows, ATT_T), F32))
        states = [init for _ in range(ATT_G)]
        for kj in range(qi + 1):
            states = [update(states[g], g, strips[g][kj * ATT_T:(kj + 1) * ATT_T], kj, kj == qi)
                      for g in range(ATT_G)]
        pad_rows = jnp.zeros((HEAD_SLOT - acc_rows, ATT_T), F32)
        for g in range(ATT_G):
            acc = jnp.concatenate([states[g][1], pad_rows], axis=0).T
            o_ref[q0:q1, g * V_DIM:(g + 1) * V_DIM] = (acc[:, :V_DIM] / acc[:, V_DIM:V_DIM + 1]).astype(BF16)


def _attention(q3, k3, vt4):
    batch, seq, _ = q3.shape
    slot = pl.BlockSpec((None, seq, ATT_G * HEAD_SLOT), lambda b, h: (b, 0, h))
    return pl.pallas_call(
        functools.partial(_attention_kernel, seq=seq),
        grid=(batch, MLA_HEADS // ATT_G),
        in_specs=[slot, slot,
                  pl.BlockSpec((None, seq // TM_IN, ATT_G * V_DIM, TM_IN), lambda b, h: (b, 0, h, 0))],
        out_specs=pl.BlockSpec((None, seq, ATT_G * V_DIM), lambda b, h: (b, 0, h)),
        out_shape=jax.ShapeDtypeStruct((batch, seq, V_WIDTH), BF16),
        compiler_params=_cparams(("parallel", "parallel")),
        name="attention",
    )(q3, k3, vt4)


def _split_bf16(v):
    hi = v.astype(BF16)
    return hi, (v - hi.astype(F32)).astype(BF16)


def _gelu_tanh(x):
    c1 = -2.0 * float(np.sqrt(2.0 / np.pi) * np.log2(np.e))
    c2 = c1 * 0.044715
    return x / (1.0 + jnp.exp2(x * (c1 + c2 * (x * x))))


def _gmlp_tile(uv_ref, vg_ref, vb_ref, ws_ref, bst_ref, a_ref):
    v = uv_ref[:, GM_WIDTH:].astype(F32)
    mu = jnp.mean(v, axis=-1, keepdims=True)
    vc = v - mu
    vn = (vc * lax.rsqrt(jnp.mean(vc * vc, axis=-1, keepdims=True) + EPS) * vg_ref[...] + vb_ref[...]).astype(BF16)
    row = lax.broadcasted_iota(jnp.int32, (GM_CHUNK, GM_CHUNK), 0)
    col = lax.broadcasted_iota(jnp.int32, (GM_CHUNK, GM_CHUNK), 1)
    tril = col <= row
    bst = bst_ref[...]
    for g in range(GM_GROUPS):
        ws = jnp.where(tril, ws_ref[g], 0.0).astype(BF16)
        bias = bst[:, g:g + 1]
        cs = slice(g * GM_GROUP_CH, (g + 1) * GM_GROUP_CH)
        for c in range(TM_MERGE // GM_CHUNK):
            rs = slice(c * GM_CHUNK, (c + 1) * GM_CHUNK)
            mixed = jnp.dot(ws, vn[rs, cs], preferred_element_type=F32) + bias
            a_ref[rs, cs] = (uv_ref[rs, cs].astype(F32) * mixed).astype(BF16)


def _merge_kernel(uv_ref, vg_ref, vb_ref, ws_ref, bst_ref, ot_ref, gate_ref, x_ref, wa_ref, wb_ref, wo_ref,
                  fg_ref, *rest, with_router):
    if with_router:
        wr_ref, xn_ref, lg_ref, a_ref = rest
    else:
        xn_ref, h2_ref, a_ref = rest
    yb = jnp.dot(ot_ref[...], wb_ref[...], preferred_element_type=F32)
    _gmlp_tile(uv_ref, vg_ref, vb_ref, ws_ref, bst_ref, a_ref)
    ya = jnp.dot(a_ref[...], wa_ref[...], preferred_element_type=F32)
    ga = gate_ref[:, :D_MODEL].astype(F32)
    gb = gate_ref[:, D_MODEL:].astype(F32)
    merged = (ga * ya + gb * yb).astype(BF16)
    xn = x_ref[...] + jnp.dot(merged, wo_ref[...], preferred_element_type=F32)
    xn_ref[...] = xn
    h2 = _rms(xn, fg_ref[...])
    if with_router:
        h_hi, h_lo = _split_bf16(h2)
        w_hi, w_lo = _split_bf16(wr_ref[...])
        hh_hl = jnp.dot(h_hi, jnp.concatenate([w_hi, w_lo], axis=1), preferred_element_type=F32)
        lg_ref[...] = hh_hl[:, :LANES] + (hh_hl[:, LANES:] + jnp.dot(h_lo, w_hi, preferred_element_type=F32))
    else:
        h2_ref[...] = h2.astype(BF16)


def _merge(uv, vg, vb, ws, bst, ot, gates, x2, wa, wb, wo, fg, wr):
    n = x2.shape[0]
    with_router = wr is not None
    full = lambda shape: pl.BlockSpec(shape, lambda i: (0,) * len(shape))
    row = lambda w: pl.BlockSpec((TM_MERGE, w), lambda i: (i, 0))
    in_specs = [
        row(2 * GM_WIDTH), full((1, GM_WIDTH)), full((1, GM_WIDTH)),
        full((GM_GROUPS, GM_CHUNK, GM_CHUNK)), full((GM_CHUNK, GM_GROUPS)),
        row(V_WIDTH), row(2 * D_MODEL), row(D_MODEL),
        full((GM_WIDTH, D_MODEL)), full((V_WIDTH, D_MODEL)), full((D_MODEL, D_MODEL)), full((1, D_MODEL)),
    ]
    args = [uv, vg, vb, ws, bst, ot, gates, x2, wa, wb, wo, fg]
    if with_router:
        in_specs.append(full((D_MODEL, LANES)))
        out_specs = [row(D_MODEL), row(LANES)]
        out_shape = [jax.ShapeDtypeStruct((n, D_MODEL), F32), jax.ShapeDtypeStruct((n, LANES), F32)]
        args.append(wr)
    else:
        out_specs = [row(D_MODEL), row(D_MODEL)]
        out_shape = [jax.ShapeDtypeStruct((n, D_MODEL), F32), jax.ShapeDtypeStruct((n, D_MODEL), BF16)]
    return pl.pallas_call(
        functools.partial(_merge_kernel, with_router=with_router),
        grid=(n // TM_MERGE,),
        in_specs=in_specs, out_specs=out_specs, out_shape=out_shape,
        scratch_shapes=[pltpu.VMEM((TM_MERGE, GM_WIDTH), BF16)],
        compiler_params=_cparams(("parallel",)),
        name="merge_router" if with_router else "merge",
    )(*args)


def _round_up_tile(c):
    return (c + (TM_FFN - 1)) & (-TM_FFN)


def _route_kernel(lg_ref, br_ref, imeta_ref, wcol_ref, cnt_ref, run_ref):
    @pl.when(pl.program_id(0) == 0)
    def _():
        run_ref[...] = jnp.zeros_like(run_ref)

    lt = (lg_ref[...] + br_ref[...]).T[:N_EXPERTS, :]
    sub = lax.broadcasted_iota(jnp.int32, lt.shape, 0)
    m1 = jnp.max(lt, axis=0, keepdims=True)
    i1 = jnp.min(jnp.where(lt == m1, sub, N_EXPERTS), axis=0, keepdims=True)
    rest = jnp.where(sub == i1, -jnp.inf, lt)
    m2 = jnp.max(rest, axis=0, keepdims=True)
    i2 = jnp.min(jnp.where(rest == m2, sub, N_EXPERTS), axis=0, keepdims=True)
    e2 = jnp.exp(m2 - m1)
    w1 = 1.0 / (1.0 + e2)
    w2 = e2 / (1.0 + e2)

    onehot = jnp.where((sub == i1) | (sub == i2), 1.0, 0.0)
    src = lax.broadcasted_iota(jnp.int32, (TM_ROUTE, TM_ROUTE), 0)
    dst = lax.broadcasted_iota(jnp.int32, (TM_ROUTE, TM_ROUTE), 1)
    earlier = jnp.where(src < dst, 1.0, 0.0).astype(BF16)
    seen = jnp.dot(onehot.astype(BF16), earlier, preferred_element_type=F32) + run_ref[:, :1]
    r1 = jnp.sum(jnp.where(sub == i1, seen, 0.0), axis=0, keepdims=True).astype(jnp.int32)
    r2 = jnp.sum(jnp.where(sub == i2, seen, 0.0), axis=0, keepdims=True).astype(jnp.int32)
    imeta_ref[...] = jnp.where(sub == 0, i1, jnp.where(sub == 1, i2, jnp.where(sub == 2, r1, jnp.where(sub == 3, r2, 0))))

    total = run_ref[...] + jnp.sum(onehot, axis=1, keepdims=True)
    run_ref[...] = total
    cnt_ref[...] = total.astype(jnp.int32)

    sub_w = lax.broadcasted_iota(jnp.int32, (LANES, TM_ROUTE), 0)
    wcol_ref[...] = jnp.where(sub_w == 0, w1, jnp.where(sub_w == 1, w2, 0.0)).T


def _route(logits, br):
    n = logits.shape[0]
    return pl.pallas_call(
        _route_kernel,
        grid=(n // TM_ROUTE,),
        in_specs=[pl.BlockSpec((TM_ROUTE, LANES), lambda i: (i, 0)),
                  pl.BlockSpec((1, LANES), lambda i: (0, 0))],
        out_specs=[pl.BlockSpec((N_EXPERTS, TM_ROUTE), lambda i: (0, i)),
                   pl.BlockSpec((TM_ROUTE, LANES), lambda i: (i, 0)),
                   pl.BlockSpec((N_EXPERTS, LANES), lambda i: (0, 0))],
        out_shape=[jax.ShapeDtypeStruct((N_EXPERTS, n), jnp.int32),
                   jax.ShapeDtypeStruct((n, LANES), F32),
                   jax.ShapeDtypeStruct((N_EXPERTS, LANES), jnp.int32)],
        scratch_shapes=[pltpu.VMEM((N_EXPERTS, LANES), F32)],
        compiler_params=_cparams(("arbitrary",)),
        name="route",
    )(logits, br)


def _positions_kernel(cnt_ref, imeta_ref, pos_ref):
    im = imeta_ref[...]
    i1, i2, r1, r2 = im[0:1], im[1:2], im[2:3], im[3:4]
    start = jnp.int32(0)
    p1 = jnp.zeros_like(i1)
    p2 = jnp.zeros_like(i2)
    for e in range(N_EXPERTS):
        p1 = jnp.where(i1 == e, start, p1)
        p2 = jnp.where(i2 == e, start, p2)
        start = start + _round_up_tile(cnt_ref[e])
    pos_ref[...] = jnp.concatenate([p1 + r1, p2 + r2], axis=0)


def _positions(cnt, imeta):
    n = imeta.shape[1]
    return pl.pallas_call(
        _positions_kernel,
        grid_spec=pltpu.PrefetchScalarGridSpec(
            num_scalar_prefetch=1,
            grid=(n // TM_POS,),
            in_specs=[pl.BlockSpec((N_EXPERTS, TM_POS), lambda i, c: (0, i))],
            out_specs=pl.BlockSpec((2, TM_POS), lambda i, c: (0, i)),
        ),
        out_shape=jax.ShapeDtypeStruct((2, n), jnp.int32),
        compiler_params=_cparams(("parallel",)),
        name="positions",
    )(cnt, imeta)


def _scatter_kernel(pos_ref, cnt_ref, xn_ref, fg_ref, xs_ref, hbuf, zbuf, sems, zsem, *, n_tokens):
    n_tiles = xs_ref.shape[0] // TM_FFN

    @pl.when(pl.program_id(0) == 0)
    def _():
        zbuf[...] = jnp.zeros_like(zbuf)

        def zero_tile(t):
            rows = pl.ds(pl.multiple_of(t * TM_FFN, TM_FFN), TM_FFN)
            return pltpu.make_async_copy(zbuf, xs_ref.at[rows], zsem)

        def each_zero_tile(action):
            end = jnp.int32(0)
            for e in range(N_EXPERTS):
                tiles = _round_up_tile(cnt_ref[e]) // TM_FFN
                end = end + tiles

                @pl.when(tiles > 0)
                def _(end=end):
                    action(zero_tile(end - 1))

            def tail(t, carry):
                action(zero_tile(t))
                return carry

            lax.fori_loop(end, n_tiles, tail, 0)

        each_zero_tile(lambda copy: copy.start())
        each_zero_tile(lambda copy: copy.wait())

    i = pl.program_id(0)
    last = pl.num_programs(0) - 1
    slot = i % 2

    def drain(s):
        for k in range(2):
            pltpu.make_async_copy(hbuf.at[s], xs_ref.at[pl.ds(0, TM_SCAT)], sems.at[s]).wait()

    @pl.when(i >= 2)
    def _():
        drain(slot)

    base = i * TM_SCAT
    hbuf[slot] = _rms(xn_ref[...], fg_ref[...])

    def issue(r, carry):
        for k in range(2):
            dst = pos_ref[k * n_tokens + base + r]
            pltpu.make_async_copy(hbuf.at[slot, pl.ds(r, 1)], xs_ref.at[pl.ds(dst, 1)], sems.at[slot]).start()
        return carry

    lax.fori_loop(0, TM_SCAT, issue, 0, unroll=8)

    @pl.when(i == last)
    def _():
        drain(1 - slot)
        drain(slot)


def _scatter(pos, cnt, xn, fg):
    n = xn.shape[0]
    assert n // TM_SCAT >= 2
    rows = 2 * n + N_EXPERTS * TM_FFN
    return pl.pallas_call(
        functools.partial(_scatter_kernel, n_tokens=n),
        grid_spec=pltpu.PrefetchScalarGridSpec(
            num_scalar_prefetch=2,
            grid=(n // TM_SCAT,),
            in_specs=[pl.BlockSpec((TM_SCAT, D_MODEL), lambda i, p, c: (i, 0)),
                      pl.BlockSpec((1, D_MODEL), lambda i, p, c: (0, 0))],
            out_specs=pl.BlockSpec(memory_space=pl.ANY),
            scratch_shapes=[pltpu.VMEM((2, TM_SCAT, D_MODEL), F32), pltpu.VMEM((TM_FFN, D_MODEL), F32),
                            pltpu.SemaphoreType.DMA((2,)), pltpu.SemaphoreType.DMA(())],
        ),
        out_shape=jax.ShapeDtypeStruct((rows, D_MODEL), F32),
        compiler_params=_cparams(("arbitrary",)),
        name="moe_scatter",
    )(pos, cnt, xn, fg)


def _combine_kernel(pos_ref, xn_ref, wcol_ref, ys_ref, o_ref, buf, sems, *, n_tokens):
    i = pl.program_id(0)
    slot = i % 2

    def fetch_tile(tile, s):
        base = tile * TM_COMB

        def issue(r, carry):
            for k in range(2):
                src = pos_ref[k * n_tokens + base + r]
                pltpu.make_async_copy(ys_ref.at[pl.ds(src, 1)], buf.at[s, k, pl.ds(r, 1)], sems.at[s]).start()
            return carry

        lax.fori_loop(0, TM_COMB, issue, 0, unroll=8)

    @pl.when(i == 0)
    def _():
        fetch_tile(0, 0)

    @pl.when(i + 1 < pl.num_programs(0))
    def _():
        fetch_tile(i + 1, 1 - slot)

    for k in range(2):
        pltpu.make_async_copy(ys_ref.at[pl.ds(0, TM_COMB)], buf.at[slot, k], sems.at[slot]).wait()
    w = wcol_ref[...]
    o_ref[...] = xn_ref[...] + (w[:, 0:1] * buf[slot, 0] + w[:, 1:2] * buf[slot, 1])


def _combine(pos, xn, wcol, ys):
    n = xn.shape[0]
    return pl.pallas_call(
        functools.partial(_combine_kernel, n_tokens=n),
        grid_spec=pltpu.PrefetchScalarGridSpec(
            num_scalar_prefetch=1,
            grid=(n // TM_COMB,),
            in_specs=[pl.BlockSpec((TM_COMB, D_MODEL), lambda i, p: (i, 0)),
                      pl.BlockSpec((TM_COMB, LANES), lambda i, p: (i, 0)),
                      pl.BlockSpec(memory_space=pl.ANY)],
            out_specs=pl.BlockSpec((TM_COMB, D_MODEL), lambda i, p: (i, 0)),
            scratch_shapes=[pltpu.VMEM((2, 2, TM_COMB, D_MODEL), F32), pltpu.SemaphoreType.DMA((2,))],
        ),
        out_shape=jax.ShapeDtypeStruct((n, D_MODEL), F32),
        compiler_params=_cparams(("arbitrary",)),
        name="moe_combine",
    )(pos, xn, wcol, ys)


def _swiglu(h, wg_ref, wu_ref, wd_ref):
    y = None
    c0 = 0
    for width in FF_CHUNKS:
        cs = slice(c0, c0 + width)
        g = jnp.dot(h, wg_ref[:, cs], preferred_element_type=F32)
        u = jnp.dot(h, wu_ref[:, cs], preferred_element_type=F32)
        act = (g * jax.nn.sigmoid(g) * u).astype(BF16)
        part = jnp.dot(act, wd_ref[cs, :], preferred_element_type=F32)
        y = part if y is None else y + part
        c0 += width
    return y


def _ffn_dense_kernel(h_ref, x_ref, wg_ref, wu_ref, wd_ref, o_ref):
    o_ref[...] = x_ref[...] + _swiglu(h_ref[...], wg_ref, wu_ref, wd_ref)


def _ffn_dense(h2, x2, wgu, wd):
    n = h2.shape[0]
    return pl.pallas_call(
        _ffn_dense_kernel,
        grid=(n // TM_FFN,),
        in_specs=[
            pl.BlockSpec((TM_FFN, D_MODEL), lambda i: (i, 0)),
            pl.BlockSpec((TM_FFN, D_MODEL), lambda i: (i, 0)),
            pl.BlockSpec((D_MODEL, FF), lambda i: (0, 0)),
            pl.BlockSpec((D_MODEL, FF), lambda i: (0, 1)),
            pl.BlockSpec((FF, D_MODEL), lambda i: (0, 0)),
        ],
        out_specs=pl.BlockSpec((TM_FFN, D_MODEL), lambda i: (i, 0)),
        out_shape=jax.ShapeDtypeStruct((n, D_MODEL), F32),
        compiler_params=_cparams(("parallel",)),
        name="ffn_dense",
    )(h2, x2, wgu, wgu, wd)


def _tile_plan(i, cnt_ref):
    end = jnp.int32(0)
    expert = jnp.int32(0)
    for e in range(N_EXPERTS):
        end = end + _round_up_tile(cnt_ref[e])
        expert = expert + (end <= i * TM_FFN).astype(jnp.int32)
    return jnp.minimum(expert, N_EXPERTS - 1), end // TM_FFN


def _ffn_grouped_kernel(cnt_ref, xs_ref, wg_ref, wu_ref, wd_ref, o_ref):
    i = pl.program_id(0)
    _, used = _tile_plan(i, cnt_ref)

    @pl.when(i < used)
    def _():
        o_ref[...] = _swiglu(xs_ref[...].astype(BF16), wg_ref, wu_ref, wd_ref)

    @pl.when(i >= used)
    def _():
        o_ref[...] = jnp.zeros_like(o_ref)


def _ffn_grouped(cnt, xs, wgu, wd):
    rows = xs.shape[0]

    def tile(i, c):
        return jnp.minimum(i, _tile_plan(i, c)[1] - 1)

    return pl.pallas_call(
        _ffn_grouped_kernel,
        grid_spec=pltpu.PrefetchScalarGridSpec(
            num_scalar_prefetch=1,
            grid=(rows // TM_FFN,),
            in_specs=[
                pl.BlockSpec((TM_FFN, D_MODEL), lambda i, c: (tile(i, c), 0)),
                pl.BlockSpec((None, D_MODEL, FF), lambda i, c: (_tile_plan(i, c)[0], 0, 0)),
                pl.BlockSpec((None, D_MODEL, FF), lambda i, c: (_tile_plan(i, c)[0], 0, 1)),
                pl.BlockSpec((None, FF, D_MODEL), lambda i, c: (_tile_plan(i, c)[0], 0, 0)),
            ],
            out_specs=pl.BlockSpec((TM_FFN, D_MODEL), lambda i, c: (i, 0)),
        ),
        out_shape=jax.ShapeDtypeStruct((rows, D_MODEL), F32),
        compiler_params=_cparams(("arbitrary",)),
        name="ffn_grouped",
    )(cnt, xs, wgu, wgu, wd)


def _moe(xn, logits, br, fg, wgu, wd):
    n = xn.shape[0]
    imeta, wcol, cnt_lanes = _route(logits, br)
    cnt = cnt_lanes[:, 0]
    pos = _positions(cnt, imeta).reshape(2 * n)
    xs = _scatter(pos, cnt, xn, fg)
    ys = _ffn_grouped(cnt, xs, wgu, wd)
    return _combine(pos, xn, wcol, ys)


def _rope_tables(seq):
    pos = jnp.arange(seq, dtype=F32)
    inv_freq = ROPE_THETA ** (-jnp.arange(0, QK_ROPE, 2, dtype=F32) / QK_ROPE)
    ang = pos[:, None] * inv_freq[None, :]
    return jnp.cos(ang), jnp.sin(ang)


def _swap_halves(w):
    half = QK_ROPE // 2
    return jnp.concatenate([w[..., half:], w[..., :half]], axis=-1)


def _head_slots(w, width):
    k = w.shape[0]
    w3 = w.reshape(k, MLA_HEADS, width)
    return jnp.pad(w3, ((0, 0), (0, 0), (0, HEAD_SLOT - width))).reshape(k, QK_WIDTH)


def _pad_lanes(v, width=LANES):
    return jnp.pad(v, (0, width - v.shape[0])).reshape(1, width)


def kernel(x, mix_norm_g, w_in, gm_v_norm_g, gm_v_norm_b, gm_w_spatial, gm_b_spatial, gm_w_proj, mla_q_lat_g, mla_w_uq, mla_kv_lat_g, mla_w_ukv, mla_q_norm_g, mla_k_norm_g, mla_w_proj, w_out, ffn_norm_g, dense_w_gu, dense_w_down, moe_w_router, moe_b_router, moe_w_gu, moe_w_down):
    batch, seq, d = x.shape
    n = batch * seq
    depth = w_in.shape[0]
    cos, sin = _rope_tables(seq)
    x2 = x.reshape(n, d)
    q_scale = float(np.log2(np.e) / np.sqrt(QK_DIM))

    o_kr = 2 * GM_WIDTH + Q_LORA + KV_LORA
    o_gate = o_kr + QK_ROPE

    ones_nope, zeros_nope = jnp.ones((seq, QK_NOPE), F32), jnp.zeros((seq, QK_NOPE), F32)
    cc, ss = jnp.concatenate([cos, cos], axis=1), jnp.concatenate([-sin, sin], axis=1)
    tq = jnp.concatenate([ones_nope, cc, ss], axis=1)
    ta = jnp.concatenate([zeros_nope, cc, cc], axis=1)
    tb = jnp.concatenate([zeros_nope, ss, ss], axis=1)

    for l in range(depth):
        wl = w_in[l]
        w_kr = wl[:, o_kr:o_gate]
        w_krx = jnp.pad(jnp.concatenate([w_kr, w_kr], axis=1), ((0, 0), (QK_NOPE, 0)))
        w_krs = jnp.pad(jnp.concatenate([_swap_halves(w_kr)] * 2, axis=1), ((0, 0), (QK_NOPE, 0)))
        w_kr2 = jnp.concatenate([w_krx, w_krs], axis=1).astype(BF16)
        wq3 = mla_w_uq[l].reshape(Q_LORA, MLA_HEADS, QK_DIM)
        wq = jnp.concatenate([wq3, _swap_halves(wq3[:, :, QK_NOPE:])], axis=2).reshape(Q_LORA, QK_WIDTH).astype(BF16)
        wkv3 = mla_w_ukv[l].reshape(KV_LORA, MLA_HEADS, QK_NOPE + V_DIM)
        wk = _head_slots(wkv3[:, :, :QK_NOPE].reshape(KV_LORA, MLA_HEADS * QK_NOPE), QK_NOPE).astype(BF16)
        wvt = wkv3[:, :, QK_NOPE:].reshape(KV_LORA, V_WIDTH).T.astype(BF16)
        gq, gk = mla_q_norm_g[l], mla_k_norm_g[l]
        g12 = lambda g: g[QK_NOPE:]
        g21 = lambda g: _swap_halves(g[QK_NOPE:])
        gq_row = (jnp.concatenate([gq[:QK_NOPE], g12(gq), g21(gq)]) * q_scale).reshape(1, HEAD_SLOT)
        ga_row = _pad_lanes(jnp.concatenate([jnp.zeros((QK_NOPE,), F32), g12(gk), g12(gk)]))
        gb_row = _pad_lanes(jnp.concatenate([jnp.zeros((QK_NOPE,), F32), g21(gk), g21(gk)]))
        uv, gates, q, k, vt = _in_qkv(x2, mix_norm_g[l].reshape(1, d),
                                      wl[:, :o_kr].astype(BF16), w_kr2, wl[:, o_gate:].astype(BF16),
                                      mla_q_lat_g[l].reshape(1, Q_LORA), wq,
                                      mla_kv_lat_g[l].reshape(1, KV_LORA), wk, wvt,
                                      tq, ta, tb, gq_row, ga_row, gb_row, _pad_lanes(gk[:QK_NOPE]), batch, seq)
        slots = lambda t: t.reshape(batch, seq, QK_WIDTH)
        ot = _attention(slots(q), slots(k), vt).reshape(n, V_WIDTH)

        is_moe = l % 2 == 1
        m = l // 2
        wr = jnp.pad(moe_w_router[m], ((0, 0), (0, LANES - N_EXPERTS))) if is_moe else None
        outs = _merge(uv, gm_v_norm_g[l].reshape(1, GM_WIDTH), gm_v_norm_b[l].reshape(1, GM_WIDTH),
                      gm_w_spatial[l], gm_b_spatial[l].T, ot, gates, x2,
                      gm_w_proj[l].astype(BF16), mla_w_proj[l].astype(BF16),
                      w_out[l].astype(BF16), ffn_norm_g[l].reshape(1, d), wr)
        if is_moe:
            xn, logits = outs
            x2 = _moe(xn, logits, _pad_lanes(moe_b_router[m]), ffn_norm_g[l].reshape(1, d),
                      moe_w_gu[m].astype(BF16), moe_w_down[m].astype(BF16))
        else:
            xn, h2 = outs
            x2 = _ffn_dense(h2, xn, dense_w_gu[m].astype(BF16), dense_w_down[m].astype(BF16))
    return x2.reshape(batch, seq, d)
```

```python
import functools

import jax
import jax.numpy as jnp
import numpy as np
from jax import lax
from jax.experimental import pallas as pl
from jax.experimental.pallas import tpu as pltpu

F32 = jnp.float32
BF16 = jnp.bfloat16

EPS = 1e-6
LANES = 128

D_MODEL = 1024
GM_GROUPS = 8
GM_GROUP_CH = 128
GM_WIDTH = 1024
GM_CHUNK = 128
MLA_HEADS = 16
QK_NOPE = 64
QK_ROPE = 32
QK_DIM = 96
V_DIM = 64
Q_LORA = 512
KV_LORA = 256
ROPE_THETA = 10000.0
HEAD_SLOT = LANES
QK_WIDTH = MLA_HEADS * HEAD_SLOT
V_WIDTH = MLA_HEADS * V_DIM
N_EXPERTS = 8
FF = 2816

TM_IN = 512
TM_MERGE = 512
TM_FFN = 512
FF_CHUNKS = (768, 768, 768, 512)
TM_ROUTE = 1024
TM_POS = 2048
TM_SCAT = 512
TM_COMB = 512
ATT_T = 512
ATT_G = 2
ATT_SUM_ROWS = 16

VMEM_LIMIT = 56 * 1024 * 1024


def _cparams(sem):
    return pltpu.CompilerParams(dimension_semantics=sem, vmem_limit_bytes=VMEM_LIMIT)


def _rms(xf, g):
    return xf * lax.rsqrt(jnp.mean(xf * xf, axis=-1, keepdims=True) + EPS) * g


def _in_qkv_kernel(x_ref, g_ref, w_ref, wkr_ref, wgate_ref, qg_ref, wq_ref, kvg_ref, wk_ref, wvt_ref,
                   tq_ref, ta_ref, tb_ref, gq_ref, ga_ref, gb_ref, gkn_ref,
                   uv_ref, gate_ref, q_ref, k_ref, vt_ref):
    h = _rms(x_ref[...], g_ref[...]).astype(BF16)
    o_lat = 2 * GM_WIDTH
    lat = jnp.dot(h, w_ref[:, o_lat:], preferred_element_type=F32)
    krx = jnp.dot(h, wkr_ref[...], preferred_element_type=F32)
    uv_ref[...] = _gelu_tanh(jnp.dot(h, w_ref[:, :o_lat], preferred_element_type=F32)).astype(BF16)
    gate_ref[...] = jax.nn.sigmoid(jnp.dot(h, wgate_ref[...], preferred_element_type=F32)).astype(BF16)

    cq = _rms(lat[:, :Q_LORA], qg_ref[...]).astype(BF16)
    ckv = _rms(lat[:, Q_LORA:], kvg_ref[...]).astype(BF16)
    kr = krx[:, :LANES]
    kr_sw = krx[:, LANES:]
    kr_ssq = 0.5 * jnp.sum(kr * kr, axis=-1, keepdims=True)
    kr_rot = kr * (ta_ref[...] * ga_ref[...]) + kr_sw * (tb_ref[...] * gb_ref[...])
    tq, gkn = tq_ref[...] * gq_ref[...], gkn_ref[...]
    q_all = jnp.dot(cq, wq_ref[...], preferred_element_type=F32)
    k_all = jnp.dot(ckv, wk_ref[...], preferred_element_type=F32)
    first_copy = lax.broadcasted_iota(jnp.int32, (1, HEAD_SLOT), 1) < QK_DIM
    for hd in range(MLA_HEADS):
        sl = slice(hd * HEAD_SLOT, (hd + 1) * HEAD_SLOT)
        qh = q_all[:, sl]
        ssq = jnp.sum(qh * jnp.where(first_copy, qh, 0.0), axis=-1, keepdims=True)
        q_ref[:, sl] = (qh * tq * lax.rsqrt(ssq * (1.0 / QK_DIM) + EPS)).astype(BF16)
        kh = k_all[:, sl]
        rk = lax.rsqrt((jnp.sum(kh * kh, axis=-1, keepdims=True) + kr_ssq) * (1.0 / QK_DIM) + EPS)
        k_ref[:, sl] = ((kh * gkn + kr_rot) * rk).astype(BF16)
    vt_ref[...] = lax.dot_general(wvt_ref[...], ckv, (((1,), (1,)), ((), ())),
                                  preferred_element_type=F32).astype(BF16)


def _in_qkv(x2, g, w_main, w_kr, w_gate, qg, wq, kvg, wk, wvt, tq, ta, tb, gq, ga, gb, gkn, batch, seq):
    n = x2.shape[0]
    tps = seq // TM_IN
    full = lambda shape: pl.BlockSpec(shape, lambda i: (0,) * len(shape), pipeline_mode=pl.Buffered(1))
    rope = pl.BlockSpec((TM_IN, HEAD_SLOT), lambda i: (i % tps, 0))
    row = lambda w: pl.BlockSpec((TM_IN, w), lambda i: (i, 0))
    lane_row = full((1, HEAD_SLOT))
    return pl.pallas_call(
        _in_qkv_kernel,
        grid=(n // TM_IN,),
        in_specs=[
            row(D_MODEL), full((1, D_MODEL)), full(w_main.shape), full(w_kr.shape), full(w_gate.shape),
            full((1, Q_LORA)), full((Q_LORA, QK_WIDTH)),
            full((1, KV_LORA)), full((KV_LORA, QK_WIDTH)), full((V_WIDTH, KV_LORA)),
            rope, rope, rope, lane_row, lane_row, lane_row, lane_row,
        ],
        out_specs=[row(2 * GM_WIDTH), row(2 * D_MODEL), row(QK_WIDTH), row(QK_WIDTH),
                   pl.BlockSpec((None, None, V_WIDTH, TM_IN), lambda i: (i // tps, i % tps, 0, 0))],
        out_shape=[jax.ShapeDtypeStruct((n, 2 * GM_WIDTH), BF16), jax.ShapeDtypeStruct((n, 2 * D_MODEL), BF16),
                   jax.ShapeDtypeStruct((n, QK_WIDTH), BF16), jax.ShapeDtypeStruct((n, QK_WIDTH), BF16),
                   jax.ShapeDtypeStruct((batch, tps, V_WIDTH, TM_IN), BF16)],
        compiler_params=_cparams(("parallel",)),
        name="in_qkv",
    )(x2, g, w_main, w_kr, w_gate, qg, wq, kvg, wk, wvt, tq, ta, tb, gq, ga, gb, gkn)


def _attention_kernel(q_ref, k_ref, vt_ref, o_ref, *, seq):
    nblk = seq // ATT_T
    vblocks = ATT_T // TM_IN
    key_pos = lax.broadcasted_iota(jnp.int32, (ATT_T, ATT_T), 0)
    qry_pos = lax.broadcasted_iota(jnp.int32, (ATT_T, ATT_T), 1)
    causal = key_pos <= qry_pos
    nt = (((1,), (1,)), ((), ()))
    ones_rows = jnp.ones((ATT_SUM_ROWS, ATT_T), BF16)

    def update(state, g, st, kj, diagonal):
        m, acc = state
        if diagonal:
            st = jnp.where(causal, st, -1e30)
        m_new = jnp.maximum(m, jnp.max(st, axis=0, keepdims=True))
        alpha = jnp.exp2(m - m_new)
        p = jnp.exp2((st - m_new).astype(BF16))
        vt = [vt_ref[kj * vblocks + b, g * V_DIM:(g + 1) * V_DIM, :] for b in range(vblocks)]
        v_aug = jnp.concatenate([jnp.concatenate(vt, axis=1), ones_rows], axis=0)
        return m_new, alpha * acc + jnp.dot(v_aug, p, preferred_element_type=F32)

    for qi in range(nblk):
        q0, q1 = qi * ATT_T, (qi + 1) * ATT_T
        strips = [lax.dot_general(k_ref[0:q1, g * HEAD_SLOT:(g + 1) * HEAD_SLOT],
                                  q_ref[q0:q1, g * HEAD_SLOT:(g + 1) * HEAD_SLOT], nt,
                                  preferred_element_type=F32) for g in range(ATT_G)]
        acc_rows = V_DIM + ATT_SUM_ROWS
        init = (jnp.full((1, ATT_T), -1e30, F32), jnp.zeros((acc_rows, ATT_T), F32))
        states = [init for _ in range(ATT_G)]
        for kj in range(qi + 1):
            states = [update(states[g], g, strips[g][kj * ATT_T:(kj + 1) * ATT_T], kj, kj == qi)
                      for g in range(ATT_G)]
        pad_rows = jnp.zeros((HEAD_SLOT - acc_rows, ATT_T), F32)
        for g in range(ATT_G):
            acc = jnp.concatenate([states[g][1], pad_rows], axis=0).T
            o_ref[q0:q1, g * V_DIM:(g + 1) * V_DIM] = (acc[:, :V_DIM] / acc[:, V_DIM:V_DIM + 1]).astype(BF16)


def _attention(q3, k3, vt4):
    batch, seq, _ = q3.shape
    slot = pl.BlockSpec((None, seq, ATT_G * HEAD_SLOT), lambda b, h: (b, 0, h))
    return pl.pallas_call(
        functools.partial(_attention_kernel, seq=seq),
        grid=(batch, MLA_HEADS // ATT_G),
        in_specs=[slot, slot,
                  pl.BlockSpec((None, seq // TM_IN, ATT_G * V_DIM, TM_IN), lambda b, h: (b, 0, h, 0))],
        out_specs=pl.BlockSpec((None, seq, ATT_G * V_DIM), lambda b, h: (b, 0, h)),
        out_shape=jax.ShapeDtypeStruct((batch, seq, V_WIDTH), BF16),
        compiler_params=_cparams(("parallel", "parallel")),
        name="attention",
    )(q3, k3, vt4)


def _split_bf16(v):
    hi = v.astype(BF16)
    return hi, (v - hi.astype(F32)).astype(BF16)


def _gelu_tanh(x):
    c1 = -2.0 * float(np.sqrt(2.0 / np.pi) * np.log2(np.e))
    c2 = c1 * 0.044715
    return x / (1.0 + jnp.exp2(x * (c1 + c2 * (x * x))))


def _gmlp_tile(uv_ref, vg_ref, vb_ref, ws_ref, bst_ref, a_ref):
    v = uv_ref[:, GM_WIDTH:].astype(F32)
    mu = jnp.mean(v, axis=-1, keepdims=True)
    vc = v - mu
    vn = (vc * lax.rsqrt(jnp.mean(vc * vc, axis=-1, keepdims=True) + EPS) * vg_ref[...] + vb_ref[...]).astype(BF16)
    row = lax.broadcasted_iota(jnp.int32, (GM_CHUNK, GM_CHUNK), 0)
    col = lax.broadcasted_iota(jnp.int32, (GM_CHUNK, GM_CHUNK), 1)
    tril = col <= row
    bst = bst_ref[...]
    for g in range(GM_GROUPS):
        ws = jnp.where(tril, ws_ref[g], 0.0).astype(BF16)
        bias = bst[:, g:g + 1]
        cs = slice(g * GM_GROUP_CH, (g + 1) * GM_GROUP_CH)
        for c in range(TM_MERGE // GM_CHUNK):
            rs = slice(c * GM_CHUNK, (c + 1) * GM_CHUNK)
            mixed = jnp.dot(ws, vn[rs, cs], preferred_element_type=F32) + bias
            a_ref[rs, cs] = (uv_ref[rs, cs].astype(F32) * mixed).astype(BF16)


def _merge_kernel(uv_ref, vg_ref, vb_ref, ws_ref, bst_ref, ot_ref, gate_ref, x_ref, wa_ref, wb_ref, wo_ref,
                  fg_ref, *rest, with_router):
    if with_router:
        wr_ref, xn_ref, lg_ref, a_ref = rest
    else:
        xn_ref, h2_ref, a_ref = rest
    yb = jnp.dot(ot_ref[...], wb_ref[...], preferred_element_type=F32)
    _gmlp_tile(uv_ref, vg_ref, vb_ref, ws_ref, bst_ref, a_ref)
    ya = jnp.dot(a_ref[...], wa_ref[...], preferred_element_type=F32)
    ga = gate_ref[:, :D_MODEL].astype(F32)
    gb = gate_ref[:, D_MODEL:].astype(F32)
    merged = (ga * ya + gb * yb).astype(BF16)
    xn = x_ref[...] + jnp.dot(merged, wo_ref[...], preferred_element_type=F32)
    xn_ref[...] = xn
    h2 = _rms(xn, fg_ref[...])
    if with_router:
        h_hi, h_lo = _split_bf16(h2)
        w_hi, w_lo = _split_bf16(wr_ref[...])
        hh_hl = jnp.dot(h_hi, jnp.concatenate([w_hi, w_lo], axis=1), preferred_element_type=F32)
        lg_ref[...] = hh_hl[:, :LANES] + (hh_hl[:, LANES:] + jnp.dot(h_lo, w_hi, preferred_element_type=F32))
    else:
        h2_ref[...] = h2.astype(BF16)


def _merge(uv, vg, vb, ws, bst, ot, gates, x2, wa, wb, wo, fg, wr):
    n = x2.shape[0]
    with_router = wr is not None
    full = lambda shape: pl.BlockSpec(shape, lambda i: (0,) * len(shape))
    row = lambda w: pl.BlockSpec((TM_MERGE, w), lambda i: (i, 0))
    in_specs = [
        row(2 * GM_WIDTH), full((1, GM_WIDTH)), full((1, GM_WIDTH)),
        full((GM_GROUPS, GM_CHUNK, GM_CHUNK)), full((GM_CHUNK, GM_GROUPS)),
        row(V_WIDTH), row(2 * D_MODEL), row(D_MODEL),
        full((GM_WIDTH, D_MODEL)), full((V_WIDTH, D_MODEL)), full((D_MODEL, D_MODEL)), full((1, D_MODEL)),
    ]
    args = [uv, vg, vb, ws, bst, ot, gates, x2, wa, wb, wo, fg]
    if with_router:
        in_specs.append(full((D_MODEL, LANES)))
        out_specs = [row(D_MODEL), row(LANES)]
        out_shape = [jax.ShapeDtypeStruct((n, D_MODEL), F32), jax.ShapeDtypeStruct((n, LANES), F32)]
        args.append(wr)
    else:
        out_specs = [row(D_MODEL), row(D_MODEL)]
        out_shape = [jax.ShapeDtypeStruct((n, D_MODEL), F32), jax.ShapeDtypeStruct((n, D_MODEL), BF16)]
    return pl.pallas_call(
        functools.partial(_merge_kernel, with_router=with_router),
        grid=(n // TM_MERGE,),
        in_specs=in_specs, out_specs=out_specs, out_shape=out_shape,
        scratch_shapes=[pltpu.VMEM((TM_MERGE, GM_WIDTH), BF16)],
        compiler_params=_cparams(("parallel",)),
        name="merge_router" if with_router else "merge",
    )(*args)


def _round_up_tile(c):
    return (c + (TM_FFN - 1)) & (-TM_FFN)


def _route_kernel(lg_ref, br_ref, imeta_ref, wcol_ref, cnt_ref, run_ref):
    @pl.when(pl.program_id(0) == 0)
    def _():
        run_ref[...] = jnp.zeros_like(run_ref)

    lt = (lg_ref[...] + br_ref[...]).T[:N_EXPERTS, :]
    sub = lax.broadcasted_iota(jnp.int32, lt.shape, 0)
    m1 = jnp.max(lt, axis=0, keepdims=True)
    i1 = jnp.min(jnp.where(lt == m1, sub, N_EXPERTS), axis=0, keepdims=True)
    rest = jnp.where(sub == i1, -jnp.inf, lt)
    m2 = jnp.max(rest, axis=0, keepdims=True)
    i2 = jnp.min(jnp.where(rest == m2, sub, N_EXPERTS), axis=0, keepdims=True)
    e2 = jnp.exp(m2 - m1)
    w1 = 1.0 / (1.0 + e2)
    w2 = e2 / (1.0 + e2)

    onehot = jnp.where((sub == i1) | (sub == i2), 1.0, 0.0)
    src = lax.broadcasted_iota(jnp.int32, (TM_ROUTE, TM_ROUTE), 0)
    dst = lax.broadcasted_iota(jnp.int32, (TM_ROUTE, TM_ROUTE), 1)
    earlier = jnp.where(src < dst, 1.0, 0.0).astype(BF16)
    seen = jnp.dot(onehot.astype(BF16), earlier, preferred_element_type=F32) + run_ref[:, :1]
    r1 = jnp.sum(jnp.where(sub == i1, seen, 0.0), axis=0, keepdims=True).astype(jnp.int32)
    r2 = jnp.sum(jnp.where(sub == i2, seen, 0.0), axis=0, keepdims=True).astype(jnp.int32)
    imeta_ref[...] = jnp.where(sub == 0, i1, jnp.where(sub == 1, i2, jnp.where(sub == 2, r1, jnp.where(sub == 3, r2, 0))))

    total = run_ref[...] + jnp.sum(onehot, axis=1, keepdims=True)
    run_ref[...] = total
    cnt_ref[...] = total.astype(jnp.int32)

    sub_w = lax.broadcasted_iota(jnp.int32, (LANES, TM_ROUTE), 0)
    wcol_ref[...] = jnp.where(sub_w == 0, w1, jnp.where(sub_w == 1, w2, 0.0)).T


def _route(logits, br):
    n = logits.shape[0]
    return pl.pallas_call(
        _route_kernel,
        grid=(n // TM_ROUTE,),
        in_specs=[pl.BlockSpec((TM_ROUTE, LANES), lambda i: (i, 0)),
                  pl.BlockSpec((1, LANES), lambda i: (0, 0))],
        out_specs=[pl.BlockSpec((N_EXPERTS, TM_ROUTE), lambda i: (0, i)),
                   pl.BlockSpec((TM_ROUTE, LANES), lambda i: (i, 0)),
                   pl.BlockSpec((N_EXPERTS, LANES), lambda i: (0, 0))],
        out_shape=[jax.ShapeDtypeStruct((N_EXPERTS, n), jnp.int32),
                   jax.ShapeDtypeStruct((n, LANES), F32),
                   jax.ShapeDtypeStruct((N_EXPERTS, LANES), jnp.int32)],
        scratch_shapes=[pltpu.VMEM((N_EXPERTS, LANES), F32)],
        compiler_params=_cparams(("arbitrary",)),
        name="route",
    )(logits, br)


def _positions_kernel(cnt_ref, imeta_ref, pos_ref):
    im = imeta_ref[...]
    i1, i2, r1, r2 = im[0:1], im[1:2], im[2:3], im[3:4]
    start = jnp.int32(0)
    p1 = jnp.zeros_like(i1)
    p2 = jnp.zeros_like(i2)
    for e in range(N_EXPERTS):
        p1 = jnp.where(i1 == e, start, p1)
        p2 = jnp.where(i2 == e, start, p2)
        start = start + _round_up_tile(cnt_ref[e])
    pos_ref[...] = jnp.concatenate([p1 + r1, p2 + r2], axis=0)


def _positions(cnt, imeta):
    n = imeta.shape[1]
    return pl.pallas_call(
        _positions_kernel,
        grid_spec=pltpu.PrefetchScalarGridSpec(
            num_scalar_prefetch=1,
            grid=(n // TM_POS,),
            in_specs=[pl.BlockSpec((N_EXPERTS, TM_POS), lambda i, c: (0, i))],
            out_specs=pl.BlockSpec((2, TM_POS), lambda i, c: (0, i)),
        ),
        out_shape=jax.ShapeDtypeStruct((2, n), jnp.int32),
        compiler_params=_cparams(("parallel",)),
        name="positions",
    )(cnt, imeta)


def _scatter_kernel(pos_ref, cnt_ref, xn_ref, fg_ref, xs_ref, hbuf, zbuf, sems, zsem, *, n_tokens):
    n_tiles = xs_ref.shape[0] // TM_FFN

    @pl.when(pl.program_id(0) == 0)
    def _():
        zbuf[...] = jnp.zeros_like(zbuf)

        def zero_tile(t):
            rows = pl.ds(pl.multiple_of(t * TM_FFN, TM_FFN), TM_FFN)
            return pltpu.make_async_copy(zbuf, xs_ref.at[rows], zsem)

        def each_zero_tile(action):
            end = jnp.int32(0)
            for e in range(N_EXPERTS):
                tiles = _round_up_tile(cnt_ref[e]) // TM_FFN
                end = end + tiles

                @pl.when(tiles > 0)
                def _(end=end):
                    action(zero_tile(end - 1))

            def tail(t, carry):
                action(zero_tile(t))
                return carry

            lax.fori_loop(end, n_tiles, tail, 0)

        each_zero_tile(lambda copy: copy.start())
        each_zero_tile(lambda copy: copy.wait())

    i = pl.program_id(0)
    last = pl.num_programs(0) - 1
    slot = i % 2

    def drain(s):
        for k in range(2):
            pltpu.make_async_copy(hbuf.at[s], xs_ref.at[pl.ds(0, TM_SCAT)], sems.at[s]).wait()

    @pl.when(i >= 2)
    def _():
        drain(slot)

    base = i * TM_SCAT
    hbuf[slot] = _rms(xn_ref[...], fg_ref[...])

    def issue(r, carry):
        for k in range(2):
            dst = pos_ref[k * n_tokens + base + r]
            pltpu.make_async_copy(hbuf.at[slot, pl.ds(r, 1)], xs_ref.at[pl.ds(dst, 1)], sems.at[slot]).start()
        return carry

    lax.fori_loop(0, TM_SCAT, issue, 0, unroll=8)

    @pl.when(i == last)
    def _():
        drain(1 - slot)
        drain(slot)


def _scatter(pos, cnt, xn, fg):
    n = xn.shape[0]
    assert n // TM_SCAT >= 2
    rows = 2 * n + N_EXPERTS * TM_FFN
    return pl.pallas_call(
        functools.partial(_scatter_kernel, n_tokens=n),
        grid_spec=pltpu.PrefetchScalarGridSpec(
            num_scalar_prefetch=2,
            grid=(n // TM_SCAT,),
            in_specs=[pl.BlockSpec((TM_SCAT, D_MODEL), lambda i, p, c: (i, 0)),
                      pl.BlockSpec((1, D_MODEL), lambda i, p, c: (0, 0))],
            out_specs=pl.BlockSpec(memory_space=pl.ANY),
            scratch_shapes=[pltpu.VMEM((2, TM_SCAT, D_MODEL), F32), pltpu.VMEM((TM_FFN, D_MODEL), F32),
                            pltpu.SemaphoreType.DMA((2,)), pltpu.SemaphoreType.DMA(())],
        ),
        out_shape=jax.ShapeDtypeStruct((rows, D_MODEL), F32),
        compiler_params=_cparams(("arbitrary",)),
        name="moe_scatter",
    )(pos, cnt, xn, fg)


def _combine_kernel(pos_ref, xn_ref, wcol_ref, ys_ref, o_ref, buf, sems, *, n_tokens):
    i = pl.program_id(0)
    slot = i % 2

    def fetch_tile(tile, s):
        base = tile * TM_COMB

        def issue(r, carry):
            for k in range(2):
                src = pos_ref[k * n_tokens + base + r]
                pltpu.make_async_copy(ys_ref.at[pl.ds(src, 1)], buf.at[s, k, pl.ds(r, 1)], sems.at[s]).start()
            return carry

        lax.fori_loop(0, TM_COMB, issue, 0, unroll=8)

    @pl.when(i == 0)
    def _():
        fetch_tile(0, 0)

    @pl.when(i + 1 < pl.num_programs(0))
    def _():
        fetch_tile(i + 1, 1 - slot)

    for k in range(2):
        pltpu.make_async_copy(ys_ref.at[pl.ds(0, TM_COMB)], buf.at[slot, k], sems.at[slot]).wait()
    w = wcol_ref[...]
    o_ref[...] = xn_ref[...] + (w[:, 0:1] * buf[slot, 0] + w[:, 1:2] * buf[slot, 1])


def _combine(pos, xn, wcol, ys):
    n = xn.shape[0]
    return pl.pallas_call(
        functools.partial(_combine_kernel, n_tokens=n),
        grid_spec=pltpu.PrefetchScalarGridSpec(
            num_scalar_prefetch=1,
            grid=(n // TM_COMB,),
            in_specs=[pl.BlockSpec((TM_COMB, D_MODEL), lambda i, p: (i, 0)),
                      pl.BlockSpec((TM_COMB, LANES), lambda i, p: (i, 0)),
                      pl.BlockSpec(memory_space=pl.ANY)],
            out_specs=pl.BlockSpec((TM_COMB, D_MODEL), lambda i, p: (i, 0)),
            scratch_shapes=[pltpu.VMEM((2, 2, TM_COMB, D_MODEL), F32), pltpu.SemaphoreType.DMA((2,))],
        ),
        out_shape=jax.ShapeDtypeStruct((n, D_MODEL), F32),
        compiler_params=_cparams(("arbitrary",)),
        name="moe_combine",
    )(pos, xn, wcol, ys)


def _swiglu(h, wg_ref, wu_ref, wd_ref):
    y = None
    c0 = 0
    for width in FF_CHUNKS:
        cs = slice(c0, c0 + width)
        g = jnp.dot(h, wg_ref[:, cs], preferred_element_type=F32)
        u = jnp.dot(h, wu_ref[:, cs], preferred_element_type=F32)
        act = (g * jax.nn.sigmoid(g) * u).astype(BF16)
        part = jnp.dot(act, wd_ref[cs, :], preferred_element_type=F32)
        y = part if y is None else y + part
        c0 += width
    return y


def _ffn_dense_kernel(h_ref, x_ref, wg_ref, wu_ref, wd_ref, o_ref):
    o_ref[...] = x_ref[...] + _swiglu(h_ref[...], wg_ref, wu_ref, wd_ref)


def _ffn_dense(h2, x2, wgu, wd):
    n = h2.shape[0]
    return pl.pallas_call(
        _ffn_dense_kernel,
        grid=(n // TM_FFN,),
        in_specs=[
            pl.BlockSpec((TM_FFN, D_MODEL), lambda i: (i, 0)),
            pl.BlockSpec((TM_FFN, D_MODEL), lambda i: (i, 0)),
            pl.BlockSpec((D_MODEL, FF), lambda i: (0, 0)),
            pl.BlockSpec((D_MODEL, FF), lambda i: (0, 1)),
            pl.BlockSpec((FF, D_MODEL), lambda i: (0, 0)),
        ],
        out_specs=pl.BlockSpec((TM_FFN, D_MODEL), lambda i: (i, 0)),
        out_shape=jax.ShapeDtypeStruct((n, D_MODEL), F32),
        compiler_params=_cparams(("parallel",)),
        name="ffn_dense",
    )(h2, x2, wgu, wgu, wd)


def _tile_plan(i, cnt_ref):
    end = jnp.int32(0)
    expert = jnp.int32(0)
    for e in range(N_EXPERTS):
        end = end + _round_up_tile(cnt_ref[e])
        expert = expert + (end <= i * TM_FFN).astype(jnp.int32)
    return jnp.minimum(expert, N_EXPERTS - 1), end // TM_FFN


def _ffn_grouped_kernel(cnt_ref, xs_ref, wg_ref, wu_ref, wd_ref, o_ref):
    i = pl.program_id(0)
    _, used = _tile_plan(i, cnt_ref)

    @pl.when(i < used)
    def _():
        o_ref[...] = _swiglu(xs_ref[...].astype(BF16), wg_ref, wu_ref, wd_ref)

    @pl.when(i >= used)
    def _():
        o_ref[...] = jnp.zeros_like(o_ref)


def _ffn_grouped(cnt, xs, wgu, wd):
    rows = xs.shape[0]

    def tile(i, c):
        return jnp.minimum(i, _tile_plan(i, c)[1] - 1)

    return pl.pallas_call(
        _ffn_grouped_kernel,
        grid_spec=pltpu.PrefetchScalarGridSpec(
            num_scalar_prefetch=1,
            grid=(rows // TM_FFN,),
            in_specs=[
                pl.BlockSpec((TM_FFN, D_MODEL), lambda i, c: (tile(i, c), 0)),
                pl.BlockSpec((None, D_MODEL, FF), lambda i, c: (_tile_plan(i, c)[0], 0, 0)),
                pl.BlockSpec((None, D_MODEL, FF), lambda i, c: (_tile_plan(i, c)[0], 0, 1)),
                pl.BlockSpec((None, FF, D_MODEL), lambda i, c: (_tile_plan(i, c)[0], 0, 0)),
            ],
            out_specs=pl.BlockSpec((TM_FFN, D_MODEL), lambda i, c: (i, 0)),
        ),
        out_shape=jax.ShapeDtypeStruct((rows, D_MODEL), F32),
        compiler_params=_cparams(("arbitrary",)),
        name="ffn_grouped",
    )(cnt, xs, wgu, wgu, wd)


def _moe(xn, logits, br, fg, wgu, wd):
    n = xn.shape[0]
    imeta, wcol, cnt_lanes = _route(logits, br)
    cnt = cnt_lanes[:, 0]
    pos = _positions(cnt, imeta).reshape(2 * n)
    xs = _scatter(pos, cnt, xn, fg)
    ys = _ffn_grouped(cnt, xs, wgu, wd)
    return _combine(pos, xn, wcol, ys)


def _rope_tables(seq):
    pos = jnp.arange(seq, dtype=F32)
    inv_freq = ROPE_THETA ** (-jnp.arange(0, QK_ROPE, 2, dtype=F32) / QK_ROPE)
    ang = pos[:, None] * inv_freq[None, :]
    return jnp.cos(ang), jnp.sin(ang)


def _swap_halves(w):
    half = QK_ROPE // 2
    return jnp.concatenate([w[..., half:], w[..., :half]], axis=-1)


def _head_slots(w, width):
    k = w.shape[0]
    w3 = w.reshape(k, MLA_HEADS, width)
    return jnp.pad(w3, ((0, 0), (0, 0), (0, HEAD_SLOT - width))).reshape(k, QK_WIDTH)


def _pad_lanes(v, width=LANES):
    return jnp.pad(v, (0, width - v.shape[0])).reshape(1, width)


def kernel(x, mix_norm_g, w_in, gm_v_norm_g, gm_v_norm_b, gm_w_spatial, gm_b_spatial, gm_w_proj, mla_q_lat_g, mla_w_uq, mla_kv_lat_g, mla_w_ukv, mla_q_norm_g, mla_k_norm_g, mla_w_proj, w_out, ffn_norm_g, dense_w_gu, dense_w_down, moe_w_router, moe_b_router, moe_w_gu, moe_w_down):
    batch, seq, d = x.shape
    n = batch * seq
    depth = w_in.shape[0]
    cos, sin = _rope_tables(seq)
    x2 = x.reshape(n, d)
    q_scale = float(np.log2(np.e) / np.sqrt(QK_DIM))

    o_kr = 2 * GM_WIDTH + Q_LORA + KV_LORA
    o_gate = o_kr + QK_ROPE

    ones_nope, zeros_nope = jnp.ones((seq, QK_NOPE), F32), jnp.zeros((seq, QK_NOPE), F32)
    cc, ss = jnp.concatenate([cos, cos], axis=1), jnp.concatenate([-sin, sin], axis=1)
    tq = jnp.concatenate([ones_nope, cc, ss], axis=1)
    ta = jnp.concatenate([zeros_nope, cc, cc], axis=1)
    tb = jnp.concatenate([zeros_nope, ss, ss], axis=1)

    for l in range(depth):
        wl = w_in[l]
        w_kr = wl[:, o_kr:o_gate]
        w_krx = jnp.pad(jnp.concatenate([w_kr, w_kr], axis=1), ((0, 0), (QK_NOPE, 0)))
        w_krs = jnp.pad(jnp.concatenate([_swap_halves(w_kr)] * 2, axis=1), ((0, 0), (QK_NOPE, 0)))
        w_kr2 = jnp.concatenate([w_krx, w_krs], axis=1).astype(BF16)
        wq3 = mla_w_uq[l].reshape(Q_LORA, MLA_HEADS, QK_DIM)
        wq = jnp.concatenate([wq3, _swap_halves(wq3[:, :, QK_NOPE:])], axis=2).reshape(Q_LORA, QK_WIDTH).astype(BF16)
        wkv3 = mla_w_ukv[l].reshape(KV_LORA, MLA_HEADS, QK_NOPE + V_DIM)
        wk = _head_slots(wkv3[:, :, :QK_NOPE].reshape(KV_LORA, MLA_HEADS * QK_NOPE), QK_NOPE).astype(BF16)
        wvt = wkv3[:, :, QK_NOPE:].reshape(KV_LORA, V_WIDTH).T.astype(BF16)
        gq, gk = mla_q_norm_g[l], mla_k_norm_g[l]
        g12 = lambda g: g[QK_NOPE:]
        g21 = lambda g: _swap_halves(g[QK_NOPE:])
        gq_row = (jnp.concatenate([gq[:QK_NOPE], g12(gq), g21(gq)]) * q_scale).reshape(1, HEAD_SLOT)
        ga_row = _pad_lanes(jnp.concatenate([jnp.zeros((QK_NOPE,), F32), g12(gk), g12(gk)]))
        gb_row = _pad_lanes(jnp.concatenate([jnp.zeros((QK_NOPE,), F32), g21(gk), g21(gk)]))
        uv, gates, q, k, vt = _in_qkv(x2, mix_norm_g[l].reshape(1, d),
                                      wl[:, :o_kr].astype(BF16), w_kr2, wl[:, o_gate:].astype(BF16),
                                      mla_q_lat_g[l].reshape(1, Q_LORA), wq,
                                      mla_kv_lat_g[l].reshape(1, KV_LORA), wk, wvt,
                                      tq, ta, tb, gq_row, ga_row, gb_row, _pad_lanes(gk[:QK_NOPE]), batch, seq)
        slots = lambda t: t.reshape(batch, seq, QK_WIDTH)
        ot = _attention(slots(q), slots(k), vt).reshape(n, V_WIDTH)

        is_moe = l % 2 == 1
        m = l // 2
        wr = jnp.pad(moe_w_router[m], ((0, 0), (0, LANES - N_EXPERTS))) if is_moe else None
        outs = _merge(uv, gm_v_norm_g[l].reshape(1, GM_WIDTH), gm_v_norm_b[l].reshape(1, GM_WIDTH),
                      gm_w_spatial[l], gm_b_spatial[l].T, ot, gates, x2,
                      gm_w_proj[l].astype(BF16), mla_w_proj[l].astype(BF16),
                      w_out[l].astype(BF16), ffn_norm_g[l].reshape(1, d), wr)
        if is_moe:
            xn, logits = outs
            x2 = _moe(xn, logits, _pad_lanes(moe_b_router[m]), ffn_norm_g[l].reshape(1, d),
                      moe_w_gu[m].astype(BF16), moe_w_down[m].astype(BF16))
        else:
            xn, h2 = outs
            x2 = _ffn_dense(h2, xn, dense_w_gu[m].astype(BF16), dense_w_down[m].astype(BF16))
    return x2.reshape(batch, seq, d)
```

```python
import functools

import jax
import jax.numpy as jnp
import numpy as np
from jax import lax
from jax.experimental import pallas as pl
from jax.experimental.pallas import tpu as pltpu

F32 = jnp.float32
BF16 = jnp.bfloat16

EPS = 1e-6
LANES = 128

D_MODEL = 1024
GM_GROUPS = 8
GM_GROUP_CH = 128
GM_WIDTH = 1024
GM_CHUNK = 128
MLA_HEADS = 16
QK_NOPE = 64
QK_ROPE = 32
QK_DIM = 96
V_DIM = 64
Q_LORA = 512
KV_LORA = 256
ROPE_THETA = 10000.0
HEAD_SLOT = LANES
QK_WIDTH = MLA_HEADS * HEAD_SLOT
V_WIDTH = MLA_HEADS * V_DIM
N_EXPERTS = 8
FF = 2816

TM_IN = 256
TM_MERGE = 512
TM_FFN = 512
FF_CHUNKS = (768, 768, 768, 512)
TM_POS = 2048
TM_SCAT = 512
TM_COMB = 512
ATT_T = 512
ATT_G = 2
ATT_SUM_ROWS = 16

VMEM_LIMIT = 56 * 1024 * 1024


def _cparams(sem):
    return pltpu.CompilerParams(dimension_semantics=sem, vmem_limit_bytes=VMEM_LIMIT)


def _rms(xf, g):
    return xf * lax.rsqrt(jnp.mean(xf * xf, axis=-1, keepdims=True) + EPS) * g


def _in_qkv_kernel(x_ref, g_ref, w_ref, wkr_ref, wgate_ref, qg_ref, wq_ref, kvg_ref, wk_ref, wvt_ref,
                   tq_ref, ta_ref, tb_ref, gq_ref, ga_ref, gb_ref, gkn_ref,
                   uv_ref, gate_ref, q_ref, k_ref, vt_ref):
    h = _rms(x_ref[...], g_ref[...]).astype(BF16)
    o_lat = 2 * GM_WIDTH
    lat = jnp.dot(h, w_ref[:, o_lat:], preferred_element_type=F32)
    krx = jnp.dot(h, wkr_ref[...], preferred_element_type=F32)
    uv_ref[...] = _gelu_tanh(jnp.dot(h, w_ref[:, :o_lat], preferred_element_type=F32)).astype(BF16)
    gate_ref[...] = jax.nn.sigmoid(jnp.dot(h, wgate_ref[...], preferred_element_type=F32)).astype(BF16)

    cq = _rms(lat[:, :Q_LORA], qg_ref[...]).astype(BF16)
    ckv = _rms(lat[:, Q_LORA:], kvg_ref[...]).astype(BF16)
    kr = krx[:, :LANES]
    kr_sw = krx[:, LANES:]
    kr_ssq = 0.5 * jnp.sum(kr * kr, axis=-1, keepdims=True)
    kr_rot = kr * (ta_ref[...] * ga_ref[...]) + kr_sw * (tb_ref[...] * gb_ref[...])
    tq, gkn = tq_ref[...] * gq_ref[...], gkn_ref[...]
    q_all = jnp.dot(cq, wq_ref[...], preferred_element_type=F32)
    k_all = jnp.dot(ckv, wk_ref[...], preferred_element_type=F32)
    first_copy = lax.broadcasted_iota(jnp.int32, (1, HEAD_SLOT), 1) < QK_DIM
    for hd in range(MLA_HEADS):
        sl = slice(hd * HEAD_SLOT, (hd + 1) * HEAD_SLOT)
        qh = q_all[:, sl]
        ssq = jnp.sum(qh * jnp.where(first_copy, qh, 0.0), axis=-1, keepdims=True)
        q_ref[:, sl] = (qh * tq * lax.rsqrt(ssq * (1.0 / QK_DIM) + EPS)).astype(BF16)
        kh = k_all[:, sl]
        rk = lax.rsqrt((jnp.sum(kh * kh, axis=-1, keepdims=True) + kr_ssq) * (1.0 / QK_DIM) + EPS)
        k_ref[:, sl] = ((kh * gkn + kr_rot) * rk).astype(BF16)
    vt_ref[...] = lax.dot_general(wvt_ref[...], ckv, (((1,), (1,)), ((), ())),
                                  preferred_element_type=F32).astype(BF16)


def _in_qkv(x2, g, w_main, w_kr, w_gate, qg, wq, kvg, wk, wvt, tq, ta, tb, gq, ga, gb, gkn, batch, seq):
    n = x2.shape[0]
    tps = seq // TM_IN
    full = lambda shape: pl.BlockSpec(shape, lambda i: (0,) * len(shape), pipeline_mode=pl.Buffered(1))
    rope = pl.BlockSpec((TM_IN, HEAD_SLOT), lambda i: (i % tps, 0))
    row = lambda w: pl.BlockSpec((TM_IN, w), lambda i: (i, 0))
    lane_row = full((1, HEAD_SLOT))
    return pl.pallas_call(
        _in_qkv_kernel,
        grid=(n // TM_IN,),
        in_specs=[
            row(D_MODEL), full((1, D_MODEL)), full(w_main.shape), full(w_kr.shape), full(w_gate.shape),
            full((1, Q_LORA)), full((Q_LORA, QK_WIDTH)),
            full((1, KV_LORA)), full((KV_LORA, QK_WIDTH)), full((V_WIDTH, KV_LORA)),
            rope, rope, rope, lane_row, lane_row, lane_row, lane_row,
        ],
        out_specs=[row(2 * GM_WIDTH), row(2 * D_MODEL), row(QK_WIDTH), row(QK_WIDTH),
                   pl.BlockSpec((None, None, V_WIDTH, TM_IN), lambda i: (i // tps, i % tps, 0, 0))],
        out_shape=[jax.ShapeDtypeStruct((n, 2 * GM_WIDTH), BF16), jax.ShapeDtypeStruct((n, 2 * D_MODEL), BF16),
                   jax.ShapeDtypeStruct((n, QK_WIDTH), BF16), jax.ShapeDtypeStruct((n, QK_WIDTH), BF16),
                   jax.ShapeDtypeStruct((batch, tps, V_WIDTH, TM_IN), BF16)],
        compiler_params=_cparams(("parallel",)),
        name="in_qkv",
    )(x2, g, w_main, w_kr, w_gate, qg, wq, kvg, wk, wvt, tq, ta, tb, gq, ga, gb, gkn)


def _attention_kernel(q_ref, k_ref, vt_ref, o_ref, *, seq):
    nblk = seq // ATT_T
    vblocks = ATT_T // TM_IN
    key_pos = lax.broadcasted_iota(jnp.int32, (ATT_T, ATT_T), 0)
    qry_pos = lax.broadcasted_iota(jnp.int32, (ATT_T, ATT_T), 1)
    causal = key_pos <= qry_pos
    nt = (((1,), (1,)), ((), ()))
    ones_rows = jnp.ones((ATT_SUM_ROWS, ATT_T), BF16)

    def update(state, g, st, kj, diagonal):
        m, acc = state
        if diagonal:
            st = jnp.where(causal, st, -1e30)
        m_new = jnp.maximum(m, jnp.max(st, axis=0, keepdims=True))
        alpha = jnp.exp2(m - m_new)
        p = jnp.exp2((st - m_new).astype(BF16))
        vt = [vt_ref[kj * vblocks + b, g * V_DIM:(g + 1) * V_DIM, :] for b in range(vblocks)]
        v_aug = jnp.concatenate([jnp.concatenate(vt, axis=1), ones_rows], axis=0)
        return m_new, alpha * acc + jnp.dot(v_aug, p, preferred_element_type=F32)

    for qi in range(nblk):
        q0, q1 = qi * ATT_T, (qi + 1) * ATT_T
        strips = [lax.dot_general(k_ref[0:q1, g * HEAD_SLOT:(g + 1) * HEAD_SLOT],
                                  q_ref[q0:q1, g * HEAD_SLOT:(g + 1) * HEAD_SLOT], nt,
                                  preferred_element_type=F32) for g in range(ATT_G)]
        acc_rows = V_DIM + ATT_SUM_ROWS
        init = (jnp.full((1, ATT_T), -1e30, F32), jnp.zeros((acc_rows, ATT_T), F32))
        states = [init for _ in range(ATT_G)]
        for kj in range(qi + 1):
            states = [update(states[g], g, strips[g][kj * ATT_T:(kj + 1) * ATT_T], kj, kj == qi)
                      for g in range(ATT_G)]
        pad_rows = jnp.zeros((HEAD_SLOT - acc_rows, ATT_T), F32)
        for g in range(ATT_G):
            acc = jnp.concatenate([states[g][1], pad_rows], axis=0).T
            o_ref[q0:q1, g * V_DIM:(g + 1) * V_DIM] = (acc[:, :V_DIM] / acc[:, V_DIM:V_DIM + 1]).astype(BF16)


def _attention(q3, k3, vt4):
    batch, seq, _ = q3.shape
    slot = pl.BlockSpec((None, seq, ATT_G * HEAD_SLOT), lambda b, h: (b, 0, h))
    return pl.pallas_call(
        functools.partial(_attention_kernel, seq=seq),
        grid=(batch, MLA_HEADS // ATT_G),
        in_specs=[slot, slot,
                  pl.BlockSpec((None, seq // TM_IN, ATT_G * V_DIM, TM_IN), lambda b, h: (b, 0, h, 0))],
        out_specs=pl.BlockSpec((None, seq, ATT_G * V_DIM), lambda b, h: (b, 0, h)),
        out_shape=jax.ShapeDtypeStruct((batch, seq, V_WIDTH), BF16),
        compiler_params=_cparams(("parallel", "parallel")),
        name="attention",
    )(q3, k3, vt4)


def _split_bf16(v):
    hi = v.astype(BF16)
    return hi, (v - hi.astype(F32)).astype(BF16)


def _gelu_tanh(x):
    c1 = -2.0 * float(np.sqrt(2.0 / np.pi) * np.log2(np.e))
    c2 = c1 * 0.044715
    return x / (1.0 + jnp.exp2(x * (c1 + c2 * (x * x))))


def _gmlp_tile(uv_ref, vg_ref, vb_ref, ws_ref, bst_ref, a_ref):
    v = uv_ref[:, GM_WIDTH:].astype(F32)
    mu = jnp.mean(v, axis=-1, keepdims=True)
    vc = v - mu
    vn = (vc * lax.rsqrt(jnp.mean(vc * vc, axis=-1, keepdims=True) + EPS) * vg_ref[...] + vb_ref[...]).astype(BF16)
    row = lax.broadcasted_iota(jnp.int32, (GM_CHUNK, GM_CHUNK), 0)
    col = lax.broadcasted_iota(jnp.int32, (GM_CHUNK, GM_CHUNK), 1)
    tril = col <= row
    bst = bst_ref[...]
    for g in range(GM_GROUPS):
        ws = jnp.where(tril, ws_ref[g], 0.0).astype(BF16)
        bias = bst[:, g:g + 1]
        cs = slice(g * GM_GROUP_CH, (g + 1) * GM_GROUP_CH)
        for c in range(TM_MERGE // GM_CHUNK):
            rs = slice(c * GM_CHUNK, (c + 1) * GM_CHUNK)
            mixed = jnp.dot(ws, vn[rs, cs], preferred_element_type=F32) + bias
            a_ref[rs, cs] = (uv_ref[rs, cs].astype(F32) * mixed).astype(BF16)


def _merge_kernel(uv_ref, vg_ref, vb_ref, ws_ref, bst_ref, ot_ref, gate_ref, x_ref, wa_ref, wb_ref, wo_ref,
                  fg_ref, *rest, with_router):
    if with_router:
        wr_ref, br_ref, xn_ref, imeta_ref, wcol_ref, cnt_ref, a_ref, run_ref = rest
    else:
        xn_ref, h2_ref, a_ref = rest
    yb = jnp.dot(ot_ref[...], wb_ref[...], preferred_element_type=F32)
    _gmlp_tile(uv_ref, vg_ref, vb_ref, ws_ref, bst_ref, a_ref)
    ya = jnp.dot(a_ref[...], wa_ref[...], preferred_element_type=F32)
    ga = gate_ref[:, :D_MODEL].astype(F32)
    gb = gate_ref[:, D_MODEL:].astype(F32)
    merged = (ga * ya + gb * yb).astype(BF16)
    xn = x_ref[...] + jnp.dot(merged, wo_ref[...], preferred_element_type=F32)
    xn_ref[...] = xn
    h2 = _rms(xn, fg_ref[...])
    if with_router:
        h_hi, h_lo = _split_bf16(h2)
        w_hi, w_lo = _split_bf16(wr_ref[...])
        hh_hl = jnp.dot(h_hi, jnp.concatenate([w_hi, w_lo], axis=1), preferred_element_type=F32)
        logits = hh_hl[:, :LANES] + (hh_hl[:, LANES:] + jnp.dot(h_lo, w_hi, preferred_element_type=F32))
        _route_tile(logits + br_ref[...], run_ref, imeta_ref, wcol_ref, cnt_ref)
    else:
        h2_ref[...] = h2.astype(BF16)


def _merge(uv, vg, vb, ws, bst, ot, gates, x2, wa, wb, wo, fg, wr=None, br=None):
    n = x2.shape[0]
    with_router = wr is not None
    full = lambda shape: pl.BlockSpec(shape, lambda i: (0,) * len(shape))
    row = lambda w: pl.BlockSpec((TM_MERGE, w), lambda i: (i, 0))
    in_specs = [
        row(2 * GM_WIDTH), full((1, GM_WIDTH)), full((1, GM_WIDTH)),
        full((GM_GROUPS, GM_CHUNK, GM_CHUNK)), full((GM_CHUNK, GM_GROUPS)),
        row(V_WIDTH), row(2 * D_MODEL), row(D_MODEL),
        full((GM_WIDTH, D_MODEL)), full((V_WIDTH, D_MODEL)), full((D_MODEL, D_MODEL)), full((1, D_MODEL)),
    ]
    args = [uv, vg, vb, ws, bst, ot, gates, x2, wa, wb, wo, fg]
    scratch = [pltpu.VMEM((TM_MERGE, GM_WIDTH), BF16)]
    if with_router:
        in_specs += [full((D_MODEL, LANES)), full((1, LANES))]
        out_specs = [row(D_MODEL), pl.BlockSpec((N_EXPERTS, TM_MERGE), lambda i: (0, i)), row(LANES),
                     full((N_EXPERTS, LANES))]
        out_shape = [jax.ShapeDtypeStruct((n, D_MODEL), F32), jax.ShapeDtypeStruct((N_EXPERTS, n), jnp.int32),
                     jax.ShapeDtypeStruct((n, LANES), F32), jax.ShapeDtypeStruct((N_EXPERTS, LANES), jnp.int32)]
        args += [wr, br]
        scratch.append(pltpu.VMEM((N_EXPERTS, LANES), F32))
    else:
        out_specs = [row(D_MODEL), row(D_MODEL)]
        out_shape = [jax.ShapeDtypeStruct((n, D_MODEL), F32), jax.ShapeDtypeStruct((n, D_MODEL), BF16)]
    return pl.pallas_call(
        functools.partial(_merge_kernel, with_router=with_router),
        grid=(n // TM_MERGE,),
        in_specs=in_specs, out_specs=out_specs, out_shape=out_shape,
        scratch_shapes=scratch,
        compiler_params=_cparams(("arbitrary",) if with_router else ("parallel",)),
        name="merge_router" if with_router else "merge",
    )(*args)


def _round_up_tile(c):
    return (c + (TM_FFN - 1)) & (-TM_FFN)


def _route_tile(logits, run_ref, imeta_ref, wcol_ref, cnt_ref):
    tm = logits.shape[0]

    @pl.when(pl.program_id(0) == 0)
    def _():
        run_ref[...] = jnp.zeros_like(run_ref)

    lt = logits.T[:N_EXPERTS, :]
    sub = lax.broadcasted_iota(jnp.int32, lt.shape, 0)
    m1 = jnp.max(lt, axis=0, keepdims=True)
    i1 = jnp.min(jnp.where(lt == m1, sub, N_EXPERTS), axis=0, keepdims=True)
    rest = jnp.where(sub == i1, -jnp.inf, lt)
    m2 = jnp.max(rest, axis=0, keepdims=True)
    i2 = jnp.min(jnp.where(rest == m2, sub, N_EXPERTS), axis=0, keepdims=True)
    e2 = jnp.exp(m2 - m1)
    w1 = 1.0 / (1.0 + e2)
    w2 = e2 / (1.0 + e2)

    onehot = jnp.where((sub == i1) | (sub == i2), 1.0, 0.0)
    src = lax.broadcasted_iota(jnp.int32, (tm, tm), 0)
    dst = lax.broadcasted_iota(jnp.int32, (tm, tm), 1)
    earlier = jnp.where(src < dst, 1.0, 0.0).astype(BF16)
    seen = jnp.dot(onehot.astype(BF16), earlier, preferred_element_type=F32) + run_ref[:, :1]
    r1 = jnp.sum(jnp.where(sub == i1, seen, 0.0), axis=0, keepdims=True).astype(jnp.int32)
    r2 = jnp.sum(jnp.where(sub == i2, seen, 0.0), axis=0, keepdims=True).astype(jnp.int32)
    imeta_ref[...] = jnp.where(sub == 0, i1, jnp.where(sub == 1, i2, jnp.where(sub == 2, r1, jnp.where(sub == 3, r2, 0))))

    total = run_ref[...] + jnp.sum(onehot, axis=1, keepdims=True)
    run_ref[...] = total
    cnt_ref[...] = total.astype(jnp.int32)

    sub_w = lax.broadcasted_iota(jnp.int32, (LANES, tm), 0)
    wcol_ref[...] = jnp.where(sub_w == 0, w1, jnp.where(sub_w == 1, w2, 0.0)).T


def _positions_kernel(cnt_ref, imeta_ref, pos_ref):
    im = imeta_ref[...]
    i1, i2, r1, r2 = im[0:1], im[1:2], im[2:3], im[3:4]
    start = jnp.int32(0)
    p1 = jnp.zeros_like(i1)
    p2 = jnp.zeros_like(i2)
    for e in range(N_EXPERTS):
        p1 = jnp.where(i1 == e, start, p1)
        p2 = jnp.where(i2 == e, start, p2)
        start = start + _round_up_tile(cnt_ref[e])
    pos_ref[...] = jnp.concatenate([p1 + r1, p2 + r2], axis=0)


def _positions(cnt, imeta):
    n = imeta.shape[1]
    return pl.pallas_call(
        _positions_kernel,
        grid_spec=pltpu.PrefetchScalarGridSpec(
            num_scalar_prefetch=1,
            grid=(n // TM_POS,),
            in_specs=[pl.BlockSpec((N_EXPERTS, TM_POS), lambda i, c: (0, i))],
            out_specs=pl.BlockSpec((2, TM_POS), lambda i, c: (0, i)),
        ),
        out_shape=jax.ShapeDtypeStruct((2, n), jnp.int32),
        compiler_params=_cparams(("parallel",)),
        name="positions",
    )(cnt, imeta)


def _scatter_kernel(pos_ref, cnt_ref, xn_ref, fg_ref, xs_ref, hbuf, zbuf, sems, zsem, *, n_tokens):
    n_tiles = xs_ref.shape[0] // TM_FFN

    @pl.when(pl.program_id(0) == 0)
    def _():
        zbuf[...] = jnp.zeros_like(zbuf)

        def zero_tile(t):
            rows = pl.ds(pl.multiple_of(t * TM_FFN, TM_FFN), TM_FFN)
            return pltpu.make_async_copy(zbuf, xs_ref.at[rows], zsem)

        def each_zero_tile(action):
            end = jnp.int32(0)
            for e in range(N_EXPERTS):
                tiles = _round_up_tile(cnt_ref[e]) // TM_FFN
                end = end + tiles

                @pl.when(tiles > 0)
                def _(end=end):
                    action(zero_tile(end - 1))

            def tail(t, carry):
                action(zero_tile(t))
                return carry

            lax.fori_loop(end, n_tiles, tail, 0)

        each_zero_tile(lambda copy: copy.start())
        each_zero_tile(lambda copy: copy.wait())

    i = pl.program_id(0)
    last = pl.num_programs(0) - 1
    slot = i % 2

    def drain(s):
        for k in range(2):
            pltpu.make_async_copy(hbuf.at[s], xs_ref.at[pl.ds(0, TM_SCAT)], sems.at[s]).wait()

    @pl.when(i >= 2)
    def _():
        drain(slot)

    base = i * TM_SCAT
    hbuf[slot] = _rms(xn_ref[...], fg_ref[...])

    def issue(r, carry):
        for k in range(2):
            dst = pos_ref[k * n_tokens + base + r]
            pltpu.make_async_copy(hbuf.at[slot, pl.ds(r, 1)], xs_ref.at[pl.ds(dst, 1)], sems.at[slot]).start()
        return carry

    lax.fori_loop(0, TM_SCAT, issue, 0, unroll=8)

    @pl.when(i == last)
    def _():
        drain(1 - slot)
        drain(slot)


def _scatter(pos, cnt, xn, fg):
    n = xn.shape[0]
    assert n // TM_SCAT >= 2
    rows = 2 * n + N_EXPERTS * TM_FFN
    return pl.pallas_call(
        functools.partial(_scatter_kernel, n_tokens=n),
        grid_spec=pltpu.PrefetchScalarGridSpec(
            num_scalar_prefetch=2,
            grid=(n // TM_SCAT,),
            in_specs=[pl.BlockSpec((TM_SCAT, D_MODEL), lambda i, p, c: (i, 0)),
                      pl.BlockSpec((1, D_MODEL), lambda i, p, c: (0, 0))],
            out_specs=pl.BlockSpec(memory_space=pl.ANY),
            scratch_shapes=[pltpu.VMEM((2, TM_SCAT, D_MODEL), F32), pltpu.VMEM((TM_FFN, D_MODEL), F32),
                            pltpu.SemaphoreType.DMA((2,)), pltpu.SemaphoreType.DMA(())],
        ),
        out_shape=jax.ShapeDtypeStruct((rows, D_MODEL), F32),
        compiler_params=_cparams(("arbitrary",)),
        name="moe_scatter",
    )(pos, cnt, xn, fg)


def _combine_kernel(pos_ref, xn_ref, wcol_ref, ys_ref, o_ref, buf, sems, *, n_tokens):
    i = pl.program_id(0)
    slot = i % 2

    def fetch_tile(tile, s):
        base = tile * TM_COMB

        def issue(r, carry):
            for k in range(2):
                src = pos_ref[k * n_tokens + base + r]
                pltpu.make_async_copy(ys_ref.at[pl.ds(src, 1)], buf.at[s, k, pl.ds(r, 1)], sems.at[s]).start()
            return carry

        lax.fori_loop(0, TM_COMB, issue, 0, unroll=8)

    @pl.when(i == 0)
    def _():
        fetch_tile(0, 0)

    @pl.when(i + 1 < pl.num_programs(0))
    def _():
        fetch_tile(i + 1, 1 - slot)

    for k in range(2):
        pltpu.make_async_copy(ys_ref.at[pl.ds(0, TM_COMB)], buf.at[slot, k], sems.at[slot]).wait()
    w = wcol_ref[...]
    o_ref[...] = xn_ref[...] + (w[:, 0:1] * buf[slot, 0] + w[:, 1:2] * buf[slot, 1])


def _combine(pos, xn, wcol, ys):
    n = xn.shape[0]
    return pl.pallas_call(
        functools.partial(_combine_kernel, n_tokens=n),
        grid_spec=pltpu.PrefetchScalarGridSpec(
            num_scalar_prefetch=1,
            grid=(n // TM_COMB,),
            in_specs=[pl.BlockSpec((TM_COMB, D_MODEL), lambda i, p: (i, 0)),
                      pl.BlockSpec((TM_COMB, LANES), lambda i, p: (i, 0)),
                      pl.BlockSpec(memory_space=pl.ANY)],
            out_specs=pl.BlockSpec((TM_COMB, D_MODEL), lambda i, p: (i, 0)),
            scratch_shapes=[pltpu.VMEM((2, 2, TM_COMB, D_MODEL), F32), pltpu.SemaphoreType.DMA((2,))],
        ),
        out_shape=jax.ShapeDtypeStruct((n, D_MODEL), F32),
        compiler_params=_cparams(("arbitrary",)),
        name="moe_combine",
    )(pos, xn, wcol, ys)


def _swiglu(h, wg_ref, wu_ref, wd_ref):
    y = None
    c0 = 0
    for width in FF_CHUNKS:
        cs = slice(c0, c0 + width)
        g = jnp.dot(h, wg_ref[:, cs], preferred_element_type=F32)
        u = jnp.dot(h, wu_ref[:, cs], preferred_element_type=F32)
        act = (g * jax.nn.sigmoid(g) * u).astype(BF16)
        part = jnp.dot(act, wd_ref[cs, :], preferred_element_type=F32)
        y = part if y is None else y + part
        c0 += width
    return y


def _ffn_dense_kernel(h_ref, x_ref, wg_ref, wu_ref, wd_ref, o_ref):
    o_ref[...] = x_ref[...] + _swiglu(h_ref[...], wg_ref, wu_ref, wd_ref)


def _ffn_dense(h2, x2, wgu, wd):
    n = h2.shape[0]
    return pl.pallas_call(
        _ffn_dense_kernel,
        grid=(n // TM_FFN,),
        in_specs=[
            pl.BlockSpec((TM_FFN, D_MODEL), lambda i: (i, 0)),
            pl.BlockSpec((TM_FFN, D_MODEL), lambda i: (i, 0)),
            pl.BlockSpec((D_MODEL, FF), lambda i: (0, 0)),
            pl.BlockSpec((D_MODEL, FF), lambda i: (0, 1)),
            pl.BlockSpec((FF, D_MODEL), lambda i: (0, 0)),
        ],
        out_specs=pl.BlockSpec((TM_FFN, D_MODEL), lambda i: (i, 0)),
        out_shape=jax.ShapeDtypeStruct((n, D_MODEL), F32),
        compiler_params=_cparams(("parallel",)),
        name="ffn_dense",
    )(h2, x2, wgu, wgu, wd)


def _tile_plan(i, cnt_ref):
    end = jnp.int32(0)
    expert = jnp.int32(0)
    for e in range(N_EXPERTS):
        end = end + _round_up_tile(cnt_ref[e])
        expert = expert + (end <= i * TM_FFN).astype(jnp.int32)
    return jnp.minimum(expert, N_EXPERTS - 1), end // TM_FFN


def _ffn_grouped_kernel(cnt_ref, xs_ref, wg_ref, wu_ref, wd_ref, o_ref):
    i = pl.program_id(0)
    _, used = _tile_plan(i, cnt_ref)

    @pl.when(i < used)
    def _():
        o_ref[...] = _swiglu(xs_ref[...].astype(BF16), wg_ref, wu_ref, wd_ref)

    @pl.when(i >= used)
    def _():
        o_ref[...] = jnp.zeros_like(o_ref)


def _ffn_grouped(cnt, xs, wgu, wd):
    rows = xs.shape[0]

    def tile(i, c):
        return jnp.minimum(i, _tile_plan(i, c)[1] - 1)

    return pl.pallas_call(
        _ffn_grouped_kernel,
        grid_spec=pltpu.PrefetchScalarGridSpec(
            num_scalar_prefetch=1,
            grid=(rows // TM_FFN,),
            in_specs=[
                pl.BlockSpec((TM_FFN, D_MODEL), lambda i, c: (tile(i, c), 0)),
                pl.BlockSpec((None, D_MODEL, FF), lambda i, c: (_tile_plan(i, c)[0], 0, 0)),
                pl.BlockSpec((None, D_MODEL, FF), lambda i, c: (_tile_plan(i, c)[0], 0, 1)),
                pl.BlockSpec((None, FF, D_MODEL), lambda i, c: (_tile_plan(i, c)[0], 0, 0)),
            ],
            out_specs=pl.BlockSpec((TM_FFN, D_MODEL), lambda i, c: (i, 0)),
        ),
        out_shape=jax.ShapeDtypeStruct((rows, D_MODEL), F32),
        compiler_params=_cparams(("arbitrary",)),
        name="ffn_grouped",
    )(cnt, xs, wgu, wgu, wd)


def _moe(xn, imeta, wcol, cnt_lanes, fg, wgu, wd):
    n = xn.shape[0]
    cnt = cnt_lanes[:, 0]
    pos = _positions(cnt, imeta).reshape(2 * n)
    xs = _scatter(pos, cnt, xn, fg)
    ys = _ffn_grouped(cnt, xs, wgu, wd)
    return _combine(pos, xn, wcol, ys)


def _rope_tables(seq):
    pos = jnp.arange(seq, dtype=F32)
    inv_freq = ROPE_THETA ** (-jnp.arange(0, QK_ROPE, 2, dtype=F32) / QK_ROPE)
    ang = pos[:, None] * inv_freq[None, :]
    return jnp.cos(ang), jnp.sin(ang)


def _swap_halves(w):
    half = QK_ROPE // 2
    return jnp.concatenate([w[..., half:], w[..., :half]], axis=-1)


def _head_slots(w, width):
    k = w.shape[0]
    w3 = w.reshape(k, MLA_HEADS, width)
    return jnp.pad(w3, ((0, 0), (0, 0), (0, HEAD_SLOT - width))).reshape(k, QK_WIDTH)


def _pad_lanes(v, width=LANES):
    return jnp.pad(v, (0, width - v.shape[0])).reshape(1, width)


def kernel(x, mix_norm_g, w_in, gm_v_norm_g, gm_v_norm_b, gm_w_spatial, gm_b_spatial, gm_w_proj, mla_q_lat_g, mla_w_uq, mla_kv_lat_g, mla_w_ukv, mla_q_norm_g, mla_k_norm_g, mla_w_proj, w_out, ffn_norm_g, dense_w_gu, dense_w_down, moe_w_router, moe_b_router, moe_w_gu, moe_w_down):
    batch, seq, d = x.shape
    n = batch * seq
    depth = w_in.shape[0]
    cos, sin = _rope_tables(seq)
    x2 = x.reshape(n, d)
    q_scale = float(np.log2(np.e) / np.sqrt(QK_DIM))

    o_kr = 2 * GM_WIDTH + Q_LORA + KV_LORA
    o_gate = o_kr + QK_ROPE

    ones_nope, zeros_nope = jnp.ones((seq, QK_NOPE), F32), jnp.zeros((seq, QK_NOPE), F32)
    cc, ss = jnp.concatenate([cos, cos], axis=1), jnp.concatenate([-sin, sin], axis=1)
    tq = jnp.concatenate([ones_nope, cc, ss], axis=1)
    ta = jnp.concatenate([zeros_nope, cc, cc], axis=1)
    tb = jnp.concatenate([zeros_nope, ss, ss], axis=1)

    for l in range(depth):
        wl = w_in[l]
        w_kr = wl[:, o_kr:o_gate]
        w_krx = jnp.pad(jnp.concatenate([w_kr, w_kr], axis=1), ((0, 0), (QK_NOPE, 0)))
        w_krs = jnp.pad(jnp.concatenate([_swap_halves(w_kr)] * 2, axis=1), ((0, 0), (QK_NOPE, 0)))
        w_kr2 = jnp.concatenate([w_krx, w_krs], axis=1).astype(BF16)
        wq3 = mla_w_uq[l].reshape(Q_LORA, MLA_HEADS, QK_DIM)
        wq = jnp.concatenate([wq3, _swap_halves(wq3[:, :, QK_NOPE:])], axis=2).reshape(Q_LORA, QK_WIDTH).astype(BF16)
        wkv3 = mla_w_ukv[l].reshape(KV_LORA, MLA_HEADS, QK_NOPE + V_DIM)
        wk = _head_slots(wkv3[:, :, :QK_NOPE].reshape(KV_LORA, MLA_HEADS * QK_NOPE), QK_NOPE).astype(BF16)
        wvt = wkv3[:, :, QK_NOPE:].reshape(KV_LORA, V_WIDTH).T.astype(BF16)
        gq, gk = mla_q_norm_g[l], mla_k_norm_g[l]
        g12 = lambda g: g[QK_NOPE:]
        g21 = lambda g: _swap_halves(g[QK_NOPE:])
        gq_row = (jnp.concatenate([gq[:QK_NOPE], g12(gq), g21(gq)]) * q_scale).reshape(1, HEAD_SLOT)
        ga_row = _pad_lanes(jnp.concatenate([jnp.zeros((QK_NOPE,), F32), g12(gk), g12(gk)]))
        gb_row = _pad_lanes(jnp.concatenate([jnp.zeros((QK_NOPE,), F32), g21(gk), g21(gk)]))
        uv, gates, q, k, vt = _in_qkv(x2, mix_norm_g[l].reshape(1, d),
                                      wl[:, :o_kr].astype(BF16), w_kr2, wl[:, o_gate:].astype(BF16),
                                      mla_q_lat_g[l].reshape(1, Q_LORA), wq,
                                      mla_kv_lat_g[l].reshape(1, KV_LORA), wk, wvt,
                                      tq, ta, tb, gq_row, ga_row, gb_row, _pad_lanes(gk[:QK_NOPE]), batch, seq)
        slots = lambda t: t.reshape(batch, seq, QK_WIDTH)
        ot = _attention(slots(q), slots(k), vt).reshape(n, V_WIDTH)

        is_moe = l % 2 == 1
        m = l // 2
        router = {}
        if is_moe:
            router = dict(wr=jnp.pad(moe_w_router[m], ((0, 0), (0, LANES - N_EXPERTS))),
                          br=_pad_lanes(moe_b_router[m]))
        outs = _merge(uv, gm_v_norm_g[l].reshape(1, GM_WIDTH), gm_v_norm_b[l].reshape(1, GM_WIDTH),
                      gm_w_spatial[l], gm_b_spatial[l].T, ot, gates, x2,
                      gm_w_proj[l].astype(BF16), mla_w_proj[l].astype(BF16),
                      w_out[l].astype(BF16), ffn_norm_g[l].reshape(1, d), **router)
        if is_moe:
            xn, imeta, wcol, cnt_lanes = outs
            x2 = _moe(xn, imeta, wcol, cnt_lanes, ffn_norm_g[l].reshape(1, d),
                      moe_w_gu[m].astype(BF16), moe_w_down[m].astype(BF16))
        else:
            xn, h2 = outs
            x2 = _ffn_dense(h2, xn, dense_w_gu[m].astype(BF16), dense_w_down[m].astype(BF16))
    return x2.reshape(batch, seq, d)
```

```python
import functools

import jax
import jax.numpy as jnp
import numpy as np
from jax import lax
from jax.experimental import pallas as pl
from jax.experimental.pallas import tpu as pltpu

F32 = jnp.float32
BF16 = jnp.bfloat16

EPS = 1e-6
LANES = 128

D_MODEL = 1024
GM_GROUPS = 8
GM_GROUP_CH = 128
GM_WIDTH = 1024
GM_CHUNK = 128
MLA_HEADS = 16
QK_NOPE = 64
QK_ROPE = 32
QK_DIM = 96
V_DIM = 64
Q_LORA = 512
KV_LORA = 256
ROPE_THETA = 10000.0
HEAD_SLOT = LANES
QK_WIDTH = MLA_HEADS * HEAD_SLOT
V_WIDTH = MLA_HEADS * V_DIM
N_EXPERTS = 8
FF = 2816

TM_IN = 256
TM_MERGE = 512
TM_FFN = 512
FF_CHUNKS = (768, 768, 768, 512)
TM_POS = 2048
TM_SCAT = 1024
TM_COMB = 1024
ATT_T = 512
ATT_G = 2
ATT_SUM_ROWS = 16

VMEM_LIMIT = 56 * 1024 * 1024


def _cparams(sem):
    return pltpu.CompilerParams(dimension_semantics=sem, vmem_limit_bytes=VMEM_LIMIT)


def _rms(xf, g):
    return xf * lax.rsqrt(jnp.mean(xf * xf, axis=-1, keepdims=True) + EPS) * g


def _in_qkv_kernel(x_ref, g_ref, w_ref, wkr_ref, wgate_ref, qg_ref, wq_ref, kvg_ref, wk_ref, wvt_ref,
                   tq_ref, ta_ref, tb_ref, gq_ref, ga_ref, gb_ref, gkn_ref,
                   uv_ref, gate_ref, q_ref, k_ref, vt_ref):
    h = _rms(x_ref[...], g_ref[...]).astype(BF16)
    o_lat = 2 * GM_WIDTH
    lat = jnp.dot(h, w_ref[:, o_lat:], preferred_element_type=F32)
    krx = jnp.dot(h, wkr_ref[...], preferred_element_type=F32)
    uv_ref[...] = _gelu_tanh(jnp.dot(h, w_ref[:, :o_lat], preferred_element_type=F32)).astype(BF16)
    gate_ref[...] = jax.nn.sigmoid(jnp.dot(h, wgate_ref[...], preferred_element_type=F32)).astype(BF16)

    cq = _rms(lat[:, :Q_LORA], qg_ref[...]).astype(BF16)
    ckv = _rms(lat[:, Q_LORA:], kvg_ref[...]).astype(BF16)
    kr = krx[:, :LANES]
    kr_sw = krx[:, LANES:]
    kr_ssq = 0.5 * jnp.sum(kr * kr, axis=-1, keepdims=True)
    kr_rot = kr * (ta_ref[...] * ga_ref[...]) + kr_sw * (tb_ref[...] * gb_ref[...])
    tq, gkn = tq_ref[...] * gq_ref[...], gkn_ref[...]
    q_all = jnp.dot(cq, wq_ref[...], preferred_element_type=F32)
    k_all = jnp.dot(ckv, wk_ref[...], preferred_element_type=F32)
    first_copy = lax.broadcasted_iota(jnp.int32, (1, HEAD_SLOT), 1) < QK_DIM
    for hd in range(MLA_HEADS):
        sl = slice(hd * HEAD_SLOT, (hd + 1) * HEAD_SLOT)
        qh = q_all[:, sl]
        ssq = jnp.sum(qh * jnp.where(first_copy, qh, 0.0), axis=-1, keepdims=True)
        q_ref[:, sl] = (qh * tq * lax.rsqrt(ssq * (1.0 / QK_DIM) + EPS)).astype(BF16)
        kh = k_all[:, sl]
        rk = lax.rsqrt((jnp.sum(kh * kh, axis=-1, keepdims=True) + kr_ssq) * (1.0 / QK_DIM) + EPS)
        k_ref[:, sl] = ((kh * gkn + kr_rot) * rk).astype(BF16)
    vt_ref[...] = lax.dot_general(wvt_ref[...], ckv, (((1,), (1,)), ((), ())),
                                  preferred_element_type=F32).astype(BF16)


def _in_qkv(x2, g, w_main, w_kr, w_gate, qg, wq, kvg, wk, wvt, tq, ta, tb, gq, ga, gb, gkn, batch, seq):
    n = x2.shape[0]
    tps = seq // TM_IN
    full = lambda shape: pl.BlockSpec(shape, lambda i: (0,) * len(shape), pipeline_mode=pl.Buffered(1))
    rope = pl.BlockSpec((TM_IN, HEAD_SLOT), lambda i: (i % tps, 0))
    row = lambda w: pl.BlockSpec((TM_IN, w), lambda i: (i, 0))
    lane_row = full((1, HEAD_SLOT))
    return pl.pallas_call(
        _in_qkv_kernel,
        grid=(n // TM_IN,),
        in_specs=[
            row(D_MODEL), full((1, D_MODEL)), full(w_main.shape), full(w_kr.shape), full(w_gate.shape),
            full((1, Q_LORA)), full((Q_LORA, QK_WIDTH)),
            full((1, KV_LORA)), full((KV_LORA, QK_WIDTH)), full((V_WIDTH, KV_LORA)),
            rope, rope, rope, lane_row, lane_row, lane_row, lane_row,
        ],
        out_specs=[row(2 * GM_WIDTH), row(2 * D_MODEL), row(QK_WIDTH), row(QK_WIDTH),
                   pl.BlockSpec((None, None, V_WIDTH, TM_IN), lambda i: (i // tps, i % tps, 0, 0))],
        out_shape=[jax.ShapeDtypeStruct((n, 2 * GM_WIDTH), BF16), jax.ShapeDtypeStruct((n, 2 * D_MODEL), BF16),
                   jax.ShapeDtypeStruct((n, QK_WIDTH), BF16), jax.ShapeDtypeStruct((n, QK_WIDTH), BF16),
                   jax.ShapeDtypeStruct((batch, tps, V_WIDTH, TM_IN), BF16)],
        compiler_params=_cparams(("parallel",)),
        name="in_qkv",
    )(x2, g, w_main, w_kr, w_gate, qg, wq, kvg, wk, wvt, tq, ta, tb, gq, ga, gb, gkn)


def _attention_kernel(q_ref, k_ref, vt_ref, o_ref, *, seq):
    nblk = seq // ATT_T
    vblocks = ATT_T // TM_IN
    key_pos = lax.broadcasted_iota(jnp.int32, (ATT_T, ATT_T), 0)
    qry_pos = lax.broadcasted_iota(jnp.int32, (ATT_T, ATT_T), 1)
    causal = key_pos <= qry_pos
    nt = (((1,), (1,)), ((), ()))
    ones_rows = jnp.ones((ATT_SUM_ROWS, ATT_T), BF16)

    def update(state, g, st, kj, diagonal):
        if diagonal:
            st = jnp.where(causal, st, -1e30)
        m_blk = jnp.max(st, axis=0, keepdims=True)
        m_new = m_blk if state is None else jnp.maximum(state[0], m_blk)
        p = jnp.exp2((st - m_new).astype(BF16))
        vt = [vt_ref[kj * vblocks + b, g * V_DIM:(g + 1) * V_DIM, :] for b in range(vblocks)]
        v_aug = jnp.concatenate([jnp.concatenate(vt, axis=1), ones_rows], axis=0)
        pv = jnp.dot(v_aug, p, preferred_element_type=F32)
        if state is None:
            return m_new, pv
        return m_new, jnp.exp2(state[0] - m_new) * state[1] + pv

    for qi in range(nblk):
        q0, q1 = qi * ATT_T, (qi + 1) * ATT_T
        strips = [lax.dot_general(k_ref[0:q1, g * HEAD_SLOT:(g + 1) * HEAD_SLOT],
                                  q_ref[q0:q1, g * HEAD_SLOT:(g + 1) * HEAD_SLOT], nt,
                                  preferred_element_type=F32) for g in range(ATT_G)]
        acc_rows = V_DIM + ATT_SUM_ROWS
        states = [None] * ATT_G
        for kj in range(qi + 1):
            states = [update(states[g], g, strips[g][kj * ATT_T:(kj + 1) * ATT_T], kj, kj == qi)
                      for g in range(ATT_G)]
        pad_rows = jnp.zeros((HEAD_SLOT - acc_rows, ATT_T), F32)
        for g in range(ATT_G):
            acc = jnp.concatenate([states[g][1], pad_rows], axis=0).T
            o_ref[q0:q1, g * V_DIM:(g + 1) * V_DIM] = (acc[:, :V_DIM] / acc[:, V_DIM:V_DIM + 1]).astype(BF16)


def _attention(q3, k3, vt4):
    batch, seq, _ = q3.shape
    slot = pl.BlockSpec((None, seq, ATT_G * HEAD_SLOT), lambda b, h: (b, 0, h))
    return pl.pallas_call(
        functools.partial(_attention_kernel, seq=seq),
        grid=(batch, MLA_HEADS // ATT_G),
        in_specs=[slot, slot,
                  pl.BlockSpec((None, seq // TM_IN, ATT_G * V_DIM, TM_IN), lambda b, h: (b, 0, h, 0))],
        out_specs=pl.BlockSpec((None, seq, ATT_G * V_DIM), lambda b, h: (b, 0, h)),
        out_shape=jax.ShapeDtypeStruct((batch, seq, V_WIDTH), BF16),
        compiler_params=_cparams(("parallel", "parallel")),
        name="attention",
    )(q3, k3, vt4)


def _split_bf16(v):
    hi = v.astype(BF16)
    return hi, (v - hi.astype(F32)).astype(BF16)


def _gelu_tanh(x):
    c1 = -2.0 * float(np.sqrt(2.0 / np.pi) * np.log2(np.e))
    c2 = c1 * 0.044715
    return x / (1.0 + jnp.exp2(x * (c1 + c2 * (x * x))))


def _gmlp_tile(uv_ref, vg_ref, vb_ref, ws_ref, bst_ref, a_ref):
    v = uv_ref[:, GM_WIDTH:].astype(F32)
    mu = jnp.mean(v, axis=-1, keepdims=True)
    vc = v - mu
    vn = (vc * lax.rsqrt(jnp.mean(vc * vc, axis=-1, keepdims=True) + EPS) * vg_ref[...] + vb_ref[...]).astype(BF16)
    row = lax.broadcasted_iota(jnp.int32, (GM_CHUNK, GM_CHUNK), 0)
    col = lax.broadcasted_iota(jnp.int32, (GM_CHUNK, GM_CHUNK), 1)
    tril = col <= row
    bst = bst_ref[...]
    for g in range(GM_GROUPS):
        ws = jnp.where(tril, ws_ref[g], 0.0).astype(BF16)
        bias = bst[:, g:g + 1]
        cs = slice(g * GM_GROUP_CH, (g + 1) * GM_GROUP_CH)
        for c in range(TM_MERGE // GM_CHUNK):
            rs = slice(c * GM_CHUNK, (c + 1) * GM_CHUNK)
            mixed = jnp.dot(ws, vn[rs, cs], preferred_element_type=F32) + bias
            a_ref[rs, cs] = (uv_ref[rs, cs].astype(F32) * mixed).astype(BF16)


def _merge_kernel(uv_ref, vg_ref, vb_ref, ws_ref, bst_ref, ot_ref, gate_ref, x_ref, wa_ref, wb_ref, wo_ref,
                  fg_ref, *rest, with_router):
    if with_router:
        wr_ref, br_ref, xn_ref, imeta_ref, wcol_ref, cnt_ref, a_ref, run_ref = rest
    else:
        xn_ref, h2_ref, a_ref = rest
    yb = jnp.dot(ot_ref[...], wb_ref[...], preferred_element_type=F32)
    _gmlp_tile(uv_ref, vg_ref, vb_ref, ws_ref, bst_ref, a_ref)
    ya = jnp.dot(a_ref[...], wa_ref[...], preferred_element_type=F32)
    ga = gate_ref[:, :D_MODEL].astype(F32)
    gb = gate_ref[:, D_MODEL:].astype(F32)
    merged = (ga * ya + gb * yb).astype(BF16)
    xn = x_ref[...] + jnp.dot(merged, wo_ref[...], preferred_element_type=F32)
    xn_ref[...] = xn
    h2 = _rms(xn, fg_ref[...])
    if with_router:
        h_hi, h_lo = _split_bf16(h2)
        w_hi, w_lo = _split_bf16(wr_ref[...])
        hh_hl = jnp.dot(h_hi, jnp.concatenate([w_hi, w_lo], axis=1), preferred_element_type=F32)
        logits = hh_hl[:, :LANES] + (hh_hl[:, LANES:] + jnp.dot(h_lo, w_hi, preferred_element_type=F32))
        _route_tile(logits + br_ref[...], run_ref, imeta_ref, wcol_ref, cnt_ref)
    else:
        h2_ref[...] = h2.astype(BF16)


def _merge(uv, vg, vb, ws, bst, ot, gates, x2, wa, wb, wo, fg, wr=None, br=None):
    n = x2.shape[0]
    with_router = wr is not None
    full = lambda shape: pl.BlockSpec(shape, lambda i: (0,) * len(shape))
    row = lambda w: pl.BlockSpec((TM_MERGE, w), lambda i: (i, 0))
    in_specs = [
        row(2 * GM_WIDTH), full((1, GM_WIDTH)), full((1, GM_WIDTH)),
        full((GM_GROUPS, GM_CHUNK, GM_CHUNK)), full((GM_CHUNK, GM_GROUPS)),
        row(V_WIDTH), row(2 * D_MODEL), row(D_MODEL),
        full((GM_WIDTH, D_MODEL)), full((V_WIDTH, D_MODEL)), full((D_MODEL, D_MODEL)), full((1, D_MODEL)),
    ]
    args = [uv, vg, vb, ws, bst, ot, gates, x2, wa, wb, wo, fg]
    scratch = [pltpu.VMEM((TM_MERGE, GM_WIDTH), BF16)]
    if with_router:
        in_specs += [full((D_MODEL, LANES)), full((1, LANES))]
        out_specs = [row(D_MODEL), pl.BlockSpec((N_EXPERTS, TM_MERGE), lambda i: (0, i)), row(LANES),
                     full((N_EXPERTS, LANES))]
        out_shape = [jax.ShapeDtypeStruct((n, D_MODEL), F32), jax.ShapeDtypeStruct((N_EXPERTS, n), jnp.int32),
                     jax.ShapeDtypeStruct((n, LANES), F32), jax.ShapeDtypeStruct((N_EXPERTS, LANES), jnp.int32)]
        args += [wr, br]
        scratch.append(pltpu.VMEM((N_EXPERTS, LANES), F32))
    else:
        out_specs = [row(D_MODEL), row(D_MODEL)]
        out_shape = [jax.ShapeDtypeStruct((n, D_MODEL), F32), jax.ShapeDtypeStruct((n, D_MODEL), BF16)]
    return pl.pallas_call(
        functools.partial(_merge_kernel, with_router=with_router),
        grid=(n // TM_MERGE,),
        in_specs=in_specs, out_specs=out_specs, out_shape=out_shape,
        scratch_shapes=scratch,
        compiler_params=_cparams(("arbitrary",) if with_router else ("parallel",)),
        name="merge_router" if with_router else "merge",
    )(*args)


def _round_up_tile(c):
    return (c + (TM_FFN - 1)) & (-TM_FFN)


def _route_tile(logits, run_ref, imeta_ref, wcol_ref, cnt_ref):
    tm = logits.shape[0]

    @pl.when(pl.program_id(0) == 0)
    def _():
        run_ref[...] = jnp.zeros_like(run_ref)

    lt = logits.T[:N_EXPERTS, :]
    sub = lax.broadcasted_iota(jnp.int32, lt.shape, 0)
    m1 = jnp.max(lt, axis=0, keepdims=True)
    i1 = jnp.min(jnp.where(lt == m1, sub, N_EXPERTS), axis=0, keepdims=True)
    rest = jnp.where(sub == i1, -jnp.inf, lt)
    m2 = jnp.max(rest, axis=0, keepdims=True)
    i2 = jnp.min(jnp.where(rest == m2, sub, N_EXPERTS), axis=0, keepdims=True)
    e2 = jnp.exp(m2 - m1)
    w1 = 1.0 / (1.0 + e2)
    w2 = e2 / (1.0 + e2)

    onehot = jnp.where((sub == i1) | (sub == i2), 1.0, 0.0)
    src = lax.broadcasted_iota(jnp.int32, (tm, tm), 0)
    dst = lax.broadcasted_iota(jnp.int32, (tm, tm), 1)
    earlier = jnp.where(src < dst, 1.0, 0.0).astype(BF16)
    seen = jnp.dot(onehot.astype(BF16), earlier, preferred_element_type=F32) + run_ref[:, :1]
    r1 = jnp.sum(jnp.where(sub == i1, seen, 0.0), axis=0, keepdims=True).astype(jnp.int32)
    r2 = jnp.sum(jnp.where(sub == i2, seen, 0.0), axis=0, keepdims=True).astype(jnp.int32)
    imeta_ref[...] = jnp.where(sub == 0, i1, jnp.where(sub == 1, i2, jnp.where(sub == 2, r1, jnp.where(sub == 3, r2, 0))))

    total = run_ref[...] + jnp.sum(onehot, axis=1, keepdims=True)
    run_ref[...] = total
    cnt_ref[...] = total.astype(jnp.int32)

    sub_w = lax.broadcasted_iota(jnp.int32, (LANES, tm), 0)
    wcol_ref[...] = jnp.where(sub_w == 0, w1, jnp.where(sub_w == 1, w2, 0.0)).T


def _positions_kernel(cnt_ref, imeta_ref, pos_ref):
    im = imeta_ref[...]
    i1, i2, r1, r2 = im[0:1], im[1:2], im[2:3], im[3:4]
    start = jnp.int32(0)
    p1 = jnp.zeros_like(i1)
    p2 = jnp.zeros_like(i2)
    for e in range(N_EXPERTS):
        p1 = jnp.where(i1 == e, start, p1)
        p2 = jnp.where(i2 == e, start, p2)
        start = start + _round_up_tile(cnt_ref[e])
    pos_ref[...] = jnp.concatenate([p1 + r1, p2 + r2], axis=0)


def _positions(cnt, imeta):
    n = imeta.shape[1]
    return pl.pallas_call(
        _positions_kernel,
        grid_spec=pltpu.PrefetchScalarGridSpec(
            num_scalar_prefetch=1,
            grid=(n // TM_POS,),
            in_specs=[pl.BlockSpec((N_EXPERTS, TM_POS), lambda i, c: (0, i))],
            out_specs=pl.BlockSpec((2, TM_POS), lambda i, c: (0, i)),
        ),
        out_shape=jax.ShapeDtypeStruct((2, n), jnp.int32),
        compiler_params=_cparams(("parallel",)),
        name="positions",
    )(cnt, imeta)


def _scatter_kernel(pos_ref, cnt_ref, xn_ref, fg_ref, xs_ref, hbuf, zbuf, sems, zsem, *, n_tokens):
    n_tiles = xs_ref.shape[0] // TM_FFN

    @pl.when(pl.program_id(0) == 0)
    def _():
        zbuf[...] = jnp.zeros_like(zbuf)

        def zero_tile(t):
            rows = pl.ds(pl.multiple_of(t * TM_FFN, TM_FFN), TM_FFN)
            return pltpu.make_async_copy(zbuf, xs_ref.at[rows], zsem)

        def each_zero_tile(action):
            end = jnp.int32(0)
            for e in range(N_EXPERTS):
                tiles = _round_up_tile(cnt_ref[e]) // TM_FFN
                end = end + tiles

                @pl.when(tiles > 0)
                def _(end=end):
                    action(zero_tile(end - 1))

            def tail(t, carry):
                action(zero_tile(t))
                return carry

            lax.fori_loop(end, n_tiles, tail, 0)

        each_zero_tile(lambda copy: copy.start())
        each_zero_tile(lambda copy: copy.wait())

    i = pl.program_id(0)
    last = pl.num_programs(0) - 1
    slot = i % 2

    def drain(s):
        for k in range(2):
            pltpu.make_async_copy(hbuf.at[s], xs_ref.at[pl.ds(0, TM_SCAT)], sems.at[s]).wait()

    @pl.when(i >= 2)
    def _():
        drain(slot)

    base = i * TM_SCAT
    hbuf[slot] = _rms(xn_ref[...], fg_ref[...])

    def issue(r, carry):
        for k in range(2):
            dst = pos_ref[k * n_tokens + base + r]
            pltpu.make_async_copy(hbuf.at[slot, pl.ds(r, 1)], xs_ref.at[pl.ds(dst, 1)], sems.at[slot]).start()
        return carry

    lax.fori_loop(0, TM_SCAT, issue, 0, unroll=8)

    @pl.when(i == last)
    def _():
        drain(1 - slot)
        drain(slot)


def _scatter(pos, cnt, xn, fg):
    n = xn.shape[0]
    assert n // TM_SCAT >= 2
    rows = 2 * n + N_EXPERTS * TM_FFN
    return pl.pallas_call(
        functools.partial(_scatter_kernel, n_tokens=n),
        grid_spec=pltpu.PrefetchScalarGridSpec(
            num_scalar_prefetch=2,
            grid=(n // TM_SCAT,),
            in_specs=[pl.BlockSpec((TM_SCAT, D_MODEL), lambda i, p, c: (i, 0)),
                      pl.BlockSpec((1, D_MODEL), lambda i, p, c: (0, 0))],
            out_specs=pl.BlockSpec(memory_space=pl.ANY),
            scratch_shapes=[pltpu.VMEM((2, TM_SCAT, D_MODEL), F32), pltpu.VMEM((TM_FFN, D_MODEL), F32),
                            pltpu.SemaphoreType.DMA((2,)), pltpu.SemaphoreType.DMA(())],
        ),
        out_shape=jax.ShapeDtypeStruct((rows, D_MODEL), F32),
        compiler_params=_cparams(("arbitrary",)),
        name="moe_scatter",
    )(pos, cnt, xn, fg)


def _combine_kernel(pos_ref, xn_ref, wcol_ref, ys_ref, o_ref, buf, sems, *, n_tokens):
    i = pl.program_id(0)
    slot = i % 2

    def fetch_tile(tile, s):
        base = tile * TM_COMB

        def issue(r, carry):
            for k in range(2):
                src = pos_ref[k * n_tokens + base + r]
                pltpu.make_async_copy(ys_ref.at[pl.ds(src, 1)], buf.at[s, k, pl.ds(r, 1)], sems.at[s]).start()
            return carry

        lax.fori_loop(0, TM_COMB, issue, 0, unroll=8)

    @pl.when(i == 0)
    def _():
        fetch_tile(0, 0)

    @pl.when(i + 1 < pl.num_programs(0))
    def _():
        fetch_tile(i + 1, 1 - slot)

    for k in range(2):
        pltpu.make_async_copy(ys_ref.at[pl.ds(0, TM_COMB)], buf.at[slot, k], sems.at[slot]).wait()
    w = wcol_ref[...]
    o_ref[...] = xn_ref[...] + (w[:, 0:1] * buf[slot, 0] + w[:, 1:2] * buf[slot, 1])


def _combine(pos, xn, wcol, ys):
    n = xn.shape[0]
    return pl.pallas_call(
        functools.partial(_combine_kernel, n_tokens=n),
        grid_spec=pltpu.PrefetchScalarGridSpec(
            num_scalar_prefetch=1,
            grid=(n // TM_COMB,),
            in_specs=[pl.BlockSpec((TM_COMB, D_MODEL), lambda i, p: (i, 0)),
                      pl.BlockSpec((TM_COMB, LANES), lambda i, p: (i, 0)),
                      pl.BlockSpec(memory_space=pl.ANY)],
            out_specs=pl.BlockSpec((TM_COMB, D_MODEL), lambda i, p: (i, 0)),
            scratch_shapes=[pltpu.VMEM((2, 2, TM_COMB, D_MODEL), F32), pltpu.SemaphoreType.DMA((2,))],
        ),
        out_shape=jax.ShapeDtypeStruct((n, D_MODEL), F32),
        compiler_params=_cparams(("arbitrary",)),
        name="moe_combine",
    )(pos, xn, wcol, ys)


def _swiglu(h, wg_ref, wu_ref, wd_ref):
    y = None
    c0 = 0
    for width in FF_CHUNKS:
        cs = slice(c0, c0 + width)
        g = jnp.dot(h, wg_ref[:, cs], preferred_element_type=F32)
        u = jnp.dot(h, wu_ref[:, cs], preferred_element_type=F32)
        act = (g * jax.nn.sigmoid(g) * u).astype(BF16)
        part = jnp.dot(act, wd_ref[cs, :], preferred_element_type=F32)
        y = part if y is None else y + part
        c0 += width
    return y


def _ffn_dense_kernel(h_ref, x_ref, wg_ref, wu_ref, wd_ref, o_ref):
    o_ref[...] = x_ref[...] + _swiglu(h_ref[...], wg_ref, wu_ref, wd_ref)


def _ffn_dense(h2, x2, wgu, wd):
    n = h2.shape[0]
    return pl.pallas_call(
        _ffn_dense_kernel,
        grid=(n // TM_FFN,),
        in_specs=[
            pl.BlockSpec((TM_FFN, D_MODEL), lambda i: (i, 0)),
            pl.BlockSpec((TM_FFN, D_MODEL), lambda i: (i, 0)),
            pl.BlockSpec((D_MODEL, FF), lambda i: (0, 0)),
            pl.BlockSpec((D_MODEL, FF), lambda i: (0, 1)),
            pl.BlockSpec((FF, D_MODEL), lambda i: (0, 0)),
        ],
        out_specs=pl.BlockSpec((TM_FFN, D_MODEL), lambda i: (i, 0)),
        out_shape=jax.ShapeDtypeStruct((n, D_MODEL), F32),
        compiler_params=_cparams(("parallel",)),
        name="ffn_dense",
    )(h2, x2, wgu, wgu, wd)


def _tile_plan(i, cnt_ref):
    end = jnp.int32(0)
    expert = jnp.int32(0)
    for e in range(N_EXPERTS):
        end = end + _round_up_tile(cnt_ref[e])
        expert = expert + (end <= i * TM_FFN).astype(jnp.int32)
    return jnp.minimum(expert, N_EXPERTS - 1), end // TM_FFN


def _ffn_grouped_kernel(cnt_ref, xs_ref, wg_ref, wu_ref, wd_ref, o_ref):
    i = pl.program_id(0)
    _, used = _tile_plan(i, cnt_ref)

    @pl.when(i < used)
    def _():
        o_ref[...] = _swiglu(xs_ref[...].astype(BF16), wg_ref, wu_ref, wd_ref)

    @pl.when(i >= used)
    def _():
        o_ref[...] = jnp.zeros_like(o_ref)


def _ffn_grouped(cnt, xs, wgu, wd):
    rows = xs.shape[0]

    def tile(i, c):
        return jnp.minimum(i, _tile_plan(i, c)[1] - 1)

    return pl.pallas_call(
        _ffn_grouped_kernel,
        grid_spec=pltpu.PrefetchScalarGridSpec(
            num_scalar_prefetch=1,
            grid=(rows // TM_FFN,),
            in_specs=[
                pl.BlockSpec((TM_FFN, D_MODEL), lambda i, c: (tile(i, c), 0)),
                pl.BlockSpec((None, D_MODEL, FF), lambda i, c: (_tile_plan(i, c)[0], 0, 0)),
                pl.BlockSpec((None, D_MODEL, FF), lambda i, c: (_tile_plan(i, c)[0], 0, 1)),
                pl.BlockSpec((None, FF, D_MODEL), lambda i, c: (_tile_plan(i, c)[0], 0, 0)),
            ],
            out_specs=pl.BlockSpec((TM_FFN, D_MODEL), lambda i, c: (i, 0)),
        ),
        out_shape=jax.ShapeDtypeStruct((rows, D_MODEL), F32),
        compiler_params=_cparams(("arbitrary",)),
        name="ffn_grouped",
    )(cnt, xs, wgu, wgu, wd)


def _moe(xn, imeta, wcol, cnt_lanes, fg, wgu, wd):
    n = xn.shape[0]
    cnt = cnt_lanes[:, 0]
    pos = _positions(cnt, imeta).reshape(2 * n)
    xs = _scatter(pos, cnt, xn, fg)
    ys = _ffn_grouped(cnt, xs, wgu, wd)
    return _combine(pos, xn, wcol, ys)


def _rope_tables(seq):
    pos = jnp.arange(seq, dtype=F32)
    inv_freq = ROPE_THETA ** (-jnp.arange(0, QK_ROPE, 2, dtype=F32) / QK_ROPE)
    ang = pos[:, None] * inv_freq[None, :]
    return jnp.cos(ang), jnp.sin(ang)


def _swap_halves(w):
    half = QK_ROPE // 2
    return jnp.concatenate([w[..., half:], w[..., :half]], axis=-1)


def _head_slots(w, width):
    k = w.shape[0]
    w3 = w.reshape(k, MLA_HEADS, width)
    return jnp.pad(w3, ((0, 0), (0, 0), (0, HEAD_SLOT - width))).reshape(k, QK_WIDTH)


def _pad_lanes(v, width=LANES):
    return jnp.pad(v, (0, width - v.shape[0])).reshape(1, width)


def kernel(x, mix_norm_g, w_in, gm_v_norm_g, gm_v_norm_b, gm_w_spatial, gm_b_spatial, gm_w_proj, mla_q_lat_g, mla_w_uq, mla_kv_lat_g, mla_w_ukv, mla_q_norm_g, mla_k_norm_g, mla_w_proj, w_out, ffn_norm_g, dense_w_gu, dense_w_down, moe_w_router, moe_b_router, moe_w_gu, moe_w_down):
    batch, seq, d = x.shape
    n = batch * seq
    depth = w_in.shape[0]
    cos, sin = _rope_tables(seq)
    x2 = x.reshape(n, d)
    q_scale = float(np.log2(np.e) / np.sqrt(QK_DIM))

    o_kr = 2 * GM_WIDTH + Q_LORA + KV_LORA
    o_gate = o_kr + QK_ROPE

    ones_nope, zeros_nope = jnp.ones((seq, QK_NOPE), F32), jnp.zeros((seq, QK_NOPE), F32)
    cc, ss = jnp.concatenate([cos, cos], axis=1), jnp.concatenate([-sin, sin], axis=1)
    tq = jnp.concatenate([ones_nope, cc, ss], axis=1)
    ta = jnp.concatenate([zeros_nope, cc, cc], axis=1)
    tb = jnp.concatenate([zeros_nope, ss, ss], axis=1)

    for l in range(depth):
        wl = w_in[l]
        w_kr = wl[:, o_kr:o_gate]
        w_krx = jnp.pad(jnp.concatenate([w_kr, w_kr], axis=1), ((0, 0), (QK_NOPE, 0)))
        w_krs = jnp.pad(jnp.concatenate([_swap_halves(w_kr)] * 2, axis=1), ((0, 0), (QK_NOPE, 0)))
        w_kr2 = jnp.concatenate([w_krx, w_krs], axis=1).astype(BF16)
        wq3 = mla_w_uq[l].reshape(Q_LORA, MLA_HEADS, QK_DIM)
        wq = jnp.concatenate([wq3, _swap_halves(wq3[:, :, QK_NOPE:])], axis=2).reshape(Q_LORA, QK_WIDTH).astype(BF16)
        wkv3 = mla_w_ukv[l].reshape(KV_LORA, MLA_HEADS, QK_NOPE + V_DIM)
        wk = _head_slots(wkv3[:, :, :QK_NOPE].reshape(KV_LORA, MLA_HEADS * QK_NOPE), QK_NOPE).astype(BF16)
        wvt = wkv3[:, :, QK_NOPE:].reshape(KV_LORA, V_WIDTH).T.astype(BF16)
        gq, gk = mla_q_norm_g[l], mla_k_norm_g[l]
        g12 = lambda g: g[QK_NOPE:]
        g21 = lambda g: _swap_halves(g[QK_NOPE:])
        gq_row = (jnp.concatenate([gq[:QK_NOPE], g12(gq), g21(gq)]) * q_scale).reshape(1, HEAD_SLOT)
        ga_row = _pad_lanes(jnp.concatenate([jnp.zeros((QK_NOPE,), F32), g12(gk), g12(gk)]))
        gb_row = _pad_lanes(jnp.concatenate([jnp.zeros((QK_NOPE,), F32), g21(gk), g21(gk)]))
        uv, gates, q, k, vt = _in_qkv(x2, mix_norm_g[l].reshape(1, d),
                                      wl[:, :o_kr].astype(BF16), w_kr2, wl[:, o_gate:].astype(BF16),
                                      mla_q_lat_g[l].reshape(1, Q_LORA), wq,
                                      mla_kv_lat_g[l].reshape(1, KV_LORA), wk, wvt,
                                      tq, ta, tb, gq_row, ga_row, gb_row, _pad_lanes(gk[:QK_NOPE]), batch, seq)
        slots = lambda t: t.reshape(batch, seq, QK_WIDTH)
        ot = _attention(slots(q), slots(k), vt).reshape(n, V_WIDTH)

        is_moe = l % 2 == 1
        m = l // 2
        router = {}
        if is_moe:
            router = dict(wr=jnp.pad(moe_w_router[m], ((0, 0), (0, LANES - N_EXPERTS))),
                          br=_pad_lanes(moe_b_router[m]))
        outs = _merge(uv, gm_v_norm_g[l].reshape(1, GM_WIDTH), gm_v_norm_b[l].reshape(1, GM_WIDTH),
                      gm_w_spatial[l], gm_b_spatial[l].T, ot, gates, x2,
                      gm_w_proj[l].astype(BF16), mla_w_proj[l].astype(BF16),
                      w_out[l].astype(BF16), ffn_norm_g[l].reshape(1, d), **router)
        if is_moe:
            xn, imeta, wcol, cnt_lanes = outs
            x2 = _moe(xn, imeta, wcol, cnt_lanes, ffn_norm_g[l].reshape(1, d),
                      moe_w_gu[m].astype(BF16), moe_w_down[m].astype(BF16))
        else:
            xn, h2 = outs
            x2 = _ffn_dense(h2, xn, dense_w_gu[m].astype(BF16), dense_w_down[m].astype(BF16))
    return x2.reshape(batch, seq, d)
```

```python
import functools

import jax
import jax.numpy as jnp
import numpy as np
from jax import lax
from jax.experimental import pallas as pl
from jax.experimental.pallas import tpu as pltpu

F32 = jnp.float32
BF16 = jnp.bfloat16

EPS = 1e-6
LANES = 128

D_MODEL = 1024
GM_GROUPS = 8
GM_GROUP_CH = 128
GM_WIDTH = 1024
GM_CHUNK = 128
MLA_HEADS = 16
QK_NOPE = 64
QK_ROPE = 32
QK_DIM = 96
V_DIM = 64
Q_LORA = 512
KV_LORA = 256
ROPE_THETA = 10000.0
HEAD_SLOT = LANES
QK_WIDTH = MLA_HEADS * HEAD_SLOT
V_WIDTH = MLA_HEADS * V_DIM
N_EXPERTS = 8
FF = 2816

TM_IN = 256
TM_MERGE = 512
TM_FFN = 512
FF_CHUNKS = (768, 768, 768, 512)
TM_POS = 2048
TM_SCAT = 512
TM_COMB = 512
ATT_T = 256
ATT_G = 2
ATT_SUM_ROWS = 16

VMEM_LIMIT = 56 * 1024 * 1024


def _cparams(sem):
    return pltpu.CompilerParams(dimension_semantics=sem, vmem_limit_bytes=VMEM_LIMIT)


def _rms(xf, g):
    return xf * lax.rsqrt(jnp.mean(xf * xf, axis=-1, keepdims=True) + EPS) * g


def _in_qkv_kernel(x_ref, g_ref, w_ref, wkr_ref, wgate_ref, qg_ref, wq_ref, kvg_ref, wk_ref, wvt_ref,
                   tq_ref, ta_ref, tb_ref, gq_ref, ga_ref, gb_ref, gkn_ref,
                   uv_ref, gate_ref, q_ref, k_ref, vt_ref):
    h = _rms(x_ref[...], g_ref[...]).astype(BF16)
    o_lat = 2 * GM_WIDTH
    lat = jnp.dot(h, w_ref[:, o_lat:], preferred_element_type=F32)
    krx = jnp.dot(h, wkr_ref[...], preferred_element_type=F32)
    uv_ref[...] = _gelu_tanh(jnp.dot(h, w_ref[:, :o_lat], preferred_element_type=F32)).astype(BF16)
    gate_ref[...] = jax.nn.sigmoid(jnp.dot(h, wgate_ref[...], preferred_element_type=F32)).astype(BF16)

    cq = _rms(lat[:, :Q_LORA], qg_ref[...]).astype(BF16)
    ckv = _rms(lat[:, Q_LORA:], kvg_ref[...]).astype(BF16)
    kr = krx[:, :LANES]
    kr_sw = krx[:, LANES:]
    kr_ssq = 0.5 * jnp.sum(kr * kr, axis=-1, keepdims=True)
    kr_rot = kr * (ta_ref[...] * ga_ref[...]) + kr_sw * (tb_ref[...] * gb_ref[...])
    tq, gkn = tq_ref[...] * gq_ref[...], gkn_ref[...]
    q_all = jnp.dot(cq, wq_ref[...], preferred_element_type=F32)
    k_all = jnp.dot(ckv, wk_ref[...], preferred_element_type=F32)
    first_copy = lax.broadcasted_iota(jnp.int32, (1, HEAD_SLOT), 1) < QK_DIM
    for hd in range(MLA_HEADS):
        sl = slice(hd * HEAD_SLOT, (hd + 1) * HEAD_SLOT)
        qh = q_all[:, sl]
        ssq = jnp.sum(qh * jnp.where(first_copy, qh, 0.0), axis=-1, keepdims=True)
        q_ref[:, sl] = (qh * tq * lax.rsqrt(ssq * (1.0 / QK_DIM) + EPS)).astype(BF16)
        kh = k_all[:, sl]
        rk = lax.rsqrt((jnp.sum(kh * kh, axis=-1, keepdims=True) + kr_ssq) * (1.0 / QK_DIM) + EPS)
        k_ref[:, sl] = ((kh * gkn + kr_rot) * rk).astype(BF16)
    vt_ref[...] = lax.dot_general(wvt_ref[...], ckv, (((1,), (1,)), ((), ())),
                                  preferred_element_type=F32).astype(BF16)


def _in_qkv(x2, g, w_main, w_kr, w_gate, qg, wq, kvg, wk, wvt, tq, ta, tb, gq, ga, gb, gkn, batch, seq):
    n = x2.shape[0]
    tps = seq // TM_IN
    full = lambda shape: pl.BlockSpec(shape, lambda i: (0,) * len(shape), pipeline_mode=pl.Buffered(1))
    rope = pl.BlockSpec((TM_IN, HEAD_SLOT), lambda i: (i % tps, 0))
    row = lambda w: pl.BlockSpec((TM_IN, w), lambda i: (i, 0))
    lane_row = full((1, HEAD_SLOT))
    return pl.pallas_call(
        _in_qkv_kernel,
        grid=(n // TM_IN,),
        in_specs=[
            row(D_MODEL), full((1, D_MODEL)), full(w_main.shape), full(w_kr.shape), full(w_gate.shape),
            full((1, Q_LORA)), full((Q_LORA, QK_WIDTH)),
            full((1, KV_LORA)), full((KV_LORA, QK_WIDTH)), full((V_WIDTH, KV_LORA)),
            rope, rope, rope, lane_row, lane_row, lane_row, lane_row,
        ],
        out_specs=[row(2 * GM_WIDTH), row(2 * D_MODEL), row(QK_WIDTH), row(QK_WIDTH),
                   pl.BlockSpec((None, None, V_WIDTH, TM_IN), lambda i: (i // tps, i % tps, 0, 0))],
        out_shape=[jax.ShapeDtypeStruct((n, 2 * GM_WIDTH), BF16), jax.ShapeDtypeStruct((n, 2 * D_MODEL), BF16),
                   jax.ShapeDtypeStruct((n, QK_WIDTH), BF16), jax.ShapeDtypeStruct((n, QK_WIDTH), BF16),
                   jax.ShapeDtypeStruct((batch, tps, V_WIDTH, TM_IN), BF16)],
        compiler_params=_cparams(("parallel",)),
        name="in_qkv",
    )(x2, g, w_main, w_kr, w_gate, qg, wq, kvg, wk, wvt, tq, ta, tb, gq, ga, gb, gkn)


def _attention_kernel(q_ref, k_ref, vt_ref, o_ref, *, seq):
    nblk = seq // ATT_T
    vblocks = ATT_T // TM_IN
    key_pos = lax.broadcasted_iota(jnp.int32, (ATT_T, ATT_T), 0)
    qry_pos = lax.broadcasted_iota(jnp.int32, (ATT_T, ATT_T), 1)
    causal = key_pos <= qry_pos
    nt = (((1,), (1,)), ((), ()))
    ones_rows = jnp.ones((ATT_SUM_ROWS, ATT_T), BF16)

    def update(state, g, st, kj, diagonal):
        m, acc = state
        if diagonal:
            st = jnp.where(causal, st, -1e30)
        m_new = jnp.maximum(m, jnp.max(st, axis=0, keepdims=True))
        alpha = jnp.exp2(m - m_new)
        p = jnp.exp2((st - m_new).astype(BF16))
        vt = [vt_ref[kj * vblocks + b, g * V_DIM:(g + 1) * V_DIM, :] for b in range(vblocks)]
        v_aug = jnp.concatenate([jnp.concatenate(vt, axis=1), ones_rows], axis=0)
        return m_new, alpha * acc + jnp.dot(v_aug, p, preferred_element_type=F32)

    for qi in range(nblk):
        q0, q1 = qi * ATT_T, (qi + 1) * ATT_T
        strips = [lax.dot_general(k_ref[0:q1, g * HEAD_SLOT:(g + 1) * HEAD_SLOT],
                                  q_ref[q0:q1, g * HEAD_SLOT:(g + 1) * HEAD_SLOT], nt,
                                  preferred_element_type=F32) for g in range(ATT_G)]
        acc_rows = V_DIM + ATT_SUM_ROWS
        init = (jnp.full((1, ATT_T), -1e30, F32), jnp.zeros((acc_rows, ATT_T), F32))
        states = [init for _ in range(ATT_G)]
        for kj in range(qi + 1):
            states = [update(states[g], g, strips[g][kj * ATT_T:(kj + 1) * ATT_T], kj, kj == qi)
                      for g in range(ATT_G)]
        pad_rows = jnp.zeros((HEAD_SLOT - acc_rows, ATT_T), F32)
        for g in range(ATT_G):
            acc = jnp.concatenate([states[g][1], pad_rows], axis=0).T
            o_ref[q0:q1, g * V_DIM:(g + 1) * V_DIM] = (acc[:, :V_DIM] / acc[:, V_DIM:V_DIM + 1]).astype(BF16)


def _attention(q3, k3, vt4):
    batch, seq, _ = q3.shape
    slot = pl.BlockSpec((None, seq, ATT_G * HEAD_SLOT), lambda b, h: (b, 0, h))
    return pl.pallas_call(
        functools.partial(_attention_kernel, seq=seq),
        grid=(batch, MLA_HEADS // ATT_G),
        in_specs=[slot, slot,
                  pl.BlockSpec((None, seq // TM_IN, ATT_G * V_DIM, TM_IN), lambda b, h: (b, 0, h, 0))],
        out_specs=pl.BlockSpec((None, seq, ATT_G * V_DIM), lambda b, h: (b, 0, h)),
        out_shape=jax.ShapeDtypeStruct((batch, seq, V_WIDTH), BF16),
        compiler_params=_cparams(("parallel", "parallel")),
        name="attention",
    )(q3, k3, vt4)


def _split_bf16(v):
    hi = v.astype(BF16)
    return hi, (v - hi.astype(F32)).astype(BF16)


def _gelu_tanh(x):
    c1 = -2.0 * float(np.sqrt(2.0 / np.pi) * np.log2(np.e))
    c2 = c1 * 0.044715
    return x / (1.0 + jnp.exp2(x * (c1 + c2 * (x * x))))


def _gmlp_tile(uv_ref, vg_ref, vb_ref, ws_ref, bst_ref, a_ref):
    v = uv_ref[:, GM_WIDTH:].astype(F32)
    mu = jnp.mean(v, axis=-1, keepdims=True)
    vc = v - mu
    vn = (vc * lax.rsqrt(jnp.mean(vc * vc, axis=-1, keepdims=True) + EPS) * vg_ref[...] + vb_ref[...]).astype(BF16)
    row = lax.broadcasted_iota(jnp.int32, (GM_CHUNK, GM_CHUNK), 0)
    col = lax.broadcasted_iota(jnp.int32, (GM_CHUNK, GM_CHUNK), 1)
    tril = col <= row
    bst = bst_ref[...]
    for g in range(GM_GROUPS):
        ws = jnp.where(tril, ws_ref[g], 0.0).astype(BF16)
        bias = bst[:, g:g + 1]
        cs = slice(g * GM_GROUP_CH, (g + 1) * GM_GROUP_CH)
        for c in range(TM_MERGE // GM_CHUNK):
            rs = slice(c * GM_CHUNK, (c + 1) * GM_CHUNK)
            mixed = jnp.dot(ws, vn[rs, cs], preferred_element_type=F32) + bias
            a_ref[rs, cs] = (uv_ref[rs, cs].astype(F32) * mixed).astype(BF16)


def _merge_kernel(uv_ref, vg_ref, vb_ref, ws_ref, bst_ref, ot_ref, gate_ref, x_ref, wa_ref, wb_ref, wo_ref,
                  fg_ref, *rest, with_router):
    if with_router:
        wr_ref, br_ref, xn_ref, imeta_ref, wcol_ref, cnt_ref, a_ref, run_ref = rest
    else:
        xn_ref, h2_ref, a_ref = rest
    yb = jnp.dot(ot_ref[...], wb_ref[...], preferred_element_type=F32)
    _gmlp_tile(uv_ref, vg_ref, vb_ref, ws_ref, bst_ref, a_ref)
    ya = jnp.dot(a_ref[...], wa_ref[...], preferred_element_type=F32)
    ga = gate_ref[:, :D_MODEL].astype(F32)
    gb = gate_ref[:, D_MODEL:].astype(F32)
    merged = (ga * ya + gb * yb).astype(BF16)
    xn = x_ref[...] + jnp.dot(merged, wo_ref[...], preferred_element_type=F32)
    xn_ref[...] = xn
    h2 = _rms(xn, fg_ref[...])
    if with_router:
        h_hi, h_lo = _split_bf16(h2)
        w_hi, w_lo = _split_bf16(wr_ref[...])
        hh_hl = jnp.dot(h_hi, jnp.concatenate([w_hi, w_lo], axis=1), preferred_element_type=F32)
        logits = hh_hl[:, :LANES] + (hh_hl[:, LANES:] + jnp.dot(h_lo, w_hi, preferred_element_type=F32))
        _route_tile(logits + br_ref[...], run_ref, imeta_ref, wcol_ref, cnt_ref)
    else:
        h2_ref[...] = h2.astype(BF16)


def _merge(uv, vg, vb, ws, bst, ot, gates, x2, wa, wb, wo, fg, wr=None, br=None):
    n = x2.shape[0]
    with_router = wr is not None
    full = lambda shape: pl.BlockSpec(shape, lambda i: (0,) * len(shape))
    row = lambda w: pl.BlockSpec((TM_MERGE, w), lambda i: (i, 0))
    in_specs = [
        row(2 * GM_WIDTH), full((1, GM_WIDTH)), full((1, GM_WIDTH)),
        full((GM_GROUPS, GM_CHUNK, GM_CHUNK)), full((GM_CHUNK, GM_GROUPS)),
        row(V_WIDTH), row(2 * D_MODEL), row(D_MODEL),
        full((GM_WIDTH, D_MODEL)), full((V_WIDTH, D_MODEL)), full((D_MODEL, D_MODEL)), full((1, D_MODEL)),
    ]
    args = [uv, vg, vb, ws, bst, ot, gates, x2, wa, wb, wo, fg]
    scratch = [pltpu.VMEM((TM_MERGE, GM_WIDTH), BF16)]
    if with_router:
        in_specs += [full((D_MODEL, LANES)), full((1, LANES))]
        out_specs = [row(D_MODEL), pl.BlockSpec((N_EXPERTS, TM_MERGE), lambda i: (0, i)), row(LANES),
                     full((N_EXPERTS, LANES))]
        out_shape = [jax.ShapeDtypeStruct((n, D_MODEL), F32), jax.ShapeDtypeStruct((N_EXPERTS, n), jnp.int32),
                     jax.ShapeDtypeStruct((n, LANES), F32), jax.ShapeDtypeStruct((N_EXPERTS, LANES), jnp.int32)]
        args += [wr, br]
        scratch.append(pltpu.VMEM((N_EXPERTS, LANES), F32))
    else:
        out_specs = [row(D_MODEL), row(D_MODEL)]
        out_shape = [jax.ShapeDtypeStruct((n, D_MODEL), F32), jax.ShapeDtypeStruct((n, D_MODEL), BF16)]
    return pl.pallas_call(
        functools.partial(_merge_kernel, with_router=with_router),
        grid=(n // TM_MERGE,),
        in_specs=in_specs, out_specs=out_specs, out_shape=out_shape,
        scratch_shapes=scratch,
        compiler_params=_cparams(("arbitrary",) if with_router else ("parallel",)),
        name="merge_router" if with_router else "merge",
    )(*args)


def _round_up_tile(c):
    return (c + (TM_FFN - 1)) & (-TM_FFN)


def _route_tile(logits, run_ref, imeta_ref, wcol_ref, cnt_ref):
    tm = logits.shape[0]

    @pl.when(pl.program_id(0) == 0)
    def _():
        run_ref[...] = jnp.zeros_like(run_ref)

    lt = logits.T[:N_EXPERTS, :]
    sub = lax.broadcasted_iota(jnp.int32, lt.shape, 0)
    m1 = jnp.max(lt, axis=0, keepdims=True)
    i1 = jnp.min(jnp.where(lt == m1, sub, N_EXPERTS), axis=0, keepdims=True)
    rest = jnp.where(sub == i1, -jnp.inf, lt)
    m2 = jnp.max(rest, axis=0, keepdims=True)
    i2 = jnp.min(jnp.where(rest == m2, sub, N_EXPERTS), axis=0, keepdims=True)
    e2 = jnp.exp(m2 - m1)
    w1 = 1.0 / (1.0 + e2)
    w2 = e2 / (1.0 + e2)

    onehot = jnp.where((sub == i1) | (sub == i2), 1.0, 0.0)
    src = lax.broadcasted_iota(jnp.int32, (tm, tm), 0)
    dst = lax.broadcasted_iota(jnp.int32, (tm, tm), 1)
    earlier = jnp.where(src < dst, 1.0, 0.0).astype(BF16)
    seen = jnp.dot(onehot.astype(BF16), earlier, preferred_element_type=F32) + run_ref[:, :1]
    r1 = jnp.sum(jnp.where(sub == i1, seen, 0.0), axis=0, keepdims=True).astype(jnp.int32)
    r2 = jnp.sum(jnp.where(sub == i2, seen, 0.0), axis=0, keepdims=True).astype(jnp.int32)
    imeta_ref[...] = jnp.where(sub == 0, i1, jnp.where(sub == 1, i2, jnp.where(sub == 2, r1, jnp.where(sub == 3, r2, 0))))

    total = run_ref[...] + jnp.sum(onehot, axis=1, keepdims=True)
    run_ref[...] = total
    cnt_ref[...] = total.astype(jnp.int32)

    sub_w = lax.broadcasted_iota(jnp.int32, (LANES, tm), 0)
    wcol_ref[...] = jnp.where(sub_w == 0, w1, jnp.where(sub_w == 1, w2, 0.0)).T


def _positions_kernel(cnt_ref, imeta_ref, pos_ref):
    im = imeta_ref[...]
    i1, i2, r1, r2 = im[0:1], im[1:2], im[2:3], im[3:4]
    start = jnp.int32(0)
    p1 = jnp.zeros_like(i1)
    p2 = jnp.zeros_like(i2)
    for e in range(N_EXPERTS):
        p1 = jnp.where(i1 == e, start, p1)
        p2 = jnp.where(i2 == e, start, p2)
        start = start + _round_up_tile(cnt_ref[e])
    pos_ref[...] = jnp.concatenate([p1 + r1, p2 + r2], axis=0)


def _positions(cnt, imeta):
    n = imeta.shape[1]
    return pl.pallas_call(
        _positions_kernel,
        grid_spec=pltpu.PrefetchScalarGridSpec(
            num_scalar_prefetch=1,
            grid=(n // TM_POS,),
            in_specs=[pl.BlockSpec((N_EXPERTS, TM_POS), lambda i, c: (0, i))],
            out_specs=pl.BlockSpec((2, TM_POS), lambda i, c: (0, i)),
        ),
        out_shape=jax.ShapeDtypeStruct((2, n), jnp.int32),
        compiler_params=_cparams(("parallel",)),
        name="positions",
    )(cnt, imeta)


def _scatter_kernel(pos_ref, cnt_ref, xn_ref, fg_ref, xs_ref, hbuf, zbuf, sems, zsem, *, n_tokens):
    n_tiles = xs_ref.shape[0] // TM_FFN

    @pl.when(pl.program_id(0) == 0)
    def _():
        zbuf[...] = jnp.zeros_like(zbuf)

        def zero_tile(t):
            rows = pl.ds(pl.multiple_of(t * TM_FFN, TM_FFN), TM_FFN)
            return pltpu.make_async_copy(zbuf, xs_ref.at[rows], zsem)

        def each_zero_tile(action):
            end = jnp.int32(0)
            for e in range(N_EXPERTS):
                tiles = _round_up_tile(cnt_ref[e]) // TM_FFN
                end = end + tiles

                @pl.when(tiles > 0)
                def _(end=end):
                    action(zero_tile(end - 1))

            def tail(t, carry):
                action(zero_tile(t))
                return carry

            lax.fori_loop(end, n_tiles, tail, 0)

        each_zero_tile(lambda copy: copy.start())
        each_zero_tile(lambda copy: copy.wait())

    i = pl.program_id(0)
    last = pl.num_programs(0) - 1
    slot = i % 2

    def drain(s):
        for k in range(2):
            pltpu.make_async_copy(hbuf.at[s], xs_ref.at[pl.ds(0, TM_SCAT)], sems.at[s]).wait()

    @pl.when(i >= 2)
    def _():
        drain(slot)

    base = i * TM_SCAT
    hbuf[slot] = _rms(xn_ref[...], fg_ref[...])

    def issue(r, carry):
        for k in range(2):
            dst = pos_ref[k * n_tokens + base + r]
            pltpu.make_async_copy(hbuf.at[slot, pl.ds(r, 1)], xs_ref.at[pl.ds(dst, 1)], sems.at[slot]).start()
        return carry

    lax.fori_loop(0, TM_SCAT, issue, 0, unroll=8)

    @pl.when(i == last)
    def _():
        drain(1 - slot)
        drain(slot)


def _scatter(pos, cnt, xn, fg):
    n = xn.shape[0]
    assert n // TM_SCAT >= 2
    rows = 2 * n + N_EXPERTS * TM_FFN
    return pl.pallas_call(
        functools.partial(_scatter_kernel, n_tokens=n),
        grid_spec=pltpu.PrefetchScalarGridSpec(
            num_scalar_prefetch=2,
            grid=(n // TM_SCAT,),
            in_specs=[pl.BlockSpec((TM_SCAT, D_MODEL), lambda i, p, c: (i, 0)),
                      pl.BlockSpec((1, D_MODEL), lambda i, p, c: (0, 0))],
            out_specs=pl.BlockSpec(memory_space=pl.ANY),
            scratch_shapes=[pltpu.VMEM((2, TM_SCAT, D_MODEL), F32), pltpu.VMEM((TM_FFN, D_MODEL), F32),
                            pltpu.SemaphoreType.DMA((2,)), pltpu.SemaphoreType.DMA(())],
        ),
        out_shape=jax.ShapeDtypeStruct((rows, D_MODEL), F32),
        compiler_params=_cparams(("arbitrary",)),
        name="moe_scatter",
    )(pos, cnt, xn, fg)


def _combine_kernel(pos_ref, xn_ref, wcol_ref, ys_ref, o_ref, buf, sems, *, n_tokens):
    i = pl.program_id(0)
    slot = i % 2

    def fetch_tile(tile, s):
        base = tile * TM_COMB

        def issue(r, carry):
            for k in range(2):
                src = pos_ref[k * n_tokens + base + r]
                pltpu.make_async_copy(ys_ref.at[pl.ds(src, 1)], buf.at[s, k, pl.ds(r, 1)], sems.at[s]).start()
            return carry

        lax.fori_loop(0, TM_COMB, issue, 0, unroll=8)

    @pl.when(i == 0)
    def _():
        fetch_tile(0, 0)

    @pl.when(i + 1 < pl.num_programs(0))
    def _():
        fetch_tile(i + 1, 1 - slot)

    for k in range(2):
        pltpu.make_async_copy(ys_ref.at[pl.ds(0, TM_COMB)], buf.at[slot, k], sems.at[slot]).wait()
    w = wcol_ref[...]
    o_ref[...] = xn_ref[...] + (w[:, 0:1] * buf[slot, 0] + w[:, 1:2] * buf[slot, 1])


def _combine(pos, xn, wcol, ys):
    n = xn.shape[0]
    return pl.pallas_call(
        functools.partial(_combine_kernel, n_tokens=n),
        grid_spec=pltpu.PrefetchScalarGridSpec(
            num_scalar_prefetch=1,
            grid=(n // TM_COMB,),
            in_specs=[pl.BlockSpec((TM_COMB, D_MODEL), lambda i, p: (i, 0)),
                      pl.BlockSpec((TM_COMB, LANES), lambda i, p: (i, 0)),
                      pl.BlockSpec(memory_space=pl.ANY)],
            out_specs=pl.BlockSpec((TM_COMB, D_MODEL), lambda i, p: (i, 0)),
            scratch_shapes=[pltpu.VMEM((2, 2, TM_COMB, D_MODEL), F32), pltpu.SemaphoreType.DMA((2,))],
        ),
        out_shape=jax.ShapeDtypeStruct((n, D_MODEL), F32),
        compiler_params=_cparams(("arbitrary",)),
        name="moe_combine",
    )(pos, xn, wcol, ys)


def _swiglu(h, wg_ref, wu_ref, wd_ref):
    y = None
    c0 = 0
    for width in FF_CHUNKS:
        cs = slice(c0, c0 + width)
        g = jnp.dot(h, wg_ref[:, cs], preferred_element_type=F32)
        u = jnp.dot(h, wu_ref[:, cs], preferred_element_type=F32)
        act = (g * jax.nn.sigmoid(g) * u).astype(BF16)
        part = jnp.dot(act, wd_ref[cs, :], preferred_element_type=F32)
        y = part if y is None else y + part
        c0 += width
    return y


def _ffn_dense_kernel(h_ref, x_ref, wg_ref, wu_ref, wd_ref, o_ref):
    o_ref[...] = x_ref[...] + _swiglu(h_ref[...], wg_ref, wu_ref, wd_ref)


def _ffn_dense(h2, x2, wgu, wd):
    n = h2.shape[0]
    return pl.pallas_call(
        _ffn_dense_kernel,
        grid=(n // TM_FFN,),
        in_specs=[
            pl.BlockSpec((TM_FFN, D_MODEL), lambda i: (i, 0)),
            pl.BlockSpec((TM_FFN, D_MODEL), lambda i: (i, 0)),
            pl.BlockSpec((D_MODEL, FF), lambda i: (0, 0)),
            pl.BlockSpec((D_MODEL, FF), lambda i: (0, 1)),
            pl.BlockSpec((FF, D_MODEL), lambda i: (0, 0)),
        ],
        out_specs=pl.BlockSpec((TM_FFN, D_MODEL), lambda i: (i, 0)),
        out_shape=jax.ShapeDtypeStruct((n, D_MODEL), F32),
        compiler_params=_cparams(("parallel",)),
        name="ffn_dense",
    )(h2, x2, wgu, wgu, wd)


def _tile_plan(i, cnt_ref):
    end = jnp.int32(0)
    expert = jnp.int32(0)
    for e in range(N_EXPERTS):
        end = end + _round_up_tile(cnt_ref[e])
        expert = expert + (end <= i * TM_FFN).astype(jnp.int32)
    return jnp.minimum(expert, N_EXPERTS - 1), end // TM_FFN


def _ffn_grouped_kernel(cnt_ref, xs_ref, wg_ref, wu_ref, wd_ref, o_ref):
    i = pl.program_id(0)
    _, used = _tile_plan(i, cnt_ref)

    @pl.when(i < used)
    def _():
        o_ref[...] = _swiglu(xs_ref[...].astype(BF16), wg_ref, wu_ref, wd_ref)

    @pl.when(i >= used)
    def _():
        o_ref[...] = jnp.zeros_like(o_ref)


def _ffn_grouped(cnt, xs, wgu, wd):
    rows = xs.shape[0]

    def tile(i, c):
        return jnp.minimum(i, _tile_plan(i, c)[1] - 1)

    return pl.pallas_call(
        _ffn_grouped_kernel,
        grid_spec=pltpu.PrefetchScalarGridSpec(
            num_scalar_prefetch=1,
            grid=(rows // TM_FFN,),
            in_specs=[
                pl.BlockSpec((TM_FFN, D_MODEL), lambda i, c: (tile(i, c), 0)),
                pl.BlockSpec((None, D_MODEL, FF), lambda i, c: (_tile_plan(i, c)[0], 0, 0)),
                pl.BlockSpec((None, D_MODEL, FF), lambda i, c: (_tile_plan(i, c)[0], 0, 1)),
                pl.BlockSpec((None, FF, D_MODEL), lambda i, c: (_tile_plan(i, c)[0], 0, 0)),
            ],
            out_specs=pl.BlockSpec((TM_FFN, D_MODEL), lambda i, c: (i, 0)),
        ),
        out_shape=jax.ShapeDtypeStruct((rows, D_MODEL), F32),
        compiler_params=_cparams(("arbitrary",)),
        name="ffn_grouped",
    )(cnt, xs, wgu, wgu, wd)


def _moe(xn, imeta, wcol, cnt_lanes, fg, wgu, wd):
    n = xn.shape[0]
    cnt = cnt_lanes[:, 0]
    pos = _positions(cnt, imeta).reshape(2 * n)
    xs = _scatter(pos, cnt, xn, fg)
    ys = _ffn_grouped(cnt, xs, wgu, wd)
    return _combine(pos, xn, wcol, ys)


def _rope_tables(seq):
    pos = jnp.arange(seq, dtype=F32)
    inv_freq = ROPE_THETA ** (-jnp.arange(0, QK_ROPE, 2, dtype=F32) / QK_ROPE)
    ang = pos[:, None] * inv_freq[None, :]
    return jnp.cos(ang), jnp.sin(ang)


def _swap_halves(w):
    half = QK_ROPE // 2
    return jnp.concatenate([w[..., half:], w[..., :half]], axis=-1)


def _head_slots(w, width):
    k = w.shape[0]
    w3 = w.reshape(k, MLA_HEADS, width)
    return jnp.pad(w3, ((0, 0), (0, 0), (0, HEAD_SLOT - width))).reshape(k, QK_WIDTH)


def _pad_lanes(v, width=LANES):
    return jnp.pad(v, (0, width - v.shape[0])).reshape(1, width)


def kernel(x, mix_norm_g, w_in, gm_v_norm_g, gm_v_norm_b, gm_w_spatial, gm_b_spatial, gm_w_proj, mla_q_lat_g, mla_w_uq, mla_kv_lat_g, mla_w_ukv, mla_q_norm_g, mla_k_norm_g, mla_w_proj, w_out, ffn_norm_g, dense_w_gu, dense_w_down, moe_w_router, moe_b_router, moe_w_gu, moe_w_down):
    batch, seq, d = x.shape
    n = batch * seq
    depth = w_in.shape[0]
    cos, sin = _rope_tables(seq)
    x2 = x.reshape(n, d)
    q_scale = float(np.log2(np.e) / np.sqrt(QK_DIM))

    o_kr = 2 * GM_WIDTH + Q_LORA + KV_LORA
    o_gate = o_kr + QK_ROPE

    ones_nope, zeros_nope = jnp.ones((seq, QK_NOPE), F32), jnp.zeros((seq, QK_NOPE), F32)
    cc, ss = jnp.concatenate([cos, cos], axis=1), jnp.concatenate([-sin, sin], axis=1)
    tq = jnp.concatenate([ones_nope, cc, ss], axis=1)
    ta = jnp.concatenate([zeros_nope, cc, cc], axis=1)
    tb = jnp.concatenate([zeros_nope, ss, ss], axis=1)

    for l in range(depth):
        wl = w_in[l]
        w_kr = wl[:, o_kr:o_gate]
        w_krx = jnp.pad(jnp.concatenate([w_kr, w_kr], axis=1), ((0, 0), (QK_NOPE, 0)))
        w_krs = jnp.pad(jnp.concatenate([_swap_halves(w_kr)] * 2, axis=1), ((0, 0), (QK_NOPE, 0)))
        w_kr2 = jnp.concatenate([w_krx, w_krs], axis=1).astype(BF16)
        wq3 = mla_w_uq[l].reshape(Q_LORA, MLA_HEADS, QK_DIM)
        wq = jnp.concatenate([wq3, _swap_halves(wq3[:, :, QK_NOPE:])], axis=2).reshape(Q_LORA, QK_WIDTH).astype(BF16)
        wkv3 = mla_w_ukv[l].reshape(KV_LORA, MLA_HEADS, QK_NOPE + V_DIM)
        wk = _head_slots(wkv3[:, :, :QK_NOPE].reshape(KV_LORA, MLA_HEADS * QK_NOPE), QK_NOPE).astype(BF16)
        wvt = wkv3[:, :, QK_NOPE:].reshape(KV_LORA, V_WIDTH).T.astype(BF16)
        gq, gk = mla_q_norm_g[l], mla_k_norm_g[l]
        g12 = lambda g: g[QK_NOPE:]
        g21 = lambda g: _swap_halves(g[QK_NOPE:])
        gq_row = (jnp.concatenate([gq[:QK_NOPE], g12(gq), g21(gq)]) * q_scale).reshape(1, HEAD_SLOT)
        ga_row = _pad_lanes(jnp.concatenate([jnp.zeros((QK_NOPE,), F32), g12(gk), g12(gk)]))
        gb_row = _pad_lanes(jnp.concatenate([jnp.zeros((QK_NOPE,), F32), g21(gk), g21(gk)]))
        uv, gates, q, k, vt = _in_qkv(x2, mix_norm_g[l].reshape(1, d),
                                      wl[:, :o_kr].astype(BF16), w_kr2, wl[:, o_gate:].astype(BF16),
                                      mla_q_lat_g[l].reshape(1, Q_LORA), wq,
                                      mla_kv_lat_g[l].reshape(1, KV_LORA), wk, wvt,
                                      tq, ta, tb, gq_row, ga_row, gb_row, _pad_lanes(gk[:QK_NOPE]), batch, seq)
        slots = lambda t: t.reshape(batch, seq, QK_WIDTH)
        ot = _attention(slots(q), slots(k), vt).reshape(n, V_WIDTH)

        is_moe = l % 2 == 1
        m = l // 2
        router = {}
        if is_moe:
            router = dict(wr=jnp.pad(moe_w_router[m], ((0, 0), (0, LANES - N_EXPERTS))),
                          br=_pad_lanes(moe_b_router[m]))
        outs = _merge(uv, gm_v_norm_g[l].reshape(1, GM_WIDTH), gm_v_norm_b[l].reshape(1, GM_WIDTH),
                      gm_w_spatial[l], gm_b_spatial[l].T, ot, gates, x2,
                      gm_w_proj[l].astype(BF16), mla_w_proj[l].astype(BF16),
                      w_out[l].astype(BF16), ffn_norm_g[l].reshape(1, d), **router)
        if is_moe:
            xn, imeta, wcol, cnt_lanes = outs
            x2 = _moe(xn, imeta, wcol, cnt_lanes, ffn_norm_g[l].reshape(1, d),
                      moe_w_gu[m].astype(BF16), moe_w_down[m].astype(BF16))
        else:
            xn, h2 = outs
            x2 = _ffn_dense(h2, xn, dense_w_gu[m].astype(BF16), dense_w_down[m].astype(BF16))
    return x2.reshape(batch, seq, d)
```

```python
import functools

import jax
import jax.numpy as jnp
import numpy as np
from jax import lax
from jax.experimental import pallas as pl
from jax.experimental.pallas import tpu as pltpu

F32 = jnp.float32
BF16 = jnp.bfloat16

EPS = 1e-6
LANES = 128

D_MODEL = 1024
GM_GROUPS = 8
GM_GROUP_CH = 128
GM_WIDTH = 1024
GM_CHUNK = 128
MLA_HEADS = 16
QK_NOPE = 64
QK_ROPE = 32
QK_DIM = 96
V_DIM = 64
Q_LORA = 512
KV_LORA = 256
ROPE_THETA = 10000.0
HEAD_SLOT = LANES
QK_WIDTH = MLA_HEADS * HEAD_SLOT
V_WIDTH = MLA_HEADS * V_DIM
N_EXPERTS = 8
FF = 2816

TM_IN = 256
TM_MERGE = 512
TM_FFN = 512
FF_CHUNKS = (768, 768, 768, 512)
TM_POS = 2048
TM_SCAT = 512
TM_COMB = 512
ATT_T = 512
ATT_G = 2
ATT_SUM_ROWS = 16

VMEM_LIMIT = 56 * 1024 * 1024


def _cparams(sem):
    return pltpu.CompilerParams(dimension_semantics=sem, vmem_limit_bytes=VMEM_LIMIT)


def _rms(xf, g):
    return xf * lax.rsqrt(jnp.mean(xf * xf, axis=-1, keepdims=True) + EPS) * g


def _in_qkv_kernel(x_ref, g_ref, w_ref, wkr_ref, wgate_ref, qg_ref, wq_ref, kvg_ref, wk_ref, wvt_ref,
                   tq_ref, ta_ref, tb_ref, gq_ref, ga_ref, gb_ref, gkn_ref,
                   uv_ref, gate_ref, q_ref, k_ref, vt_ref):
    h = _rms(x_ref[...], g_ref[...]).astype(BF16)
    o_lat = 2 * GM_WIDTH
    lat = jnp.dot(h, w_ref[:, o_lat:], preferred_element_type=F32)
    krx = jnp.dot(h, wkr_ref[...], preferred_element_type=F32)
    uv_ref[...] = _gelu_tanh(jnp.dot(h, w_ref[:, :o_lat], preferred_element_type=F32)).astype(BF16)
    gate_ref[...] = jax.nn.sigmoid(jnp.dot(h, wgate_ref[...], preferred_element_type=F32)).astype(BF16)

    cq = _rms(lat[:, :Q_LORA], qg_ref[...]).astype(BF16)
    ckv = _rms(lat[:, Q_LORA:], kvg_ref[...]).astype(BF16)
    kr = krx[:, :LANES]
    kr_sw = krx[:, LANES:]
    kr_ssq = 0.5 * jnp.sum(kr * kr, axis=-1, keepdims=True)
    kr_rot = kr * (ta_ref[...] * ga_ref[...]) + kr_sw * (tb_ref[...] * gb_ref[...])
    tq, gkn = tq_ref[...] * gq_ref[...], gkn_ref[...]
    q_all = jnp.dot(cq, wq_ref[...], preferred_element_type=F32)
    k_all = jnp.dot(ckv, wk_ref[...], preferred_element_type=F32)
    first_copy = lax.broadcasted_iota(jnp.int32, (1, HEAD_SLOT), 1) < QK_DIM
    for hd in range(MLA_HEADS):
        sl = slice(hd * HEAD_SLOT, (hd + 1) * HEAD_SLOT)
        qh = q_all[:, sl]
        ssq = jnp.sum(qh * jnp.where(first_copy, qh, 0.0), axis=-1, keepdims=True)
        q_ref[:, sl] = (qh * tq * lax.rsqrt(ssq * (1.0 / QK_DIM) + EPS)).astype(BF16)
        kh = k_all[:, sl]
        rk = lax.rsqrt((jnp.sum(kh * kh, axis=-1, keepdims=True) + kr_ssq) * (1.0 / QK_DIM) + EPS)
        k_ref[:, sl] = ((kh * gkn + kr_rot) * rk).astype(BF16)
    vt_ref[...] = lax.dot_general(wvt_ref[...], ckv, (((1,), (1,)), ((), ())),
                                  preferred_element_type=F32).astype(BF16)


def _in_qkv(x2, g, w_main, w_kr, w_gate, qg, wq, kvg, wk, wvt, tq, ta, tb, gq, ga, gb, gkn, batch, seq):
    n = x2.shape[0]
    tps = seq // TM_IN
    full = lambda shape: pl.BlockSpec(shape, lambda i: (0,) * len(shape), pipeline_mode=pl.Buffered(1))
    rope = pl.BlockSpec((TM_IN, HEAD_SLOT), lambda i: (i % tps, 0))
    row = lambda w: pl.BlockSpec((TM_IN, w), lambda i: (i, 0))
    lane_row = full((1, HEAD_SLOT))
    return pl.pallas_call(
        _in_qkv_kernel,
        grid=(n // TM_IN,),
        in_specs=[
            row(D_MODEL), full((1, D_MODEL)), full(w_main.shape), full(w_kr.shape), full(w_gate.shape),
            full((1, Q_LORA)), full((Q_LORA, QK_WIDTH)),
            full((1, KV_LORA)), full((KV_LORA, QK_WIDTH)), full((V_WIDTH, KV_LORA)),
            rope, rope, rope, lane_row, lane_row, lane_row, lane_row,
        ],
        out_specs=[row(2 * GM_WIDTH), row(2 * D_MODEL), row(QK_WIDTH), row(QK_WIDTH),
                   pl.BlockSpec((None, None, V_WIDTH, TM_IN), lambda i: (i // tps, i % tps, 0, 0))],
        out_shape=[jax.ShapeDtypeStruct((n, 2 * GM_WIDTH), BF16), jax.ShapeDtypeStruct((n, 2 * D_MODEL), BF16),
                   jax.ShapeDtypeStruct((n, QK_WIDTH), BF16), jax.ShapeDtypeStruct((n, QK_WIDTH), BF16),
                   jax.ShapeDtypeStruct((batch, tps, V_WIDTH, TM_IN), BF16)],
        compiler_params=_cparams(("parallel",)),
        name="in_qkv",
    )(x2, g, w_main, w_kr, w_gate, qg, wq, kvg, wk, wvt, tq, ta, tb, gq, ga, gb, gkn)


def _attention_kernel(q_ref, k_ref, vt_ref, o_ref, *, seq):
    nblk = seq // ATT_T
    vblocks = ATT_T // TM_IN
    key_pos = lax.broadcasted_iota(jnp.int32, (ATT_T, ATT_T), 0)
    qry_pos = lax.broadcasted_iota(jnp.int32, (ATT_T, ATT_T), 1)
    causal = key_pos <= qry_pos
    nt = (((1,), (1,)), ((), ()))
    ones_rows = jnp.ones((ATT_SUM_ROWS, ATT_T), BF16)

    def update(state, g, st, kj, diagonal):
        m, acc = state
        if diagonal:
            st = jnp.where(causal, st, -1e30)
        m_new = jnp.maximum(m, jnp.max(st, axis=0, keepdims=True))
        alpha = jnp.exp2(m - m_new)
        p = jnp.exp2((st - m_new).astype(BF16))
        vt = [vt_ref[kj * vblocks + b, g * V_DIM:(g + 1) * V_DIM, :] for b in range(vblocks)]
        v_aug = jnp.concatenate([jnp.concatenate(vt, axis=1), ones_rows], axis=0)
        return m_new, alpha * acc + jnp.dot(v_aug, p, preferred_element_type=F32)

    for qi in range(nblk):
        q0, q1 = qi * ATT_T, (qi + 1) * ATT_T
        strips = [lax.dot_general(k_ref[0:q1, g * HEAD_SLOT:(g + 1) * HEAD_SLOT],
                                  q_ref[q0:q1, g * HEAD_SLOT:(g + 1) * HEAD_SLOT], nt,
                                  preferred_element_type=F32) for g in range(ATT_G)]
        acc_rows = V_DIM + ATT_SUM_ROWS
        init = (jnp.full((1, ATT_T), -1e30, F32), jnp.zeros((acc_rows, ATT_T), F32))
        states = [init for _ in range(ATT_G)]
        for kj in range(qi + 1):
            states = [update(states[g], g, strips[g][kj * ATT_T:(kj + 1) * ATT_T], kj, kj == qi)
                      for g in range(ATT_G)]
        pad_rows = jnp.zeros((HEAD_SLOT - acc_rows, ATT_T), F32)
        for g in range(ATT_G):
            acc = jnp.concatenate([states[g][1], pad_rows], axis=0).T
            o_ref[q0:q1, g * V_DIM:(g + 1) * V_DIM] = (acc[:, :V_DIM] / acc[:, V_DIM:V_DIM + 1]).astype(BF16)


def _attention(q3, k3, vt4):
    batch, seq, _ = q3.shape
    slot = pl.BlockSpec((None, seq, ATT_G * HEAD_SLOT), lambda b, h: (b, 0, h))
    return pl.pallas_call(
        functools.partial(_attention_kernel, seq=seq),
        grid=(batch, MLA_HEADS // ATT_G),
        in_specs=[slot, slot,
                  pl.BlockSpec((None, seq // TM_IN, ATT_G * V_DIM, TM_IN), lambda b, h: (b, 0, h, 0))],
        out_specs=pl.BlockSpec((None, seq, ATT_G * V_DIM), lambda b, h: (b, 0, h)),
        out_shape=jax.ShapeDtypeStruct((batch, seq, V_WIDTH), BF16),
        compiler_params=_cparams(("parallel", "parallel")),
        name="attention",
    )(q3, k3, vt4)


def _split_bf16(v):
    hi = v.astype(BF16)
    return hi, (v - hi.astype(F32)).astype(BF16)


def _gelu_tanh(x):
    c1 = -2.0 * float(np.sqrt(2.0 / np.pi) * np.log2(np.e))
    c2 = c1 * 0.044715
    return x / (1.0 + jnp.exp2(x * (c1 + c2 * (x * x))))


def _gmlp_tile(uv_ref, vg_ref, vb_ref, ws_ref, bst_ref, a_ref):
    v = uv_ref[:, GM_WIDTH:].astype(F32)
    mu = jnp.mean(v, axis=-1, keepdims=True)
    vc = v - mu
    vn = (vc * lax.rsqrt(jnp.mean(vc * vc, axis=-1, keepdims=True) + EPS) * vg_ref[...] + vb_ref[...]).astype(BF16)
    row = lax.broadcasted_iota(jnp.int32, (GM_CHUNK, GM_CHUNK), 0)
    col = lax.broadcasted_iota(jnp.int32, (GM_CHUNK, GM_CHUNK), 1)
    tril = col <= row
    bst = bst_ref[...]
    for g in range(GM_GROUPS):
        ws = jnp.where(tril, ws_ref[g], 0.0).astype(BF16)
        bias = bst[:, g:g + 1]
        cs = slice(g * GM_GROUP_CH, (g + 1) * GM_GROUP_CH)
        for c in range(TM_MERGE // GM_CHUNK):
            rs = slice(c * GM_CHUNK, (c + 1) * GM_CHUNK)
            mixed = jnp.dot(ws, vn[rs, cs], preferred_element_type=F32) + bias
            a_ref[rs, cs] = (uv_ref[rs, cs].astype(F32) * mixed).astype(BF16)


def _merge_kernel(uv_ref, vg_ref, vb_ref, ws_ref, bst_ref, ot_ref, gate_ref, x_ref, wa_ref, wb_ref, wo_ref,
                  fg_ref, *rest, with_router):
    if with_router:
        wr_ref, br_ref, xn_ref, imeta_ref, wcol_ref, cnt_ref, hn_ref, a_ref, run_ref = rest
    else:
        xn_ref, h2_ref, a_ref = rest
    yb = jnp.dot(ot_ref[...], wb_ref[...], preferred_element_type=F32)
    _gmlp_tile(uv_ref, vg_ref, vb_ref, ws_ref, bst_ref, a_ref)
    ya = jnp.dot(a_ref[...], wa_ref[...], preferred_element_type=F32)
    ga = gate_ref[:, :D_MODEL].astype(F32)
    gb = gate_ref[:, D_MODEL:].astype(F32)
    merged = (ga * ya + gb * yb).astype(BF16)
    xn = x_ref[...] + jnp.dot(merged, wo_ref[...], preferred_element_type=F32)
    xn_ref[...] = xn
    h2 = _rms(xn, fg_ref[...])
    if with_router:
        hn_ref[...] = h2
        h_hi, h_lo = _split_bf16(h2)
        w_hi, w_lo = _split_bf16(wr_ref[...])
        hh_hl = jnp.dot(h_hi, jnp.concatenate([w_hi, w_lo], axis=1), preferred_element_type=F32)
        logits = hh_hl[:, :LANES] + (hh_hl[:, LANES:] + jnp.dot(h_lo, w_hi, preferred_element_type=F32))
        _route_tile(logits + br_ref[...], run_ref, imeta_ref, wcol_ref, cnt_ref)
    else:
        h2_ref[...] = h2.astype(BF16)


def _merge(uv, vg, vb, ws, bst, ot, gates, x2, wa, wb, wo, fg, wr=None, br=None):
    n = x2.shape[0]
    with_router = wr is not None
    full = lambda shape: pl.BlockSpec(shape, lambda i: (0,) * len(shape))
    row = lambda w: pl.BlockSpec((TM_MERGE, w), lambda i: (i, 0))
    in_specs = [
        row(2 * GM_WIDTH), full((1, GM_WIDTH)), full((1, GM_WIDTH)),
        full((GM_GROUPS, GM_CHUNK, GM_CHUNK)), full((GM_CHUNK, GM_GROUPS)),
        row(V_WIDTH), row(2 * D_MODEL), row(D_MODEL),
        full((GM_WIDTH, D_MODEL)), full((V_WIDTH, D_MODEL)), full((D_MODEL, D_MODEL)), full((1, D_MODEL)),
    ]
    args = [uv, vg, vb, ws, bst, ot, gates, x2, wa, wb, wo, fg]
    scratch = [pltpu.VMEM((TM_MERGE, GM_WIDTH), BF16)]
    if with_router:
        in_specs += [full((D_MODEL, LANES)), full((1, LANES))]
        out_specs = [row(D_MODEL), pl.BlockSpec((N_EXPERTS, TM_MERGE), lambda i: (0, i)), row(LANES),
                     full((N_EXPERTS, LANES)), row(D_MODEL)]
        out_shape = [jax.ShapeDtypeStruct((n, D_MODEL), F32), jax.ShapeDtypeStruct((N_EXPERTS, n), jnp.int32),
                     jax.ShapeDtypeStruct((n, LANES), F32), jax.ShapeDtypeStruct((N_EXPERTS, LANES), jnp.int32),
                     jax.ShapeDtypeStruct((n, D_MODEL), F32)]
        args += [wr, br]
        scratch.append(pltpu.VMEM((N_EXPERTS, LANES), F32))
    else:
        out_specs = [row(D_MODEL), row(D_MODEL)]
        out_shape = [jax.ShapeDtypeStruct((n, D_MODEL), F32), jax.ShapeDtypeStruct((n, D_MODEL), BF16)]
    return pl.pallas_call(
        functools.partial(_merge_kernel, with_router=with_router),
        grid=(n // TM_MERGE,),
        in_specs=in_specs, out_specs=out_specs, out_shape=out_shape,
        scratch_shapes=scratch,
        compiler_params=_cparams(("arbitrary",) if with_router else ("parallel",)),
        name="merge_router" if with_router else "merge",
    )(*args)


def _round_up_tile(c):
    return (c + (TM_FFN - 1)) & (-TM_FFN)


def _route_tile(logits, run_ref, imeta_ref, wcol_ref, cnt_ref):
    tm = logits.shape[0]

    @pl.when(pl.program_id(0) == 0)
    def _():
        run_ref[...] = jnp.zeros_like(run_ref)

    lt = logits.T[:N_EXPERTS, :]
    sub = lax.broadcasted_iota(jnp.int32, lt.shape, 0)
    m1 = jnp.max(lt, axis=0, keepdims=True)
    i1 = jnp.min(jnp.where(lt == m1, sub, N_EXPERTS), axis=0, keepdims=True)
    rest = jnp.where(sub == i1, -jnp.inf, lt)
    m2 = jnp.max(rest, axis=0, keepdims=True)
    i2 = jnp.min(jnp.where(rest == m2, sub, N_EXPERTS), axis=0, keepdims=True)
    e2 = jnp.exp(m2 - m1)
    w1 = 1.0 / (1.0 + e2)
    w2 = e2 / (1.0 + e2)

    onehot = jnp.where((sub == i1) | (sub == i2), 1.0, 0.0)
    src = lax.broadcasted_iota(jnp.int32, (tm, tm), 0)
    dst = lax.broadcasted_iota(jnp.int32, (tm, tm), 1)
    earlier = jnp.where(src < dst, 1.0, 0.0).astype(BF16)
    seen = jnp.dot(onehot.astype(BF16), earlier, preferred_element_type=F32) + run_ref[:, :1]
    r1 = jnp.sum(jnp.where(sub == i1, seen, 0.0), axis=0, keepdims=True).astype(jnp.int32)
    r2 = jnp.sum(jnp.where(sub == i2, seen, 0.0), axis=0, keepdims=True).astype(jnp.int32)
    imeta_ref[...] = jnp.where(sub == 0, i1, jnp.where(sub == 1, i2, jnp.where(sub == 2, r1, jnp.where(sub == 3, r2, 0))))

    total = run_ref[...] + jnp.sum(onehot, axis=1, keepdims=True)
    run_ref[...] = total
    cnt_ref[...] = total.astype(jnp.int32)

    sub_w = lax.broadcasted_iota(jnp.int32, (LANES, tm), 0)
    wcol_ref[...] = jnp.where(sub_w == 0, w1, jnp.where(sub_w == 1, w2, 0.0)).T


def _positions_kernel(cnt_ref, imeta_ref, pos_ref):
    im = imeta_ref[...]
    i1, i2, r1, r2 = im[0:1], im[1:2], im[2:3], im[3:4]
    start = jnp.int32(0)
    p1 = jnp.zeros_like(i1)
    p2 = jnp.zeros_like(i2)
    for e in range(N_EXPERTS):
        p1 = jnp.where(i1 == e, start, p1)
        p2 = jnp.where(i2 == e, start, p2)
        start = start + _round_up_tile(cnt_ref[e])
    pos_ref[...] = jnp.concatenate([p1 + r1, p2 + r2], axis=0)


def _positions(cnt, imeta):
    n = imeta.shape[1]
    return pl.pallas_call(
        _positions_kernel,
        grid_spec=pltpu.PrefetchScalarGridSpec(
            num_scalar_prefetch=1,
            grid=(n // TM_POS,),
            in_specs=[pl.BlockSpec((N_EXPERTS, TM_POS), lambda i, c: (0, i))],
            out_specs=pl.BlockSpec((2, TM_POS), lambda i, c: (0, i)),
        ),
        out_shape=jax.ShapeDtypeStruct((2, n), jnp.int32),
        compiler_params=_cparams(("parallel",)),
        name="positions",
    )(cnt, imeta)


def _scatter_kernel(pos_ref, cnt_ref, hn_ref, xs_ref, zbuf, sems, zsem, *, n_tokens):
    n_tiles = xs_ref.shape[0] // TM_FFN

    @pl.when(pl.program_id(0) == 0)
    def _():
        zbuf[...] = jnp.zeros_like(zbuf)

        def zero_tile(t):
            rows = pl.ds(pl.multiple_of(t * TM_FFN, TM_FFN), TM_FFN)
            return pltpu.make_async_copy(zbuf, xs_ref.at[rows], zsem)

        def each_zero_tile(action):
            end = jnp.int32(0)
            for e in range(N_EXPERTS):
                tiles = _round_up_tile(cnt_ref[e]) // TM_FFN
                end = end + tiles

                @pl.when(tiles > 0)
                def _(end=end):
                    action(zero_tile(end - 1))

            def tail(t, carry):
                action(zero_tile(t))
                return carry

            lax.fori_loop(end, n_tiles, tail, 0)

        each_zero_tile(lambda copy: copy.start())
        each_zero_tile(lambda copy: copy.wait())

    i = pl.program_id(0)
    last = pl.num_programs(0) - 1
    slot = i % 2

    def drain(s):
        for k in range(2):
            pltpu.make_async_copy(hn_ref.at[pl.ds(0, TM_SCAT)], xs_ref.at[pl.ds(0, TM_SCAT)], sems.at[s]).wait()

    base = i * TM_SCAT

    def issue(r, carry):
        for k in range(2):
            dst = pos_ref[k * n_tokens + base + r]
            pltpu.make_async_copy(hn_ref.at[pl.ds(base + r, 1)], xs_ref.at[pl.ds(dst, 1)], sems.at[slot]).start()
        return carry

    lax.fori_loop(0, TM_SCAT, issue, 0, unroll=8)

    @pl.when(i >= 1)
    def _():
        drain(1 - slot)

    @pl.when(i == last)
    def _():
        drain(slot)


def _scatter(pos, cnt, hn):
    n = hn.shape[0]
    rows = 2 * n + N_EXPERTS * TM_FFN
    return pl.pallas_call(
        functools.partial(_scatter_kernel, n_tokens=n),
        grid_spec=pltpu.PrefetchScalarGridSpec(
            num_scalar_prefetch=2,
            grid=(n // TM_SCAT,),
            in_specs=[pl.BlockSpec(memory_space=pl.ANY)],
            out_specs=pl.BlockSpec(memory_space=pl.ANY),
            scratch_shapes=[pltpu.VMEM((TM_FFN, D_MODEL), F32),
                            pltpu.SemaphoreType.DMA((2,)), pltpu.SemaphoreType.DMA(())],
        ),
        out_shape=jax.ShapeDtypeStruct((rows, D_MODEL), F32),
        compiler_params=_cparams(("arbitrary",)),
        name="moe_scatter",
    )(pos, cnt, hn)


def _combine_kernel(pos_ref, xn_ref, wcol_ref, ys_ref, o_ref, buf, sems, *, n_tokens):
    i = pl.program_id(0)
    slot = i % 2

    def fetch_tile(tile, s):
        base = tile * TM_COMB

        def issue(r, carry):
            for k in range(2):
                src = pos_ref[k * n_tokens + base + r]
                pltpu.make_async_copy(ys_ref.at[pl.ds(src, 1)], buf.at[s, k, pl.ds(r, 1)], sems.at[s]).start()
            return carry

        lax.fori_loop(0, TM_COMB, issue, 0, unroll=8)

    @pl.when(i == 0)
    def _():
        fetch_tile(0, 0)

    @pl.when(i + 1 < pl.num_programs(0))
    def _():
        fetch_tile(i + 1, 1 - slot)

    for k in range(2):
        pltpu.make_async_copy(ys_ref.at[pl.ds(0, TM_COMB)], buf.at[slot, k], sems.at[slot]).wait()
    w = wcol_ref[...]
    o_ref[...] = xn_ref[...] + (w[:, 0:1] * buf[slot, 0] + w[:, 1:2] * buf[slot, 1])


def _combine(pos, xn, wcol, ys):
    n = xn.shape[0]
    return pl.pallas_call(
        functools.partial(_combine_kernel, n_tokens=n),
        grid_spec=pltpu.PrefetchScalarGridSpec(
            num_scalar_prefetch=1,
            grid=(n // TM_COMB,),
            in_specs=[pl.BlockSpec((TM_COMB, D_MODEL), lambda i, p: (i, 0)),
                      pl.BlockSpec((TM_COMB, LANES), lambda i, p: (i, 0)),
                      pl.BlockSpec(memory_space=pl.ANY)],
            out_specs=pl.BlockSpec((TM_COMB, D_MODEL), lambda i, p: (i, 0)),
            scratch_shapes=[pltpu.VMEM((2, 2, TM_COMB, D_MODEL), F32), pltpu.SemaphoreType.DMA((2,))],
        ),
        out_shape=jax.ShapeDtypeStruct((n, D_MODEL), F32),
        compiler_params=_cparams(("arbitrary",)),
        name="moe_combine",
    )(pos, xn, wcol, ys)


def _swiglu(h, wg_ref, wu_ref, wd_ref):
    y = None
    c0 = 0
    for width in FF_CHUNKS:
        cs = slice(c0, c0 + width)
        g = jnp.dot(h, wg_ref[:, cs], preferred_element_type=F32)
        u = jnp.dot(h, wu_ref[:, cs], preferred_element_type=F32)
        act = (g * jax.nn.sigmoid(g) * u).astype(BF16)
        part = jnp.dot(act, wd_ref[cs, :], preferred_element_type=F32)
        y = part if y is None else y + part
        c0 += width
    return y


def _ffn_dense_kernel(h_ref, x_ref, wg_ref, wu_ref, wd_ref, o_ref):
    o_ref[...] = x_ref[...] + _swiglu(h_ref[...], wg_ref, wu_ref, wd_ref)


def _ffn_dense(h2, x2, wgu, wd):
    n = h2.shape[0]
    return pl.pallas_call(
        _ffn_dense_kernel,
        grid=(n // TM_FFN,),
        in_specs=[
            pl.BlockSpec((TM_FFN, D_MODEL), lambda i: (i, 0)),
            pl.BlockSpec((TM_FFN, D_MODEL), lambda i: (i, 0)),
            pl.BlockSpec((D_MODEL, FF), lambda i: (0, 0)),
            pl.BlockSpec((D_MODEL, FF), lambda i: (0, 1)),
            pl.BlockSpec((FF, D_MODEL), lambda i: (0, 0)),
        ],
        out_specs=pl.BlockSpec((TM_FFN, D_MODEL), lambda i: (i, 0)),
        out_shape=jax.ShapeDtypeStruct((n, D_MODEL), F32),
        compiler_params=_cparams(("parallel",)),
        name="ffn_dense",
    )(h2, x2, wgu, wgu, wd)


def _tile_plan(i, cnt_ref):
    end = jnp.int32(0)
    expert = jnp.int32(0)
    for e in range(N_EXPERTS):
        end = end + _round_up_tile(cnt_ref[e])
        expert = expert + (end <= i * TM_FFN).astype(jnp.int32)
    return jnp.minimum(expert, N_EXPERTS - 1), end // TM_FFN


def _ffn_grouped_kernel(cnt_ref, xs_ref, wg_ref, wu_ref, wd_ref, o_ref):
    i = pl.program_id(0)
    _, used = _tile_plan(i, cnt_ref)

    @pl.when(i < used)
    def _():
        o_ref[...] = _swiglu(xs_ref[...].astype(BF16), wg_ref, wu_ref, wd_ref)

    @pl.when(i >= used)
    def _():
        o_ref[...] = jnp.zeros_like(o_ref)


def _ffn_grouped(cnt, xs, wgu, wd):
    rows = xs.shape[0]

    def tile(i, c):
        return jnp.minimum(i, _tile_plan(i, c)[1] - 1)

    return pl.pallas_call(
        _ffn_grouped_kernel,
        grid_spec=pltpu.PrefetchScalarGridSpec(
            num_scalar_prefetch=1,
            grid=(rows // TM_FFN,),
            in_specs=[
                pl.BlockSpec((TM_FFN, D_MODEL), lambda i, c: (tile(i, c), 0)),
                pl.BlockSpec((None, D_MODEL, FF), lambda i, c: (_tile_plan(i, c)[0], 0, 0)),
                pl.BlockSpec((None, D_MODEL, FF), lambda i, c: (_tile_plan(i, c)[0], 0, 1)),
                pl.BlockSpec((None, FF, D_MODEL), lambda i, c: (_tile_plan(i, c)[0], 0, 0)),
            ],
            out_specs=pl.BlockSpec((TM_FFN, D_MODEL), lambda i, c: (i, 0)),
        ),
        out_shape=jax.ShapeDtypeStruct((rows, D_MODEL), F32),
        compiler_params=_cparams(("arbitrary",)),
        name="ffn_grouped",
    )(cnt, xs, wgu, wgu, wd)


def _moe(xn, imeta, wcol, cnt_lanes, hn, wgu, wd):
    n = xn.shape[0]
    cnt = cnt_lanes[:, 0]
    pos = _positions(cnt, imeta).reshape(2 * n)
    xs = _scatter(pos, cnt, hn)
    ys = _ffn_grouped(cnt, xs, wgu, wd)
    return _combine(pos, xn, wcol, ys)


def _rope_tables(seq):
    pos = jnp.arange(seq, dtype=F32)
    inv_freq = ROPE_THETA ** (-jnp.arange(0, QK_ROPE, 2, dtype=F32) / QK_ROPE)
    ang = pos[:, None] * inv_freq[None, :]
    return jnp.cos(ang), jnp.sin(ang)


def _swap_halves(w):
    half = QK_ROPE // 2
    return jnp.concatenate([w[..., half:], w[..., :half]], axis=-1)


def _head_slots(w, width):
    k = w.shape[0]
    w3 = w.reshape(k, MLA_HEADS, width)
    return jnp.pad(w3, ((0, 0), (0, 0), (0, HEAD_SLOT - width))).reshape(k, QK_WIDTH)


def _pad_lanes(v, width=LANES):
    return jnp.pad(v, (0, width - v.shape[0])).reshape(1, width)


def kernel(x, mix_norm_g, w_in, gm_v_norm_g, gm_v_norm_b, gm_w_spatial, gm_b_spatial, gm_w_proj, mla_q_lat_g, mla_w_uq, mla_kv_lat_g, mla_w_ukv, mla_q_norm_g, mla_k_norm_g, mla_w_proj, w_out, ffn_norm_g, dense_w_gu, dense_w_down, moe_w_router, moe_b_router, moe_w_gu, moe_w_down):
    batch, seq, d = x.shape
    n = batch * seq
    depth = w_in.shape[0]
    cos, sin = _rope_tables(seq)
    x2 = x.reshape(n, d)
    q_scale = float(np.log2(np.e) / np.sqrt(QK_DIM))

    o_kr = 2 * GM_WIDTH + Q_LORA + KV_LORA
    o_gate = o_kr + QK_ROPE

    ones_nope, zeros_nope = jnp.ones((seq, QK_NOPE), F32), jnp.zeros((seq, QK_NOPE), F32)
    cc, ss = jnp.concatenate([cos, cos], axis=1), jnp.concatenate([-sin, sin], axis=1)
    tq = jnp.concatenate([ones_nope, cc, ss], axis=1)
    ta = jnp.concatenate([zeros_nope, cc, cc], axis=1)
    tb = jnp.concatenate([zeros_nope, ss, ss], axis=1)

    for l in range(depth):
        wl = w_in[l]
        w_kr = wl[:, o_kr:o_gate]
        w_krx = jnp.pad(jnp.concatenate([w_kr, w_kr], axis=1), ((0, 0), (QK_NOPE, 0)))
        w_krs = jnp.pad(jnp.concatenate([_swap_halves(w_kr)] * 2, axis=1), ((0, 0), (QK_NOPE, 0)))
        w_kr2 = jnp.concatenate([w_krx, w_krs], axis=1).astype(BF16)
        wq3 = mla_w_uq[l].reshape(Q_LORA, MLA_HEADS, QK_DIM)
        wq = jnp.concatenate([wq3, _swap_halves(wq3[:, :, QK_NOPE:])], axis=2).reshape(Q_LORA, QK_WIDTH).astype(BF16)
        wkv3 = mla_w_ukv[l].reshape(KV_LORA, MLA_HEADS, QK_NOPE + V_DIM)
        wk = _head_slots(wkv3[:, :, :QK_NOPE].reshape(KV_LORA, MLA_HEADS * QK_NOPE), QK_NOPE).astype(BF16)
        wvt = wkv3[:, :, QK_NOPE:].reshape(KV_LORA, V_WIDTH).T.astype(BF16)
        gq, gk = mla_q_norm_g[l], mla_k_norm_g[l]
        g12 = lambda g: g[QK_NOPE:]
        g21 = lambda g: _swap_halves(g[QK_NOPE:])
        gq_row = (jnp.concatenate([gq[:QK_NOPE], g12(gq), g21(gq)]) * q_scale).reshape(1, HEAD_SLOT)
        ga_row = _pad_lanes(jnp.concatenate([jnp.zeros((QK_NOPE,), F32), g12(gk), g12(gk)]))
        gb_row = _pad_lanes(jnp.concatenate([jnp.zeros((QK_NOPE,), F32), g21(gk), g21(gk)]))
        uv, gates, q, k, vt = _in_qkv(x2, mix_norm_g[l].reshape(1, d),
                                      wl[:, :o_kr].astype(BF16), w_kr2, wl[:, o_gate:].astype(BF16),
                                      mla_q_lat_g[l].reshape(1, Q_LORA), wq,
                                      mla_kv_lat_g[l].reshape(1, KV_LORA), wk, wvt,
                                      tq, ta, tb, gq_row, ga_row, gb_row, _pad_lanes(gk[:QK_NOPE]), batch, seq)
        slots = lambda t: t.reshape(batch, seq, QK_WIDTH)
        ot = _attention(slots(q), slots(k), vt).reshape(n, V_WIDTH)

        is_moe = l % 2 == 1
        m = l // 2
        router = {}
        if is_moe:
            router = dict(wr=jnp.pad(moe_w_router[m], ((0, 0), (0, LANES - N_EXPERTS))),
                          br=_pad_lanes(moe_b_router[m]))
        outs = _merge(uv, gm_v_norm_g[l].reshape(1, GM_WIDTH), gm_v_norm_b[l].reshape(1, GM_WIDTH),
                      gm_w_spatial[l], gm_b_spatial[l].T, ot, gates, x2,
                      gm_w_proj[l].astype(BF16), mla_w_proj[l].astype(BF16),
                      w_out[l].astype(BF16), ffn_norm_g[l].reshape(1, d), **router)
        if is_moe:
            xn, imeta, wcol, cnt_lanes, hn = outs
            x2 = _moe(xn, imeta, wcol, cnt_lanes, hn, moe_w_gu[m].astype(BF16), moe_w_down[m].astype(BF16))
        else:
            xn, h2 = outs
            x2 = _ffn_dense(h2, xn, dense_w_gu[m].astype(BF16), dense_w_down[m].astype(BF16))
    return x2.reshape(batch, seq, d)
```

```python
import functools

import jax
import jax.numpy as jnp
import numpy as np
from jax import lax
from jax.experimental import pallas as pl
from jax.experimental.pallas import tpu as pltpu

F32 = jnp.float32
BF16 = jnp.bfloat16

EPS = 1e-6
LANES = 128

D_MODEL = 1024
GM_GROUPS = 8
GM_GROUP_CH = 128
GM_WIDTH = 1024
GM_CHUNK = 128
MLA_HEADS = 16
QK_NOPE = 64
QK_ROPE = 32
QK_DIM = 96
V_DIM = 64
Q_LORA = 512
KV_LORA = 256
ROPE_THETA = 10000.0
HEAD_SLOT = LANES
QK_WIDTH = MLA_HEADS * HEAD_SLOT
V_WIDTH = MLA_HEADS * V_DIM
N_EXPERTS = 8
FF = 2816

TM_IN = 256
TM_MERGE = 512
TM_FFN = 512
FF_CHUNKS = (768, 768, 768, 512)
TM_POS = 2048
TM_SCAT = 512
TM_COMB = 512
ATT_T = 512
ATT_G = 2
ATT_SUM_ROWS = 16

VMEM_LIMIT = 56 * 1024 * 1024


def _cparams(sem):
    return pltpu.CompilerParams(dimension_semantics=sem, vmem_limit_bytes=VMEM_LIMIT)


def _rms(xf, g):
    return xf * lax.rsqrt(jnp.mean(xf * xf, axis=-1, keepdims=True) + EPS) * g


def _in_qkv_kernel(x_ref, g_ref, w_ref, wkr_ref, wgate_ref, qg_ref, wq_ref, kvg_ref, wk_ref, wvt_ref,
                   tq_ref, ta_ref, tb_ref, gq_ref, ga_ref, gb_ref, gkn_ref,
                   uv_ref, gate_ref, q_ref, k_ref, vt_ref):
    h = _rms(x_ref[...], g_ref[...]).astype(BF16)
    o_lat = 2 * GM_WIDTH
    lat = jnp.dot(h, w_ref[:, o_lat:], preferred_element_type=F32)
    krx = jnp.dot(h, wkr_ref[...], preferred_element_type=F32)
    uv_ref[...] = _gelu_tanh(jnp.dot(h, w_ref[:, :o_lat], preferred_element_type=F32)).astype(BF16)
    gate_ref[...] = jax.nn.sigmoid(jnp.dot(h, wgate_ref[...], preferred_element_type=F32)).astype(BF16)

    cq = _rms(lat[:, :Q_LORA], qg_ref[...]).astype(BF16)
    ckv = _rms(lat[:, Q_LORA:], kvg_ref[...]).astype(BF16)
    kr = krx[:, :LANES]
    kr_sw = krx[:, LANES:]
    kr_ssq = 0.5 * jnp.sum(kr * kr, axis=-1, keepdims=True)
    kr_rot = kr * (ta_ref[...] * ga_ref[...]) + kr_sw * (tb_ref[...] * gb_ref[...])
    tq, gkn = tq_ref[...] * gq_ref[...], gkn_ref[...]
    q_all = jnp.dot(cq, wq_ref[...], preferred_element_type=F32)
    k_all = jnp.dot(ckv, wk_ref[...], preferred_element_type=F32)
    first_copy = lax.broadcasted_iota(jnp.int32, (1, HEAD_SLOT), 1) < QK_DIM
    for hd in range(MLA_HEADS):
        sl = slice(hd * HEAD_SLOT, (hd + 1) * HEAD_SLOT)
        qh = q_all[:, sl]
        ssq = jnp.sum(qh * jnp.where(first_copy, qh, 0.0), axis=-1, keepdims=True)
        q_ref[:, sl] = (qh * tq * lax.rsqrt(ssq * (1.0 / QK_DIM) + EPS)).astype(BF16)
        kh = k_all[:, sl]
        rk = lax.rsqrt((jnp.sum(kh * kh, axis=-1, keepdims=True) + kr_ssq) * (1.0 / QK_DIM) + EPS)
        k_ref[:, sl] = ((kh * gkn + kr_rot) * rk).astype(BF16)
    vt_ref[...] = lax.dot_general(wvt_ref[...], ckv, (((1,), (1,)), ((), ())),
                                  preferred_element_type=F32).astype(BF16)


def _in_qkv(x2, g, w_main, w_kr, w_gate, qg, wq, kvg, wk, wvt, tq, ta, tb, gq, ga, gb, gkn, batch, seq):
    n = x2.shape[0]
    tps = seq // TM_IN
    full = lambda shape: pl.BlockSpec(shape, lambda i: (0,) * len(shape), pipeline_mode=pl.Buffered(1))
    rope = pl.BlockSpec((TM_IN, HEAD_SLOT), lambda i: (i % tps, 0))
    row = lambda w: pl.BlockSpec((TM_IN, w), lambda i: (i, 0))
    lane_row = full((1, HEAD_SLOT))
    return pl.pallas_call(
        _in_qkv_kernel,
        grid=(n // TM_IN,),
        in_specs=[
            row(D_MODEL), full((1, D_MODEL)), full(w_main.shape), full(w_kr.shape), full(w_gate.shape),
            full((1, Q_LORA)), full((Q_LORA, QK_WIDTH)),
            full((1, KV_LORA)), full((KV_LORA, QK_WIDTH)), full((V_WIDTH, KV_LORA)),
            rope, rope, rope, lane_row, lane_row, lane_row, lane_row,
        ],
        out_specs=[row(2 * GM_WIDTH), row(2 * D_MODEL), row(QK_WIDTH), row(QK_WIDTH),
                   pl.BlockSpec((None, None, V_WIDTH, TM_IN), lambda i: (i // tps, i % tps, 0, 0))],
        out_shape=[jax.ShapeDtypeStruct((n, 2 * GM_WIDTH), BF16), jax.ShapeDtypeStruct((n, 2 * D_MODEL), BF16),
                   jax.ShapeDtypeStruct((n, QK_WIDTH), BF16), jax.ShapeDtypeStruct((n, QK_WIDTH), BF16),
                   jax.ShapeDtypeStruct((batch, tps, V_WIDTH, TM_IN), BF16)],
        compiler_params=_cparams(("parallel",)),
        name="in_qkv",
    )(x2, g, w_main, w_kr, w_gate, qg, wq, kvg, wk, wvt, tq, ta, tb, gq, ga, gb, gkn)


def _attention_kernel(q_ref, k_ref, vt_ref, o_ref, *, seq):
    nblk = seq // ATT_T
    vblocks = ATT_T // TM_IN
    key_pos = lax.broadcasted_iota(jnp.int32, (ATT_T, ATT_T), 0)
    qry_pos = lax.broadcasted_iota(jnp.int32, (ATT_T, ATT_T), 1)
    causal = key_pos <= qry_pos
    nt = (((1,), (1,)), ((), ()))
    ones_rows = jnp.ones((ATT_SUM_ROWS, ATT_T), BF16)

    def update(state, g, st, kj, diagonal):
        m, acc = state
        if diagonal:
            st = jnp.where(causal, st, -1e30)
        m_new = jnp.maximum(m, jnp.max(st, axis=0, keepdims=True))
        alpha = jnp.exp2(m - m_new)
        p = jnp.exp2((st - m_new).astype(BF16))
        vt = [vt_ref[kj * vblocks + b, g * V_DIM:(g + 1) * V_DIM, :] for b in range(vblocks)]
        v_aug = jnp.concatenate([jnp.concatenate(vt, axis=1), ones_rows], axis=0)
        return m_new, alpha * acc + jnp.dot(v_aug, p, preferred_element_type=F32)

    for qi in range(nblk):
        q0, q1 = qi * ATT_T, (qi + 1) * ATT_T
        strips = [lax.dot_general(k_ref[0:q1, g * HEAD_SLOT:(g + 1) * HEAD_SLOT],
                                  q_ref[q0:q1, g * HEAD_SLOT:(g + 1) * HEAD_SLOT], nt,
                                  preferred_element_type=F32) for g in range(ATT_G)]
        acc_rows = V_DIM + ATT_SUM_ROWS
        init = (jnp.full((1, ATT_T), -1e30, F32), jnp.zeros((acc_rows, ATT_T), F32))
        states = [init for _ in range(ATT_G)]
        for kj in range(qi + 1):
            states = [update(states[g], g, strips[g][kj * ATT_T:(kj + 1) * ATT_T], kj, kj == qi)
                      for g in range(ATT_G)]
        pad_rows = jnp.zeros((HEAD_SLOT - acc_rows, ATT_T), F32)
        for g in range(ATT_G):
            acc = jnp.concatenate([states[g][1], pad_rows], axis=0).T
            o_ref[q0:q1, g * V_DIM:(g + 1) * V_DIM] = (acc[:, :V_DIM] / acc[:, V_DIM:V_DIM + 1]).astype(BF16)


def _attention(q3, k3, vt4):
    batch, seq, _ = q3.shape
    slot = pl.BlockSpec((None, seq, ATT_G * HEAD_SLOT), lambda b, h: (b, 0, h))
    return pl.pallas_call(
        functools.partial(_attention_kernel, seq=seq),
        grid=(batch, MLA_HEADS // ATT_G),
        in_specs=[slot, slot,
                  pl.BlockSpec((None, seq // TM_IN, ATT_G * V_DIM, TM_IN), lambda b, h: (b, 0, h, 0))],
        out_specs=pl.BlockSpec((None, seq, ATT_G * V_DIM), lambda b, h: (b, 0, h)),
        out_shape=jax.ShapeDtypeStruct((batch, seq, V_WIDTH), BF16),
        compiler_params=_cparams(("parallel", "parallel")),
        name="attention",
    )(q3, k3, vt4)


def _split_bf16(v):
    hi = v.astype(BF16)
    return hi, (v - hi.astype(F32)).astype(BF16)


def _gelu_tanh(x):
    c1 = -2.0 * float(np.sqrt(2.0 / np.pi) * np.log2(np.e))
    c2 = c1 * 0.044715
    return x / (1.0 + jnp.exp2(x * (c1 + c2 * (x * x))))


def _gmlp_tile(uv_ref, vg_ref, vb_ref, ws_ref, bst_ref, a_ref):
    v = uv_ref[:, GM_WIDTH:].astype(F32)
    mu = jnp.mean(v, axis=-1, keepdims=True)
    vc = v - mu
    vn = (vc * lax.rsqrt(jnp.mean(vc * vc, axis=-1, keepdims=True) + EPS) * vg_ref[...] + vb_ref[...]).astype(BF16)
    row = lax.broadcasted_iota(jnp.int32, (GM_CHUNK, GM_CHUNK), 0)
    col = lax.broadcasted_iota(jnp.int32, (GM_CHUNK, GM_CHUNK), 1)
    tril = col <= row
    bst = bst_ref[...]
    for g in range(GM_GROUPS):
        ws = jnp.where(tril, ws_ref[g], 0.0).astype(BF16)
        bias = bst[:, g:g + 1]
        cs = slice(g * GM_GROUP_CH, (g + 1) * GM_GROUP_CH)
        for c in range(TM_MERGE // GM_CHUNK):
            rs = slice(c * GM_CHUNK, (c + 1) * GM_CHUNK)
            mixed = jnp.dot(ws, vn[rs, cs], preferred_element_type=F32) + bias
            a_ref[rs, cs] = (uv_ref[rs, cs].astype(F32) * mixed).astype(BF16)


def _merge_kernel(uv_ref, vg_ref, vb_ref, ws_ref, bst_ref, ot_ref, gate_ref, x_ref, wa_ref, wb_ref, wo_ref,
                  fg_ref, *rest, with_router):
    if with_router:
        wr_ref, br_ref, xn_ref, imeta_ref, wcol_ref, cnt_ref, a_ref, run_ref = rest
    else:
        xn_ref, a_ref = rest
    yb = jnp.dot(ot_ref[...], wb_ref[...], preferred_element_type=F32)
    _gmlp_tile(uv_ref, vg_ref, vb_ref, ws_ref, bst_ref, a_ref)
    ya = jnp.dot(a_ref[...], wa_ref[...], preferred_element_type=F32)
    ga = gate_ref[:, :D_MODEL].astype(F32)
    gb = gate_ref[:, D_MODEL:].astype(F32)
    merged = (ga * ya + gb * yb).astype(BF16)
    xn = x_ref[...] + jnp.dot(merged, wo_ref[...], preferred_element_type=F32)
    xn_ref[...] = xn
    if with_router:
        h2 = _rms(xn, fg_ref[...])
        h_hi, h_lo = _split_bf16(h2)
        w_hi, w_lo = _split_bf16(wr_ref[...])
        hh_hl = jnp.dot(h_hi, jnp.concatenate([w_hi, w_lo], axis=1), preferred_element_type=F32)
        logits = hh_hl[:, :LANES] + (hh_hl[:, LANES:] + jnp.dot(h_lo, w_hi, preferred_element_type=F32))
        _route_tile(logits + br_ref[...], run_ref, imeta_ref, wcol_ref, cnt_ref)


def _merge(uv, vg, vb, ws, bst, ot, gates, x2, wa, wb, wo, fg, wr=None, br=None):
    n = x2.shape[0]
    with_router = wr is not None
    full = lambda shape: pl.BlockSpec(shape, lambda i: (0,) * len(shape))
    row = lambda w: pl.BlockSpec((TM_MERGE, w), lambda i: (i, 0))
    in_specs = [
        row(2 * GM_WIDTH), full((1, GM_WIDTH)), full((1, GM_WIDTH)),
        full((GM_GROUPS, GM_CHUNK, GM_CHUNK)), full((GM_CHUNK, GM_GROUPS)),
        row(V_WIDTH), row(2 * D_MODEL), row(D_MODEL),
        full((GM_WIDTH, D_MODEL)), full((V_WIDTH, D_MODEL)), full((D_MODEL, D_MODEL)), full((1, D_MODEL)),
    ]
    args = [uv, vg, vb, ws, bst, ot, gates, x2, wa, wb, wo, fg]
    scratch = [pltpu.VMEM((TM_MERGE, GM_WIDTH), BF16)]
    if with_router:
        in_specs += [full((D_MODEL, LANES)), full((1, LANES))]
        out_specs = [row(D_MODEL), pl.BlockSpec((N_EXPERTS, TM_MERGE), lambda i: (0, i)), row(LANES),
                     full((N_EXPERTS, LANES))]
        out_shape = [jax.ShapeDtypeStruct((n, D_MODEL), F32), jax.ShapeDtypeStruct((N_EXPERTS, n), jnp.int32),
                     jax.ShapeDtypeStruct((n, LANES), F32), jax.ShapeDtypeStruct((N_EXPERTS, LANES), jnp.int32)]
        args += [wr, br]
        scratch.append(pltpu.VMEM((N_EXPERTS, LANES), F32))
    else:
        out_specs = [row(D_MODEL)]
        out_shape = [jax.ShapeDtypeStruct((n, D_MODEL), F32)]
    return pl.pallas_call(
        functools.partial(_merge_kernel, with_router=with_router),
        grid=(n // TM_MERGE,),
        in_specs=in_specs, out_specs=out_specs, out_shape=out_shape,
        scratch_shapes=scratch,
        compiler_params=_cparams(("arbitrary",) if with_router else ("parallel",)),
        name="merge_router" if with_router else "merge",
    )(*args)


def _round_up_tile(c):
    return (c + (TM_FFN - 1)) & (-TM_FFN)


def _route_tile(logits, run_ref, imeta_ref, wcol_ref, cnt_ref):
    tm = logits.shape[0]

    @pl.when(pl.program_id(0) == 0)
    def _():
        run_ref[...] = jnp.zeros_like(run_ref)

    lt = logits.T[:N_EXPERTS, :]
    sub = lax.broadcasted_iota(jnp.int32, lt.shape, 0)
    m1 = jnp.max(lt, axis=0, keepdims=True)
    i1 = jnp.min(jnp.where(lt == m1, sub, N_EXPERTS), axis=0, keepdims=True)
    rest = jnp.where(sub == i1, -jnp.inf, lt)
    m2 = jnp.max(rest, axis=0, keepdims=True)
    i2 = jnp.min(jnp.where(rest == m2, sub, N_EXPERTS), axis=0, keepdims=True)
    e2 = jnp.exp(m2 - m1)
    w1 = 1.0 / (1.0 + e2)
    w2 = e2 / (1.0 + e2)

    onehot = jnp.where((sub == i1) | (sub == i2), 1.0, 0.0)
    src = lax.broadcasted_iota(jnp.int32, (tm, tm), 0)
    dst = lax.broadcasted_iota(jnp.int32, (tm, tm), 1)
    earlier = jnp.where(src < dst, 1.0, 0.0).astype(BF16)
    seen = jnp.dot(onehot.astype(BF16), earlier, preferred_element_type=F32) + run_ref[:, :1]
    r1 = jnp.sum(jnp.where(sub == i1, seen, 0.0), axis=0, keepdims=True).astype(jnp.int32)
    r2 = jnp.sum(jnp.where(sub == i2, seen, 0.0), axis=0, keepdims=True).astype(jnp.int32)
    imeta_ref[...] = jnp.where(sub == 0, i1, jnp.where(sub == 1, i2, jnp.where(sub == 2, r1, jnp.where(sub == 3, r2, 0))))

    total = run_ref[...] + jnp.sum(onehot, axis=1, keepdims=True)
    run_ref[...] = total
    cnt_ref[...] = total.astype(jnp.int32)

    sub_w = lax.broadcasted_iota(jnp.int32, (LANES, tm), 0)
    wcol_ref[...] = jnp.where(sub_w == 0, w1, jnp.where(sub_w == 1, w2, 0.0)).T


def _positions_kernel(cnt_ref, imeta_ref, pos_ref):
    im = imeta_ref[...]
    i1, i2, r1, r2 = im[0:1], im[1:2], im[2:3], im[3:4]
    start = jnp.int32(0)
    p1 = jnp.zeros_like(i1)
    p2 = jnp.zeros_like(i2)
    for e in range(N_EXPERTS):
        p1 = jnp.where(i1 == e, start, p1)
        p2 = jnp.where(i2 == e, start, p2)
        start = start + _round_up_tile(cnt_ref[e])
    pos_ref[...] = jnp.concatenate([p1 + r1, p2 + r2], axis=0)


def _positions(cnt, imeta):
    n = imeta.shape[1]
    return pl.pallas_call(
        _positions_kernel,
        grid_spec=pltpu.PrefetchScalarGridSpec(
            num_scalar_prefetch=1,
            grid=(n // TM_POS,),
            in_specs=[pl.BlockSpec((N_EXPERTS, TM_POS), lambda i, c: (0, i))],
            out_specs=pl.BlockSpec((2, TM_POS), lambda i, c: (0, i)),
        ),
        out_shape=jax.ShapeDtypeStruct((2, n), jnp.int32),
        compiler_params=_cparams(("parallel",)),
        name="positions",
    )(cnt, imeta)


def _scatter_kernel(pos_ref, cnt_ref, xn_ref, fg_ref, xs_ref, hbuf, zbuf, sems, zsem, *, n_tokens):
    n_tiles = xs_ref.shape[0] // TM_FFN

    @pl.when(pl.program_id(0) == 0)
    def _():
        zbuf[...] = jnp.zeros_like(zbuf)

        def zero_tile(t):
            rows = pl.ds(pl.multiple_of(t * TM_FFN, TM_FFN), TM_FFN)
            return pltpu.make_async_copy(zbuf, xs_ref.at[rows], zsem)

        def each_zero_tile(action):
            end = jnp.int32(0)
            for e in range(N_EXPERTS):
                tiles = _round_up_tile(cnt_ref[e]) // TM_FFN
                end = end + tiles

                @pl.when(tiles > 0)
                def _(end=end):
                    action(zero_tile(end - 1))

            def tail(t, carry):
                action(zero_tile(t))
                return carry

            lax.fori_loop(end, n_tiles, tail, 0)

        each_zero_tile(lambda copy: copy.start())
        each_zero_tile(lambda copy: copy.wait())

    i = pl.program_id(0)
    last = pl.num_programs(0) - 1
    slot = i % 2

    def drain(s):
        for k in range(2):
            pltpu.make_async_copy(hbuf.at[s], xs_ref.at[pl.ds(0, TM_SCAT)], sems.at[s]).wait()

    @pl.when(i >= 2)
    def _():
        drain(slot)

    base = i * TM_SCAT
    hbuf[slot] = _rms(xn_ref[...], fg_ref[...])

    def issue(r, carry):
        for k in range(2):
            dst = pos_ref[k * n_tokens + base + r]
            pltpu.make_async_copy(hbuf.at[slot, pl.ds(r, 1)], xs_ref.at[pl.ds(dst, 1)], sems.at[slot]).start()
        return carry

    lax.fori_loop(0, TM_SCAT, issue, 0, unroll=8)

    @pl.when(i == last)
    def _():
        drain(1 - slot)
        drain(slot)


def _scatter(pos, cnt, xn, fg):
    n = xn.shape[0]
    assert n // TM_SCAT >= 2
    rows = 2 * n + N_EXPERTS * TM_FFN
    return pl.pallas_call(
        functools.partial(_scatter_kernel, n_tokens=n),
        grid_spec=pltpu.PrefetchScalarGridSpec(
            num_scalar_prefetch=2,
            grid=(n // TM_SCAT,),
            in_specs=[pl.BlockSpec((TM_SCAT, D_MODEL), lambda i, p, c: (i, 0)),
                      pl.BlockSpec((1, D_MODEL), lambda i, p, c: (0, 0))],
            out_specs=pl.BlockSpec(memory_space=pl.ANY),
            scratch_shapes=[pltpu.VMEM((2, TM_SCAT, D_MODEL), F32), pltpu.VMEM((TM_FFN, D_MODEL), F32),
                            pltpu.SemaphoreType.DMA((2,)), pltpu.SemaphoreType.DMA(())],
        ),
        out_shape=jax.ShapeDtypeStruct((rows, D_MODEL), F32),
        compiler_params=_cparams(("arbitrary",)),
        name="moe_scatter",
    )(pos, cnt, xn, fg)


def _combine_kernel(pos_ref, xn_ref, wcol_ref, ys_ref, o_ref, buf, sems, *, n_tokens):
    i = pl.program_id(0)
    slot = i % 2

    def fetch_tile(tile, s):
        base = tile * TM_COMB

        def issue(r, carry):
            for k in range(2):
                src = pos_ref[k * n_tokens + base + r]
                pltpu.make_async_copy(ys_ref.at[pl.ds(src, 1)], buf.at[s, k, pl.ds(r, 1)], sems.at[s]).start()
            return carry

        lax.fori_loop(0, TM_COMB, issue, 0, unroll=8)

    @pl.when(i == 0)
    def _():
        fetch_tile(0, 0)

    @pl.when(i + 1 < pl.num_programs(0))
    def _():
        fetch_tile(i + 1, 1 - slot)

    for k in range(2):
        pltpu.make_async_copy(ys_ref.at[pl.ds(0, TM_COMB)], buf.at[slot, k], sems.at[slot]).wait()
    w = wcol_ref[...]
    o_ref[...] = xn_ref[...] + (w[:, 0:1] * buf[slot, 0] + w[:, 1:2] * buf[slot, 1])


def _combine(pos, xn, wcol, ys):
    n = xn.shape[0]
    return pl.pallas_call(
        functools.partial(_combine_kernel, n_tokens=n),
        grid_spec=pltpu.PrefetchScalarGridSpec(
            num_scalar_prefetch=1,
            grid=(n // TM_COMB,),
            in_specs=[pl.BlockSpec((TM_COMB, D_MODEL), lambda i, p: (i, 0)),
                      pl.BlockSpec((TM_COMB, LANES), lambda i, p: (i, 0)),
                      pl.BlockSpec(memory_space=pl.ANY)],
            out_specs=pl.BlockSpec((TM_COMB, D_MODEL), lambda i, p: (i, 0)),
            scratch_shapes=[pltpu.VMEM((2, 2, TM_COMB, D_MODEL), F32), pltpu.SemaphoreType.DMA((2,))],
        ),
        out_shape=jax.ShapeDtypeStruct((n, D_MODEL), F32),
        compiler_params=_cparams(("arbitrary",)),
        name="moe_combine",
    )(pos, xn, wcol, ys)


def _swiglu(h, wg_ref, wu_ref, wd_ref):
    y = None
    c0 = 0
    for width in FF_CHUNKS:
        cs = slice(c0, c0 + width)
        g = jnp.dot(h, wg_ref[:, cs], preferred_element_type=F32)
        u = jnp.dot(h, wu_ref[:, cs], preferred_element_type=F32)
        act = (g * jax.nn.sigmoid(g) * u).astype(BF16)
        part = jnp.dot(act, wd_ref[cs, :], preferred_element_type=F32)
        y = part if y is None else y + part
        c0 += width
    return y


def _ffn_dense_kernel(x_ref, fg_ref, wg_ref, wu_ref, wd_ref, o_ref):
    x = x_ref[...]
    o_ref[...] = x + _swiglu(_rms(x, fg_ref[...]).astype(BF16), wg_ref, wu_ref, wd_ref)


def _ffn_dense(x2, fg, wgu, wd):
    n = x2.shape[0]
    return pl.pallas_call(
        _ffn_dense_kernel,
        grid=(n // TM_FFN,),
        in_specs=[
            pl.BlockSpec((TM_FFN, D_MODEL), lambda i: (i, 0)),
            pl.BlockSpec((1, D_MODEL), lambda i: (0, 0)),
            pl.BlockSpec((D_MODEL, FF), lambda i: (0, 0)),
            pl.BlockSpec((D_MODEL, FF), lambda i: (0, 1)),
            pl.BlockSpec((FF, D_MODEL), lambda i: (0, 0)),
        ],
        out_specs=pl.BlockSpec((TM_FFN, D_MODEL), lambda i: (i, 0)),
        out_shape=jax.ShapeDtypeStruct((n, D_MODEL), F32),
        compiler_params=_cparams(("parallel",)),
        name="ffn_dense",
    )(x2, fg, wgu, wgu, wd)


def _tile_plan(i, cnt_ref):
    end = jnp.int32(0)
    expert = jnp.int32(0)
    for e in range(N_EXPERTS):
        end = end + _round_up_tile(cnt_ref[e])
        expert = expert + (end <= i * TM_FFN).astype(jnp.int32)
    return jnp.minimum(expert, N_EXPERTS - 1), end // TM_FFN


def _ffn_grouped_kernel(cnt_ref, xs_ref, wg_ref, wu_ref, wd_ref, o_ref):
    i = pl.program_id(0)
    _, used = _tile_plan(i, cnt_ref)

    @pl.when(i < used)
    def _():
        o_ref[...] = _swiglu(xs_ref[...].astype(BF16), wg_ref, wu_ref, wd_ref)

    @pl.when(i >= used)
    def _():
        o_ref[...] = jnp.zeros_like(o_ref)


def _ffn_grouped(cnt, xs, wgu, wd):
    rows = xs.shape[0]

    def tile(i, c):
        return jnp.minimum(i, _tile_plan(i, c)[1] - 1)

    return pl.pallas_call(
        _ffn_grouped_kernel,
        grid_spec=pltpu.PrefetchScalarGridSpec(
            num_scalar_prefetch=1,
            grid=(rows // TM_FFN,),
            in_specs=[
                pl.BlockSpec((TM_FFN, D_MODEL), lambda i, c: (tile(i, c), 0)),
                pl.BlockSpec((None, D_MODEL, FF), lambda i, c: (_tile_plan(i, c)[0], 0, 0)),
                pl.BlockSpec((None, D_MODEL, FF), lambda i, c: (_tile_plan(i, c)[0], 0, 1)),
                pl.BlockSpec((None, FF, D_MODEL), lambda i, c: (_tile_plan(i, c)[0], 0, 0)),
            ],
            out_specs=pl.BlockSpec((TM_FFN, D_MODEL), lambda i, c: (i, 0)),
        ),
        out_shape=jax.ShapeDtypeStruct((rows, D_MODEL), F32),
        compiler_params=_cparams(("arbitrary",)),
        name="ffn_grouped",
    )(cnt, xs, wgu, wgu, wd)


def _moe(xn, imeta, wcol, cnt_lanes, fg, wgu, wd):
    n = xn.shape[0]
    cnt = cnt_lanes[:, 0]
    pos = _positions(cnt, imeta).reshape(2 * n)
    xs = _scatter(pos, cnt, xn, fg)
    ys = _ffn_grouped(cnt, xs, wgu, wd)
    return _combine(pos, xn, wcol, ys)


def _rope_tables(seq):
    pos = jnp.arange(seq, dtype=F32)
    inv_freq = ROPE_THETA ** (-jnp.arange(0, QK_ROPE, 2, dtype=F32) / QK_ROPE)
    ang = pos[:, None] * inv_freq[None, :]
    return jnp.cos(ang), jnp.sin(ang)


def _swap_halves(w):
    half = QK_ROPE // 2
    return jnp.concatenate([w[..., half:], w[..., :half]], axis=-1)


def _head_slots(w, width):
    k = w.shape[0]
    w3 = w.reshape(k, MLA_HEADS, width)
    return jnp.pad(w3, ((0, 0), (0, 0), (0, HEAD_SLOT - width))).reshape(k, QK_WIDTH)


def _pad_lanes(v, width=LANES):
    return jnp.pad(v, (0, width - v.shape[0])).reshape(1, width)


def kernel(x, mix_norm_g, w_in, gm_v_norm_g, gm_v_norm_b, gm_w_spatial, gm_b_spatial, gm_w_proj, mla_q_lat_g, mla_w_uq, mla_kv_lat_g, mla_w_ukv, mla_q_norm_g, mla_k_norm_g, mla_w_proj, w_out, ffn_norm_g, dense_w_gu, dense_w_down, moe_w_router, moe_b_router, moe_w_gu, moe_w_down):
    batch, seq, d = x.shape
    n = batch * seq
    depth = w_in.shape[0]
    cos, sin = _rope_tables(seq)
    x2 = x.reshape(n, d)
    q_scale = float(np.log2(np.e) / np.sqrt(QK_DIM))

    o_kr = 2 * GM_WIDTH + Q_LORA + KV_LORA
    o_gate = o_kr + QK_ROPE

    ones_nope, zeros_nope = jnp.ones((seq, QK_NOPE), F32), jnp.zeros((seq, QK_NOPE), F32)
    cc, ss = jnp.concatenate([cos, cos], axis=1), jnp.concatenate([-sin, sin], axis=1)
    tq = jnp.concatenate([ones_nope, cc, ss], axis=1)
    ta = jnp.concatenate([zeros_nope, cc, cc], axis=1)
    tb = jnp.concatenate([zeros_nope, ss, ss], axis=1)

    for l in range(depth):
        wl = w_in[l]
        w_kr = wl[:, o_kr:o_gate]
        w_krx = jnp.pad(jnp.concatenate([w_kr, w_kr], axis=1), ((0, 0), (QK_NOPE, 0)))
        w_krs = jnp.pad(jnp.concatenate([_swap_halves(w_kr)] * 2, axis=1), ((0, 0), (QK_NOPE, 0)))
        w_kr2 = jnp.concatenate([w_krx, w_krs], axis=1).astype(BF16)
        wq3 = mla_w_uq[l].reshape(Q_LORA, MLA_HEADS, QK_DIM)
        wq = jnp.concatenate([wq3, _swap_halves(wq3[:, :, QK_NOPE:])], axis=2).reshape(Q_LORA, QK_WIDTH).astype(BF16)
        wkv3 = mla_w_ukv[l].reshape(KV_LORA, MLA_HEADS, QK_NOPE + V_DIM)
        wk = _head_slots(wkv3[:, :, :QK_NOPE].reshape(KV_LORA, MLA_HEADS * QK_NOPE), QK_NOPE).astype(BF16)
        wvt = wkv3[:, :, QK_NOPE:].reshape(KV_LORA, V_WIDTH).T.astype(BF16)
        gq, gk = mla_q_norm_g[l], mla_k_norm_g[l]
        g12 = lambda g: g[QK_NOPE:]
        g21 = lambda g: _swap_halves(g[QK_NOPE:])
        gq_row = (jnp.concatenate([gq[:QK_NOPE], g12(gq), g21(gq)]) * q_scale).reshape(1, HEAD_SLOT)
        ga_row = _pad_lanes(jnp.concatenate([jnp.zeros((QK_NOPE,), F32), g12(gk), g12(gk)]))
        gb_row = _pad_lanes(jnp.concatenate([jnp.zeros((QK_NOPE,), F32), g21(gk), g21(gk)]))
        uv, gates, q, k, vt = _in_qkv(x2, mix_norm_g[l].reshape(1, d),
                                      wl[:, :o_kr].astype(BF16), w_kr2, wl[:, o_gate:].astype(BF16),
                                      mla_q_lat_g[l].reshape(1, Q_LORA), wq,
                                      mla_kv_lat_g[l].reshape(1, KV_LORA), wk, wvt,
                                      tq, ta, tb, gq_row, ga_row, gb_row, _pad_lanes(gk[:QK_NOPE]), batch, seq)
        slots = lambda t: t.reshape(batch, seq, QK_WIDTH)
        ot = _attention(slots(q), slots(k), vt).reshape(n, V_WIDTH)

        is_moe = l % 2 == 1
        m = l // 2
        router = {}
        if is_moe:
            router = dict(wr=jnp.pad(moe_w_router[m], ((0, 0), (0, LANES - N_EXPERTS))),
                          br=_pad_lanes(moe_b_router[m]))
        outs = _merge(uv, gm_v_norm_g[l].reshape(1, GM_WIDTH), gm_v_norm_b[l].reshape(1, GM_WIDTH),
                      gm_w_spatial[l], gm_b_spatial[l].T, ot, gates, x2,
                      gm_w_proj[l].astype(BF16), mla_w_proj[l].astype(BF16),
                      w_out[l].astype(BF16), ffn_norm_g[l].reshape(1, d), **router)
        if is_moe:
            xn, imeta, wcol, cnt_lanes = outs
            x2 = _moe(xn, imeta, wcol, cnt_lanes, ffn_norm_g[l].reshape(1, d),
                      moe_w_gu[m].astype(BF16), moe_w_down[m].astype(BF16))
        else:
            (xn,) = outs
            x2 = _ffn_dense(xn, ffn_norm_g[l].reshape(1, d), dense_w_gu[m].astype(BF16), dense_w_down[m].astype(BF16))
    return x2.reshape(batch, seq, d)
```
